```python
import jax, jax.numpy as jnp
from jax import lax
import numpy as np

D_MODEL = 1024
BATCH = 4
SEQ = 4096
DEPTH = 1
DEC_BATCH = 32
DEC_SEQ = 8
PAST_LEN = 16384
PAGE_SIZE = 128

A_HEADS = 4
A_KDIM = 128
A_VDIM = 128
A_WIDTH = A_HEADS * A_KDIM
A_CHUNK = 64
B_HEADS = 8
B_HEAD_DIM = 64
B_WIDTH = B_HEADS * B_HEAD_DIM
MOBA_BLOCK = 256
MOBA_TOPK = 3
MOBA_QBLOCK = 64
ROT_DIM = B_HEAD_DIM // 4
ROPE_THETA = 500000.0
N_GROUPS = 4
EXPERTS_PER_GROUP = 8
N_EXPERTS = N_GROUPS * EXPERTS_PER_GROUP
EXPERT_TOPK = 2
D_EXPERT = D_MODEL // 2
MOE_ROWS = 128
IN_WIDTH = 4 * A_WIDTH + 3 * B_WIDTH + 2 * D_MODEL
NORM_EPS = 1e-6

kernel_name = 'hybrid_hgrn2_moba_hmoe_step'

F32 = jnp.float32


def _rmsnorm(x, g):
    xf = x.astype(F32)
    y = xf * lax.rsqrt(jnp.mean(xf * xf, axis=-1, keepdims=True) + NORM_EPS)
    return (y * g.astype(F32)).astype(x.dtype)


def _modulation(c, w, b):
    m = jax.nn.silu(c.astype(F32)) @ w.astype(F32) + b.astype(F32)
    return [a[:, None, :].astype(c.dtype) for a in jnp.split(m, 6, axis=-1)]


def _rope_partial(x, pos):
    half = ROT_DIM // 2
    inv = jnp.power(ROPE_THETA, -(jnp.arange(half, dtype=F32) * 2.0 / ROT_DIM))
    ang = pos.astype(F32)[:, None] * inv[None, :]
    cos = jnp.cos(ang)[:, None, :]
    sin = jnp.sin(ang)[:, None, :]
    xr = x[..., :ROT_DIM].astype(F32)
    x1, x2 = xr[..., :half], xr[..., half:]
    rot = jnp.concatenate([x1 * cos - x2 * sin, x1 * sin + x2 * cos], axis=-1)
    return jnp.concatenate([rot.astype(x.dtype), x[..., ROT_DIM:]], axis=-1)


def _hgrn2_scan(q, logf, k, v, s0):
    n, h, t, _ = q.shape
    dv = v.shape[-1]
    c = min(A_CHUNK, t)
    n_chunks = -(-t // c)
    pad = n_chunks * c - t

    def prep(a):
        a = jnp.pad(a.astype(F32), ((0, 0), (0, 0), (0, pad), (0, 0)))
        return jnp.moveaxis(a.reshape(n, h, n_chunks, c, a.shape[-1]), 2, 0)

    qc, fc, kc, vc = prep(q), prep(logf), prep(k), prep(v)
    causal = jnp.tril(jnp.ones((c, c), dtype=bool))

    def step(s, inp):
        qb, fb, kb, vb = inp
        b = jnp.cumsum(fb, axis=2)
        o_inter = jnp.einsum('nhtk,nhkv->nhtv', qb * jnp.exp(b), s)
        diff = b[:, :, :, None, :] - b[:, :, None, :, :]
        decay = jnp.exp(jnp.where(causal[:, :, None], diff, -jnp.inf))
        attn = jnp.einsum('nhtk,nhtsk,nhsk->nhts', qb, decay, kb)
        o = o_inter + jnp.einsum('nhts,nhsv->nhtv', attn, vb)
        b_last = b[:, :, -1:, :]
        s_new = (jnp.exp(b_last[:, :, 0, :])[..., None] * s
                 + jnp.einsum('nhsk,nhsv->nhkv', kb * jnp.exp(b_last - b), vb))
        return s_new, o

    s_fin, o = lax.scan(step, s0.astype(F32), (qc, fc, kc, vc))
    o = jnp.moveaxis(o, 0, 2).reshape(n, h, n_chunks * c, dv)[:, :, :t]
    return o, s_fin


def _hgrn2_branch(q_raw, f_raw, i_raw, g_raw, lb, gain, s0):
    n, t, _ = q_raw.shape
    heads = lambda a: a.reshape(n, t, A_HEADS, -1).transpose(0, 2, 1, 3)
    lbb = lb[None, :, None, :]
    q = jax.nn.silu(heads(q_raw).astype(F32))
    f = lbb + (1.0 - lbb) * jax.nn.sigmoid(heads(f_raw).astype(F32))
    o, s = _hgrn2_scan(q, jnp.log(f), 1.0 - f, heads(i_raw), s0)
    o = o.transpose(0, 2, 1, 3)
    o = _rmsnorm(o, gain) * jax.nn.silu(g_raw.astype(F32).reshape(n, t, A_HEADS, A_VDIM))
    return o.reshape(n, t, A_WIDTH).astype(q_raw.dtype), s


def _moba_qkv(bq, bk, bv, pos):
    n, t, _ = bq.shape
    shp = (n, t, B_HEADS, B_HEAD_DIM)
    q = _rope_partial(bq.reshape(shp), pos).transpose(0, 2, 1, 3)
    k = _rope_partial(bk.reshape(shp), pos).transpose(0, 2, 1, 3)
    v = bv.reshape(shp).transpose(0, 2, 1, 3)
    return q, k, v


def _moba_prompt(q, k, v):
    b, h, s, dh = q.shape
    n_blk = -(-s // MOBA_BLOCK)
    pad = n_blk * MOBA_BLOCK - s
    kb = jnp.pad(k, ((0, 0), (0, 0), (0, pad), (0, 0))).reshape(b, h, n_blk, MOBA_BLOCK, dh)
    vb = jnp.pad(v, ((0, 0), (0, 0), (0, pad), (0, 0))).reshape(b, h, n_blk, MOBA_BLOCK, dh)
    k_mean = kb.astype(F32).mean(axis=3)
    k_sel = min(MOBA_TOPK, n_blk - 1)
    n_qb = s // MOBA_QBLOCK
    qs = jnp.moveaxis(q.reshape(b, h, n_qb, MOBA_QBLOCK, dh), 2, 0)
    scale = dh ** -0.5
    bi = jnp.arange(b)[:, None, None, None]
    hi = jnp.arange(h)[None, :, None, None]
    sel_len = k_sel * MOBA_BLOCK

    def one_block(args):
        qblk, qi = args
        q0 = qi * MOBA_QBLOCK
        j = q0 // MOBA_BLOCK
        qpos = q0 + jnp.arange(MOBA_QBLOCK)
        qf = qblk.astype(F32) * scale
        own_k = lax.dynamic_index_in_dim(kb, j, axis=2, keepdims=False).astype(F32)
        own_v = lax.dynamic_index_in_dim(vb, j, axis=2, keepdims=False).astype(F32)
        kpos = j * MOBA_BLOCK + jnp.arange(MOBA_BLOCK)
        s_own = jnp.einsum('bhqd,bhkd->bhqk', qf, own_k)
        s_own = jnp.where(kpos[None, :] <= qpos[:, None], s_own, -jnp.inf)
        if k_sel == 0:
            return jnp.einsum('bhqk,bhkd->bhqd', jax.nn.softmax(s_own, axis=-1), own_v)
        blk_score = jnp.einsum('bhqd,bhnd->bhqn', qf, k_mean)
        blk_score = jnp.where(jnp.arange(n_blk) < j, blk_score, -jnp.inf)
        _, idx = lax.top_k(blk_score, k_sel)
        valid = idx < j
        ks = kb[bi, hi, idx].astype(F32)
        vs = vb[bi, hi, idx].astype(F32)
        s_sel = jnp.einsum('bhqd,bhqnkd->bhqnk', qf, ks)
        s_sel = jnp.where(valid[..., None], s_sel, -jnp.inf).reshape(b, h, MOBA_QBLOCK, sel_len)
        p = jax.nn.softmax(jnp.concatenate([s_sel, s_own], axis=-1), axis=-1)
        p_sel = p[..., :sel_len].reshape(b, h, MOBA_QBLOCK, k_sel, MOBA_BLOCK)
        return (jnp.einsum('bhqnk,bhqnkd->bhqd', p_sel, vs)
                + jnp.einsum('bhqk,bhkd->bhqd', p[..., sel_len:], own_v))

    o = lax.map(one_block, (qs, jnp.arange(n_qb)))
    return jnp.moveaxis(o, 0, 2).reshape(b, h, s, dh)


def _moba_sample(q, k_new, v_new, cache_k, cache_v, page_table):
    n, h, t, dh = q.shape
    ppb = MOBA_BLOCK // PAGE_SIZE
    n_full = PAST_LEN // MOBA_BLOCK
    n_tail = (PAST_LEN - n_full * MOBA_BLOCK) // PAGE_SIZE
    qf = q.astype(F32) * dh ** -0.5
    scores, values = [], []
    k_sel = min(MOBA_TOPK, n_full)
    if k_sel > 0:
        page_mean = cache_k.astype(F32).mean(axis=2)
        blk_mean = page_mean[page_table[:, :n_full * ppb]].reshape(n, n_full, ppb, h, dh).mean(axis=2)
        blk_score = jnp.einsum('nhtd,nbhd->nhtb', qf, blk_mean)
        _, idx = lax.top_k(blk_score, k_sel)
        logical = idx[..., None] * ppb + jnp.arange(ppb)
        phys = page_table[jnp.arange(n)[:, None, None, None, None], logical]
        hi = jnp.arange(h)[None, :, None, None, None]
        ks = cache_k[phys, hi].astype(F32).reshape(n, h, t, k_sel * MOBA_BLOCK, dh)
        vs = cache_v[phys, hi].astype(F32).reshape(n, h, t, k_sel * MOBA_BLOCK, dh)
        scores.append(jnp.einsum('nhtd,nhtkd->nhtk', qf, ks))
        values.append((vs, True))
    if n_tail > 0:
        tail = page_table[:, n_full * ppb:n_full * ppb + n_tail]
        kt = cache_k[tail].astype(F32).transpose(0, 2, 1, 3, 4).reshape(n, h, n_tail * PAGE_SIZE, dh)
        vt = cache_v[tail].astype(F32).transpose(0, 2, 1, 3, 4).reshape(n, h, n_tail * PAGE_SIZE, dh)
        scores.append(jnp.einsum('nhtd,nhkd->nhtk', qf, kt))
        values.append((vt, False))
    s_new = jnp.einsum('nhtd,nhsd->nhts', qf, k_new.astype(F32))
    scores.append(jnp.where(jnp.tril(jnp.ones((t, t), dtype=bool)), s_new, -jnp.inf))
    values.append((v_new.astype(F32), False))
    p = jax.nn.softmax(jnp.concatenate(scores, axis=-1), axis=-1)
    offs = np.cumsum([0] + [a.shape[-1] for a in scores]).tolist()
    parts = []
    for (vals, per_query), lo, hi_ in zip(values, offs[:-1], offs[1:]):
        pp = p[..., lo:hi_]
        parts.append(jnp.einsum('nhtk,nhtkd->nhtd', pp, vals) if per_query
                     else jnp.einsum('nhtk,nhkd->nhtd', pp, vals))
    return sum(parts)


def _hier_moe(x, w_group, b_group, w_er, b_er, w_gate, w_up, w_down):
    m, d = x.shape
    xf = x.astype(F32)
    rows = jnp.arange(m)
    g_logits = xf @ w_group.astype(F32) + b_group.astype(F32)
    g_sel = jnp.argmax(g_logits, axis=-1)
    p_group = jax.nn.softmax(g_logits, axis=-1)[rows, g_sel]
    e_logits = jnp.einsum('md,gde->mge', xf, w_er.astype(F32)) + b_er.astype(F32)
    e_logits = e_logits[rows, g_sel]
    top_p, top_i = lax.top_k(jax.nn.softmax(e_logits, axis=-1), EXPERT_TOPK)
    gate_w = p_group[:, None] * top_p / jnp.sum(top_p, axis=-1, keepdims=True)
    expert = (g_sel[:, None] * EXPERTS_PER_GROUP + top_i).astype(jnp.int32)

    n_assign = m * EXPERT_TOPK
    flat_e = expert.reshape(-1)
    flat_tok = jnp.repeat(jnp.arange(m, dtype=jnp.int32), EXPERT_TOPK)
    flat_w = gate_w.reshape(-1)
    order = jnp.argsort(flat_e)
    e_sorted = flat_e[order]
    counts = jnp.bincount(flat_e, length=N_EXPERTS)
    padded = (counts + MOE_ROWS - 1) // MOE_ROWS * MOE_ROWS
    start = jnp.cumsum(counts) - counts
    pad_end = jnp.cumsum(padded)
    pad_start = pad_end - padded
    dest = pad_start[e_sorted] + jnp.arange(n_assign) - start[e_sorted]
    n_blocks = -(-(n_assign + N_EXPERTS * (MOE_ROWS - 1)) // MOE_ROWS)
    n_rows = n_blocks * MOE_ROWS
    row_tok = jnp.zeros((n_rows,), jnp.int32).at[dest].set(flat_tok[order])
    row_w = jnp.zeros((n_rows,), F32).at[dest].set(flat_w[order])
    block_expert = jnp.minimum(
        jnp.searchsorted(pad_end, jnp.arange(n_blocks) * MOE_ROWS, side='right'), N_EXPERTS - 1)

    def run_block(args):
        tok, e = args
        xb = x[tok]
        hid = jax.nn.silu(xb @ w_gate[e]) * (xb @ w_up[e])
        return hid @ w_down[e]

    y = lax.map(run_block, (row_tok.reshape(n_blocks, MOE_ROWS), block_expert))
    out = jnp.zeros((m, d), F32).at[row_tok].add(y.reshape(n_rows, d).astype(F32) * row_w[:, None])
    return out.astype(x.dtype)


def setup_inputs(seed: int = 0) -> dict:
    key = jax.random.key(seed)
    ks = jax.random.split(key, 32)
    n_pages = PAST_LEN // PAGE_SIZE
    n_pool = (DEC_BATCH * n_pages * 5) // 4
    nrm = lambda k, shape, s: jax.random.normal(k, shape, jnp.float32) * s
    return {
        'x_prompt': nrm(ks[0], (BATCH, SEQ, D_MODEL), 1.0),
        'x_sample': nrm(ks[1], (DEC_BATCH, DEC_SEQ, D_MODEL), 1.0),
        'c_prompt': nrm(ks[2], (BATCH, D_MODEL), 1.0),
        'c_sample': nrm(ks[3], (DEC_BATCH, D_MODEL), 1.0),
        'cache_k': nrm(ks[4], (DEPTH, n_pool, B_HEADS, PAGE_SIZE, B_HEAD_DIM), 1.0),
        'cache_v': nrm(ks[5], (DEPTH, n_pool, B_HEADS, PAGE_SIZE, B_HEAD_DIM), 1.0),
        'state_hgrn': nrm(ks[6], (DEPTH, DEC_BATCH, A_HEADS, A_KDIM, A_VDIM), 0.3),
        'page_table': jax.random.permutation(ks[7], n_pool)[:DEC_BATCH * n_pages]
                      .reshape(DEC_BATCH, n_pages).astype(jnp.int32),
        'w_ada': nrm(ks[8], (DEPTH, D_MODEL, 6 * D_MODEL), D_MODEL ** -0.5),
        'b_ada': nrm(ks[9], (DEPTH, 6 * D_MODEL), 0.01),
        'norm_mix': 1.0 + nrm(ks[10], (DEPTH, D_MODEL), 0.02),
        'norm_ffn': 1.0 + nrm(ks[11], (DEPTH, D_MODEL), 0.02),
        'w_in': nrm(ks[12], (DEPTH, D_MODEL, IN_WIDTH), D_MODEL ** -0.5),
        'hgrn_lb_logits': nrm(ks[13], (DEPTH + 1, A_WIDTH), 0.5),
        'hgrn_norm': 1.0 + nrm(ks[14], (DEPTH, A_VDIM), 0.02),
        'w_proj_a': nrm(ks[15], (DEPTH, A_WIDTH, D_MODEL), A_WIDTH ** -0.5),
        'w_proj_b': nrm(ks[16], (DEPTH, B_WIDTH, D_MODEL), B_WIDTH ** -0.5),
        'w_out': nrm(ks[17], (DEPTH, D_MODEL, D_MODEL), D_MODEL ** -0.5),
        'w_group': nrm(ks[18], (DEPTH, D_MODEL, N_GROUPS), D_MODEL ** -0.5),
        'b_group': nrm(ks[19], (DEPTH, N_GROUPS), 0.01),
        'w_expert_router': nrm(ks[20], (DEPTH, N_GROUPS, D_MODEL, EXPERTS_PER_GROUP), D_MODEL ** -0.5),
        'b_expert_router': nrm(ks[21], (DEPTH, N_GROUPS, EXPERTS_PER_GROUP), 0.01),
        'w_gate': nrm(ks[22], (DEPTH, N_EXPERTS, D_MODEL, D_EXPERT), D_MODEL ** -0.5),
        'w_up': nrm(ks[23], (DEPTH, N_EXPERTS, D_MODEL, D_EXPERT), D_MODEL ** -0.5),
        'w_down': nrm(ks[24], (DEPTH, N_EXPERTS, D_EXPERT, D_MODEL), D_EXPERT ** -0.5),
        'norm_final': 1.0 + nrm(ks[25], (D_MODEL,), 0.02),
    }


def reference(x_prompt, x_sample, c_prompt, c_sample, cache_k, cache_v, state_hgrn, page_table,
              w_ada, b_ada, norm_mix, norm_ffn, w_in, hgrn_lb_logits, hgrn_norm,
              w_proj_a, w_proj_b, w_out, w_group, b_group, w_expert_router, b_expert_router,
              w_gate, w_up, w_down, norm_final):
    xp, xs = x_prompt, x_sample
    bp, tp, d = xp.shape
    bs, ts, _ = xs.shape
    pos_p = jnp.arange(tp)
    pos_s = PAST_LEN + jnp.arange(ts)
    lb_all = jnp.cumsum(jax.nn.softmax(hgrn_lb_logits.astype(F32), axis=0), axis=0)
    splits = np.cumsum([A_WIDTH] * 4 + [B_WIDTH] * 3 + [D_MODEL]).tolist()
    kp_l, vp_l, sp_l, ks_l, vs_l, ss_l = [], [], [], [], [], []

    for l in range(DEPTH):
        mp = _modulation(c_prompt, w_ada[l], b_ada[l])
        ms = _modulation(c_sample, w_ada[l], b_ada[l])
        lb = lb_all[l].reshape(A_HEADS, A_KDIM)

        hp = _rmsnorm(xp, norm_mix[l]) * (1 + mp[1]) + mp[0]
        hs = _rmsnorm(xs, norm_mix[l]) * (1 + ms[1]) + ms[0]
        aq_p, af_p, ai_p, ag_p, bq_p, bk_p, bv_p, ga_p, gb_p = jnp.split(hp @ w_in[l], splits, axis=-1)
        aq_s, af_s, ai_s, ag_s, bq_s, bk_s, bv_s, ga_s, gb_s = jnp.split(hs @ w_in[l], splits, axis=-1)

        s0_p = jnp.zeros((bp, A_HEADS, A_KDIM, A_VDIM), F32)
        oa_p, st_p = _hgrn2_branch(aq_p, af_p, ai_p, ag_p, lb, hgrn_norm[l], s0_p)
        oa_s, st_s = _hgrn2_branch(aq_s, af_s, ai_s, ag_s, lb, hgrn_norm[l], state_hgrn[l])

        q_p, k_p, v_p = _moba_qkv(bq_p, bk_p, bv_p, pos_p)
        q_s, k_s, v_s = _moba_qkv(bq_s, bk_s, bv_s, pos_s)
        ob_p = _moba_prompt(q_p, k_p, v_p).transpose(0, 2, 1, 3).reshape(bp, tp, B_WIDTH).astype(xp.dtype)
        ob_s = _moba_sample(q_s, k_s, v_s, cache_k[l], cache_v[l], page_table)
        ob_s = ob_s.transpose(0, 2, 1, 3).reshape(bs, ts, B_WIDTH).astype(xs.dtype)

        y_p = (jax.nn.sigmoid(ga_p) * (oa_p @ w_proj_a[l]) + jax.nn.sigmoid(gb_p) * (ob_p @ w_proj_b[l])) @ w_out[l]
        y_s = (jax.nn.sigmoid(ga_s) * (oa_s @ w_proj_a[l]) + jax.nn.sigmoid(gb_s) * (ob_s @ w_proj_b[l])) @ w_out[l]
        xp = xp + mp[2] * y_p
        xs = xs + ms[2] * y_s

        h2p = _rmsnorm(xp, norm_ffn[l]) * (1 + mp[4]) + mp[3]
        h2s = _rmsnorm(xs, norm_ffn[l]) * (1 + ms[4]) + ms[3]
        f = _hier_moe(jnp.concatenate([h2p.reshape(-1, d), h2s.reshape(-1, d)], axis=0),
                      w_group[l], b_group[l], w_expert_router[l], b_expert_router[l],
                      w_gate[l], w_up[l], w_down[l])
        xp = xp + mp[5] * f[:bp * tp].reshape(bp, tp, d)
        xs = xs + ms[5] * f[bp * tp:].reshape(bs, ts, d)

        kp_l.append(k_p); vp_l.append(v_p); sp_l.append(st_p.astype(state_hgrn.dtype))
        ks_l.append(k_s); vs_l.append(v_s); ss_l.append(st_s.astype(state_hgrn.dtype))

    y_prompt = _rmsnorm(xp, norm_final)
    y_sample = _rmsnorm(xs, norm_final)
    return (y_prompt, y_sample, jnp.stack(kp_l), jnp.stack(vp_l), jnp.stack(sp_l),
            jnp.stack(ks_l), jnp.stack(vs_l), jnp.stack(ss_l))
```

```python
import functools

import numpy as np
import jax
import jax.numpy as jnp
from jax import lax
from jax.experimental import pallas as pl
from jax.experimental.pallas import tpu as pltpu

F32 = jnp.float32
BF16 = jnp.bfloat16
I32 = jnp.int32

D_MODEL = 1024
DEPTH = 1
PAST_LEN = 16384
PAGE_SIZE = 128
A_HEADS = 4
A_KDIM = 128
A_VDIM = 128
A_WIDTH = A_HEADS * A_KDIM
B_HEADS = 8
B_HEAD_DIM = 64
B_WIDTH = B_HEADS * B_HEAD_DIM
MOBA_BLOCK = 256
MOBA_TOPK = 3
ROT_DIM = B_HEAD_DIM // 4
ROPE_THETA = 500000.0
N_GROUPS = 4
EXPERTS_PER_GROUP = 8
N_EXPERTS = N_GROUPS * EXPERTS_PER_GROUP
EXPERT_TOPK = 2
D_EXPERT = D_MODEL // 2
NORM_EPS = 1e-6

PAGES_PER_BLOCK = MOBA_BLOCK // PAGE_SIZE
ROW_TILE = 256
HGRN_CHUNK = 16
MOE_TILE = 256
LANES = 128
NEG_BIG = -1e30
VMEM_LIMIT_BYTES = 52 * 1024 * 1024


def _params(*sem):
    return pltpu.CompilerParams(dimension_semantics=sem, vmem_limit_bytes=VMEM_LIMIT_BYTES)


def _dot(a, b):
    return jnp.dot(a, b, preferred_element_type=F32)


def _dot_nt(a, b):
    return lax.dot_general(a, b, (((1,), (1,)), ((), ())), preferred_element_type=F32)


def _dot_tn(a, b):
    return lax.dot_general(a, b, (((0,), (0,)), ((), ())), preferred_element_type=F32)


def _split(a):
    hi = a.astype(BF16)
    return hi, (a - hi.astype(F32)).astype(BF16)


def _dot3(a, b, dot=_dot):
    ah, al = _split(a)
    bh, bl = _split(b)
    return dot(ah, bh) + (dot(ah, bl) + dot(al, bh))


def _silu(x):
    return x * jax.nn.sigmoid(x)


def _mod_kernel(c_ref, w_ref, b_ref, o_ref):
    o_ref[...] = _dot3(_silu(c_ref[...]), w_ref[...]) + b_ref[...]


def _modulation(c_all, w, b):
    n = c_all.shape[0]
    d, dout = w.shape
    return pl.pallas_call(
        _mod_kernel,
        grid=(dout // d,),
        in_specs=[pl.BlockSpec((n, d), lambda j: (0, 0)),
                  pl.BlockSpec((d, d), lambda j: (0, j)),
                  pl.BlockSpec((1, d), lambda j: (0, j))],
        out_specs=pl.BlockSpec((n, d), lambda j: (0, j)),
        out_shape=jax.ShapeDtypeStruct((n, dout), F32),
        compiler_params=_params("parallel"),
        name="modulation",
    )(c_all, w, b.reshape(1, dout))


def _inproj_kernel(x_ref, sh_ref, sc_ref, g_ref, w_ref, wlo_ref, za_ref, zb_ref, zg_ref):
    x = x_ref[...]
    h = x * lax.rsqrt(jnp.mean(x * x, axis=-1, keepdims=True) + NORM_EPS) * g_ref[...]
    h = h * (1.0 + sc_ref[0]) + sh_ref[0]
    hh, hl = _split(h)
    wa = 4 * A_WIDTH
    for c in range(0, wa, 512):
        za_ref[:, c:c + 512] = _dot(hh, w_ref[:, c:c + 512])
    for c in range(0, 2 * B_WIDTH, 512):
        wc = w_ref[:, wa + c:wa + c + 512]
        zb_ref[:, c:c + 512] = _dot(hh, wc) + (_dot(hh, wlo_ref[:, c:c + 512]) + _dot(hl, wc))
    c = 2 * B_WIDTH
    zb_ref[:, c:c + 512] = _dot(hh, w_ref[:, wa + c:wa + c + 512])
    wg = wa + 3 * B_WIDTH
    for c in range(0, 2 * D_MODEL, 512):
        zg_ref[:, c:c + 512] = _dot(hh, w_ref[:, wg + c:wg + c + 512])


def _inproj(x_all, mod_tiles, norm_g, w_in, tiles_per_seq):
    m, d = x_all.shape
    n_mod = mod_tiles.shape[1]
    w_hi = w_in.astype(BF16)
    wa = 4 * A_WIDTH
    wqk = w_in[:, wa:wa + 2 * B_WIDTH]
    w_lo = (wqk - wqk.astype(BF16).astype(F32)).astype(BF16)
    mod_map = lambda i: (jnp.minimum(i // tiles_per_seq, n_mod - 1), 0, 0)
    const = lambda i: (0, 0)
    row = lambda i: (i, 0)
    return pl.pallas_call(
        _inproj_kernel,
        grid=(m // ROW_TILE,),
        in_specs=[pl.BlockSpec((ROW_TILE, d), row),
                  pl.BlockSpec((1, ROW_TILE, d), mod_map),
                  pl.BlockSpec((1, ROW_TILE, d), mod_map),
                  pl.BlockSpec((1, d), const),
                  pl.BlockSpec(w_hi.shape, const, pipeline_mode=pl.Buffered(1)),
                  pl.BlockSpec(w_lo.shape, const, pipeline_mode=pl.Buffered(1))],
        out_specs=[pl.BlockSpec((ROW_TILE, wa), row),
                   pl.BlockSpec((ROW_TILE, 3 * B_WIDTH), row),
                   pl.BlockSpec((ROW_TILE, 2 * D_MODEL), row)],
        out_shape=[jax.ShapeDtypeStruct((m, wa), F32),
                   jax.ShapeDtypeStruct((m, 3 * B_WIDTH), F32),
                   jax.ShapeDtypeStruct((m, 2 * D_MODEL), F32)],
        compiler_params=_params("parallel"),
        name="inproj",
    )(x_all, mod_tiles[0], mod_tiles[1], norm_g.reshape(1, d), w_hi, w_lo)


def _rope_kernel(q_ref, k_ref, v_ref, c_ref, s1_ref, s2_ref, qo_ref, ko_ref, vo_ref):
    cos, s1, s2 = c_ref[...], s1_ref[...], s2_ref[...]
    half = ROT_DIM // 2

    def rope(x):
        up = pltpu.roll(x, B_WIDTH - half, axis=1)
        dn = pltpu.roll(x, half, axis=1)
        return x * cos + up * s1 + dn * s2

    q = rope(q_ref[...]) * (B_HEAD_DIM ** -0.5)
    k = rope(k_ref[...])
    v = v_ref[...]
    for h in range(B_HEADS):
        ls = slice(h * B_HEAD_DIM, (h + 1) * B_HEAD_DIM)
        qo_ref[0, h] = q[:, ls]
        ko_ref[0, h] = k[:, ls]
        vo_ref[0, h] = v[:, ls]


def _rope_tables(pos):
    half = ROT_DIM // 2
    inv = jnp.power(ROPE_THETA, -(jnp.arange(half, dtype=F32) * 2.0 / ROT_DIM))
    ang = pos.astype(F32)[:, None] * inv[None, :]
    cos, sin = jnp.cos(ang), jnp.sin(ang)
    t = pos.shape[0]
    rest = B_HEAD_DIM - ROT_DIM
    c = jnp.concatenate([cos, cos, jnp.ones((t, rest), F32)], axis=-1)
    s1 = jnp.concatenate([-sin, jnp.zeros((t, half + rest), F32)], axis=-1)
    s2 = jnp.concatenate([jnp.zeros((t, half), F32), sin, jnp.zeros((t, rest), F32)], axis=-1)
    return [jnp.tile(a, (1, B_HEADS)) for a in (c, s1, s2)]


def _rope_split(zb, row0, n, t, tile, pos):
    tabs = _rope_tables(pos)
    tps = t // tile
    blk0 = row0 // tile
    zmap = lambda c: (lambda b, s: (blk0 + b * tps + s, c))
    tmap = lambda b, s: (s, 0)
    omap = lambda b, s: (b, 0, s, 0)
    oshape = jax.ShapeDtypeStruct((n, B_HEADS, t, B_HEAD_DIM), F32)
    ospec = pl.BlockSpec((1, B_HEADS, tile, B_HEAD_DIM), omap)
    return pl.pallas_call(
        _rope_kernel,
        grid=(n, tps),
        in_specs=[pl.BlockSpec((tile, B_WIDTH), zmap(0)),
                  pl.BlockSpec((tile, B_WIDTH), zmap(1)),
                  pl.BlockSpec((tile, B_WIDTH), zmap(2)),
                  pl.BlockSpec((tile, B_WIDTH), tmap),
                  pl.BlockSpec((tile, B_WIDTH), tmap),
                  pl.BlockSpec((tile, B_WIDTH), tmap)],
        out_specs=[ospec, ospec, ospec],
        out_shape=[oshape, oshape, oshape],
        compiler_params=_params("parallel", "parallel"),
        name="rope_split",
    )(zb, zb, zb, *tabs)


def _hgrn_kernel(*refs, chunk, has_s0):
    aq_ref, af_ref, ai_ref, ag_ref, lb_ref, gain_ref = refs[:6]
    rest = refs[6:]
    s0_ref = None
    if has_s0:
        s0_ref, rest = rest[0], rest[1:]
    o_ref, so_ref, st_ref, q_s, b_s, k_s = rest
    t = pl.program_id(1)
    tb = aq_ref.shape[0]

    @pl.when(t == 0)
    def _():
        for h in range(A_HEADS):
            if has_s0:
                st_ref[h] = s0_ref[0, h].T
            else:
                st_ref[h] = jnp.zeros((A_VDIM, A_KDIM), F32)

    lb = lb_ref[...]
    f = lb + (1.0 - lb) * jax.nn.sigmoid(af_ref[...])
    logf = jnp.log(f)
    q_s[...] = _silu(aq_ref[...])
    k_s[...] = 1.0 - f
    row = lax.broadcasted_iota(I32, logf.shape, 0) & (chunk - 1)
    b = logf
    sh = 1
    while sh < chunk:
        b = b + jnp.where(row >= sh, pltpu.roll(b, sh, axis=0), 0.0)
        sh *= 2
    b_s[...] = b
    rowc = lax.broadcasted_iota(I32, (chunk, A_KDIM), 0)

    def one_chunk(ci, carry):
        r0 = pl.multiple_of(ci * chunk, chunk)
        rs = pl.ds(r0, chunk)
        for h in range(A_HEADS):
            ls = slice(h * A_KDIM, (h + 1) * A_KDIM)
            qc, bc, kc, vc = q_s[rs, ls], b_s[rs, ls], k_s[rs, ls], ai_ref[rs, ls]
            st = st_ref[h]
            bl = bc[chunk - 1:chunk, :]
            o = _dot_nt((qc * jnp.exp(bc)).astype(BF16), st.astype(BF16))
            for s in range(chunk):
                e = jnp.exp(jnp.where(rowc >= s, bc - bc[s:s + 1, :], -jnp.inf))
                r = jnp.sum(qc * e * kc[s:s + 1, :], axis=1, keepdims=True)
                o = o + r * vc[s:s + 1, :]
            kp = kc * jnp.exp(bl - bc)
            st_ref[h] = st * jnp.exp(bl) + _dot_tn(vc.astype(BF16), kp.astype(BF16))
            o_ref[rs, ls] = o
        return carry

    lax.fori_loop(0, tb // chunk, one_chunk, 0)

    gain = gain_ref[...]
    for h in range(A_HEADS):
        ls = slice(h * A_VDIM, (h + 1) * A_VDIM)
        oh = o_ref[:, ls]
        y = oh * lax.rsqrt(jnp.mean(oh * oh, axis=-1, keepdims=True) + NORM_EPS) * gain
        o_ref[:, ls] = y * _silu(ag_ref[:, ls])

    @pl.when(t == pl.num_programs(1) - 1)
    def _():
        for h in range(A_HEADS):
            so_ref[0, h] = st_ref[h].T


def _hgrn(za, lb, gain, row0, n, t, tile, chunk, s0=None):
    tps = t // tile
    blk0 = row0 // tile
    zmap = lambda c: (lambda b, s: (blk0 + b * tps + s, c))
    const = lambda b, s: (0, 0)
    in_specs = [pl.BlockSpec((tile, A_WIDTH), zmap(c)) for c in range(4)]
    in_specs += [pl.BlockSpec((1, A_WIDTH), const), pl.BlockSpec((1, A_VDIM), const)]
    args = [za, za, za, za, lb.reshape(1, A_WIDTH), gain.reshape(1, A_VDIM)]
    if s0 is not None:
        in_specs.append(pl.BlockSpec((1, A_HEADS, A_KDIM, A_VDIM), lambda b, s: (b, 0, 0, 0)))
        args.append(s0)
    return pl.pallas_call(
        functools.partial(_hgrn_kernel, chunk=chunk, has_s0=s0 is not None),
        grid=(n, tps),
        in_specs=in_specs,
        out_specs=[pl.BlockSpec((tile, A_WIDTH), lambda b, s: (b * tps + s, 0)),
                   pl.BlockSpec((1, A_HEADS, A_KDIM, A_VDIM), lambda b, s: (b, 0, 0, 0))],
        out_shape=[jax.ShapeDtypeStruct((n * t, A_WIDTH), F32),
                   jax.ShapeDtypeStruct((n, A_HEADS, A_KDIM, A_VDIM), F32)],
        scratch_shapes=[pltpu.VMEM((A_HEADS, A_VDIM, A_KDIM), F32),
                        pltpu.VMEM((tile, A_WIDTH), F32),
                        pltpu.VMEM((tile, A_WIDTH), F32),
                        pltpu.VMEM((tile, A_WIDTH), F32)],
        compiler_params=_params("parallel", "arbitrary"),
        name="hgrn2",
    )(*args)


def _moba_prompt_kernel(q_ref, k_ref, v_ref, o_ref, km_ref, kb_ref, vb_ref):
    j = pl.program_id(2)
    s_len = k_ref.shape[2]
    n_blk = s_len // MOBA_BLOCK
    blk = MOBA_BLOCK

    @pl.when(j == 0)
    def _():
        km_ref[...] = jnp.zeros(km_ref.shape, F32)
        for i in range(n_blk):
            km_ref[i:i + 1, :] = jnp.mean(k_ref[0, 0, i * blk:(i + 1) * blk, :], axis=0, keepdims=True)
        kb_ref[...] = k_ref[0, 0].astype(BF16)
        vb_ref[...] = v_ref[0, 0].astype(BF16)

    q = q_ref[0, 0]
    lane = lax.broadcasted_iota(I32, (blk, LANES), 1)
    valid = lane < j
    sc = jnp.where(valid, _dot3(q, km_ref[...], _dot_nt), -jnp.inf)
    rank = jnp.zeros((blk, LANES), I32)
    for i in range(n_blk - 1):
        col = sc[:, i:i + 1]
        beats = jnp.where(col > sc, 1, jnp.where(col == sc, jnp.where(lane > i, 1, 0), 0))
        rank = rank + beats
    sel = jnp.where(valid, jnp.where(rank < MOBA_TOPK, 1.0, 0.0), 0.0)

    qb = q.astype(BF16)

    def update(carry, s, mask, vb):
        m, l, acc = carry
        m_new = jnp.maximum(m, jnp.max(jnp.where(mask, s, NEG_BIG), axis=1, keepdims=True))
        p = jnp.where(mask, jnp.exp(s - m_new), 0.0)
        alpha = jnp.exp(m - m_new)
        return (m_new, alpha * l + jnp.sum(p, axis=1, keepdims=True),
                alpha * acc + _dot(p.astype(BF16), vb))

    def past_block(i, carry):
        rs = pl.ds(pl.multiple_of(i * blk, blk), blk)
        s = _dot_nt(qb, kb_ref[rs, :])
        picked = jnp.sum(jnp.where(lane == i, sel, 0.0), axis=1, keepdims=True) > 0.5
        return update(carry, s, picked, vb_ref[rs, :])

    init = (jnp.full((blk, 1), NEG_BIG, F32), jnp.zeros((blk, 1), F32), jnp.zeros((blk, B_HEAD_DIM), F32))
    carry = lax.fori_loop(0, j, past_block, init)
    rs = pl.ds(pl.multiple_of(j * blk, blk), blk)
    s = _dot_nt(qb, kb_ref[rs, :])
    causal = (lax.broadcasted_iota(I32, (blk, blk), 1) <= lax.broadcasted_iota(I32, (blk, blk), 0))
    _, l, acc = update(carry, s, causal, vb_ref[rs, :])
    o_ref[0, 0] = acc / l


def _moba_prompt(q, k, v):
    b, h, s, dh = q.shape
    full = pl.BlockSpec((1, 1, s, dh), lambda bi, hi, j: (bi, hi, 0, 0))
    tile = pl.BlockSpec((1, 1, MOBA_BLOCK, dh), lambda bi, hi, j: (bi, hi, j, 0))
    return pl.pallas_call(
        _moba_prompt_kernel,
        grid=(b, h, s // MOBA_BLOCK),
        in_specs=[tile, full, full],
        out_specs=tile,
        out_shape=jax.ShapeDtypeStruct((b, h, s, dh), F32),
        scratch_shapes=[pltpu.VMEM((LANES, dh), F32),
                        pltpu.VMEM((s, dh), BF16),
                        pltpu.VMEM((s, dh), BF16)],
        compiler_params=_params("parallel", "parallel", "arbitrary"),
        name="moba_prompt",
    )(q, k, v)


PAGE_MEAN_TILE = 16


def _page_mean_kernel(c_ref, o_ref):
    o_ref[...] = jnp.sum(c_ref[0], axis=2) * (1.0 / PAGE_SIZE)


def _page_mean(cache_k, layer):
    _, n_pool, h, ps, dh = cache_k.shape
    return pl.pallas_call(
        _page_mean_kernel,
        grid=(n_pool // PAGE_MEAN_TILE,),
        in_specs=[pl.BlockSpec((1, PAGE_MEAN_TILE, h, ps, dh), lambda i: (layer, i, 0, 0, 0))],
        out_specs=pl.BlockSpec((PAGE_MEAN_TILE, h, dh), lambda i: (i, 0, 0)),
        out_shape=jax.ShapeDtypeStruct((n_pool, h, dh), F32),
        compiler_params=_params("parallel"),
        name="page_mean",
    )(cache_k)


def _select_kernel(q_ref, pe_ref, po_ref, o_ref):
    n_blk = pe_ref.shape[2]
    t = q_ref.shape[2]
    lane = lax.broadcasted_iota(I32, (t, n_blk), 1).astype(F32)
    for h in range(B_HEADS):
        bm = 0.5 * (pe_ref[0, h] + po_ref[0, h])
        sc = _dot3(q_ref[0, h], bm, _dot_nt)
        res = jnp.zeros((t, n_blk), F32)
        for r in range(MOBA_TOPK):
            m = jnp.max(sc, axis=1, keepdims=True)
            idx = jnp.min(jnp.where(sc == m, lane, float(n_blk)), axis=1, keepdims=True)
            res = jnp.where(lane == r, idx, res)
            sc = jnp.where(lane == idx, -jnp.inf, sc)
        o_ref[0, h] = res.astype(I32)


def _select_blocks(q, pm_even, pm_odd):
    n, h, t, dh = q.shape
    n_blk = pm_even.shape[2]
    qs = pl.BlockSpec((1, h, t, dh), lambda i: (i, 0, 0, 0))
    ps = pl.BlockSpec((1, h, n_blk, dh), lambda i: (i, 0, 0, 0))
    return pl.pallas_call(
        _select_kernel,
        grid=(n,),
        in_specs=[qs, ps, ps],
        out_specs=pl.BlockSpec((1, h, t, n_blk), lambda i: (i, 0, 0, 0)),
        out_shape=jax.ShapeDtypeStruct((n, h, t, n_blk), I32),
        compiler_params=_params("parallel"),
        name="moba_select",
    )(q, pm_even, pm_odd)


def _moba_sample_kernel(pg_ref, q_ref, kn_ref, vn_ref, ck_ref, cv_ref, o_ref, kbuf, vbuf, sem, *, layer):
    n, h = pl.program_id(0), pl.program_id(1)
    t = q_ref.shape[2]
    n_pages = kbuf.shape[0]
    per_q = (n_pages // t) * PAGE_SIZE
    base = (n * pl.num_programs(1) + h) * n_pages

    def copies(i):
        pg = pg_ref[base + i]
        return (pltpu.make_async_copy(ck_ref.at[layer, pg, h], kbuf.at[i], sem.at[0]),
                pltpu.make_async_copy(cv_ref.at[layer, pg, h], vbuf.at[i], sem.at[1]))

    for i in range(n_pages):
        for cp in copies(i):
            cp.start()
    for i in range(n_pages):
        for cp in copies(i):
            cp.wait()

    kk = kbuf[...].reshape(n_pages * PAGE_SIZE, B_HEAD_DIM).astype(BF16)
    vv = vbuf[...].reshape(n_pages * PAGE_SIZE, B_HEAD_DIM).astype(BF16)
    q = q_ref[0, 0]
    s = _dot_nt(q.astype(BF16), kk)
    col = lax.broadcasted_iota(I32, s.shape, 1)
    lo = lax.broadcasted_iota(I32, s.shape, 0) * per_q
    own = (col >= lo) & (col < lo + per_q)
    sn = _dot_nt(q, kn_ref[0, 0])
    causal = lax.broadcasted_iota(I32, (t, t), 1) <= lax.broadcasted_iota(I32, (t, t), 0)
    m = jnp.maximum(jnp.max(jnp.where(own, s, NEG_BIG), axis=1, keepdims=True),
                    jnp.max(jnp.where(causal, sn, NEG_BIG), axis=1, keepdims=True))
    p = jnp.where(own, jnp.exp(s - m), 0.0)
    pn = jnp.where(causal, jnp.exp(sn - m), 0.0)
    l = jnp.sum(p, axis=1, keepdims=True) + jnp.sum(pn, axis=1, keepdims=True)
    o_ref[0, 0] = (_dot(p.astype(BF16), vv) + _dot(pn, vn_ref[0, 0])) / l


def _moba_sample(q, k_new, v_new, cache_k, cache_v, pages, layer):
    n, h, t, dh = q.shape
    n_pages = t * MOBA_TOPK * PAGES_PER_BLOCK
    blk = pl.BlockSpec((1, 1, t, dh), lambda i, j, pg: (i, j, 0, 0))
    anyspec = pl.BlockSpec(memory_space=pl.ANY)
    return pl.pallas_call(
        functools.partial(_moba_sample_kernel, layer=layer),
        grid_spec=pltpu.PrefetchScalarGridSpec(
            num_scalar_prefetch=1,
            grid=(n, h),
            in_specs=[blk, blk, blk, anyspec, anyspec],
            out_specs=blk,
            scratch_shapes=[pltpu.VMEM((n_pages, PAGE_SIZE, dh), F32),
                            pltpu.VMEM((n_pages, PAGE_SIZE, dh), F32),
                            pltpu.SemaphoreType.DMA((2,))]),
        out_shape=jax.ShapeDtypeStruct((n, h, t, dh), F32),
        compiler_params=_params("arbitrary", "arbitrary"),
        name="moba_sample",
    )(pages, q, k_new, v_new, cache_k, cache_v)


def _merge_kernel(x_ref, oa_ref, ob_ref, ga_ref, gb_ref, g1_ref, sh_ref, sc_ref, nf_ref,
                  wpa_ref, wpb_ref, wo_ref, wr_ref, br_ref, x1_ref, h2_ref, eid_ref, gw_ref):
    pa = _dot(oa_ref[...].astype(BF16), wpa_ref[...])
    pb = _dot(ob_ref[...].astype(BF16), wpb_ref[...])
    mix = jax.nn.sigmoid(ga_ref[...]) * pa + jax.nn.sigmoid(gb_ref[...]) * pb
    x1 = x_ref[...] + g1_ref[0] * _dot(mix.astype(BF16), wo_ref[...])
    x1_ref[...] = x1
    h2 = x1 * lax.rsqrt(jnp.mean(x1 * x1, axis=-1, keepdims=True) + NORM_EPS) * nf_ref[...]
    h2 = h2 * (1.0 + sc_ref[0]) + sh_ref[0]
    h2_ref[...] = h2.astype(BF16)

    logits = _dot3(h2, wr_ref[...]) + br_ref[...]
    lane = lax.broadcasted_iota(I32, logits.shape, 1)
    lanef = lane.astype(F32)
    is_g = lane < N_GROUPS
    gl = jnp.where(is_g, logits, -jnp.inf)
    gmax = jnp.max(gl, axis=1, keepdims=True)
    g_sel = jnp.min(jnp.where(gl == gmax, lanef, float(LANES)), axis=1, keepdims=True)
    p_group = 1.0 / jnp.sum(jnp.exp(gl - gmax), axis=1, keepdims=True)
    e_lo = N_GROUPS + g_sel * EXPERTS_PER_GROUP
    in_grp = (lanef >= e_lo) & (lanef < e_lo + EXPERTS_PER_GROUP)
    el = jnp.where(in_grp, logits, -jnp.inf)
    m1 = jnp.max(el, axis=1, keepdims=True)
    i1 = jnp.min(jnp.where(el == m1, lanef, float(LANES)), axis=1, keepdims=True)
    el2 = jnp.where(lanef == i1, -jnp.inf, el)
    m2 = jnp.max(el2, axis=1, keepdims=True)
    i2 = jnp.min(jnp.where(el2 == m2, lanef, float(LANES)), axis=1, keepdims=True)
    e2 = jnp.exp(m2 - m1)
    w1 = p_group / (1.0 + e2)
    w2 = p_group * e2 / (1.0 + e2)
    eid = jnp.where(lane == 0, i1 - N_GROUPS, jnp.where(lane == 1, i2 - N_GROUPS, 0.0))
    eid_ref[...] = eid.astype(I32)
    gw_ref[...] = jnp.where(lane == 0, w1, jnp.where(lane == 1, w2, 0.0))


def _merge(x_all, oa, ob, zg, mod_tiles, norm_ffn, wpa, wpb, wo, w_router, b_router, tiles_per_seq):
    m, d = x_all.shape
    n_mod = mod_tiles.shape[1]
    mod_map = lambda i: (jnp.minimum(i // tiles_per_seq, n_mod - 1), 0, 0)
    const = lambda i: (0, 0)
    row = lambda i: (i, 0)
    mod = pl.BlockSpec((1, ROW_TILE, d), mod_map)
    wspec = lambda w: pl.BlockSpec(w.shape, const, pipeline_mode=pl.Buffered(1))
    return pl.pallas_call(
        _merge_kernel,
        grid=(m // ROW_TILE,),
        in_specs=[pl.BlockSpec((ROW_TILE, d), row),
                  pl.BlockSpec((ROW_TILE, A_WIDTH), row),
                  pl.BlockSpec((ROW_TILE, B_WIDTH), row),
                  pl.BlockSpec((ROW_TILE, d), lambda i: (i, 0)),
                  pl.BlockSpec((ROW_TILE, d), lambda i: (i, 1)),
                  mod, mod, mod,
                  pl.BlockSpec((1, d), const),
                  wspec(wpa), wspec(wpb), wspec(wo), wspec(w_router),
                  pl.BlockSpec((1, LANES), const)],
        out_specs=[pl.BlockSpec((ROW_TILE, d), row),
                   pl.BlockSpec((ROW_TILE, d), row),
                   pl.BlockSpec((ROW_TILE, LANES), row),
                   pl.BlockSpec((ROW_TILE, LANES), row)],
        out_shape=[jax.ShapeDtypeStruct((m, d), F32),
                   jax.ShapeDtypeStruct((m, d), BF16),
                   jax.ShapeDtypeStruct((m, LANES), I32),
                   jax.ShapeDtypeStruct((m, LANES), F32)],
        compiler_params=_params("parallel"),
        name="merge_router",
    )(x_all, oa, ob, zg, zg, mod_tiles[2], mod_tiles[3], mod_tiles[4], norm_ffn.reshape(1, d),
      wpa, wpb, wo, w_router, b_router)


def _moe_kernel(be_ref, nu_ref, x_ref, wg_ref, wu_ref, wd_ref, y_ref, wg_s, wu_s, wd_s):
    i = pl.program_id(0)
    prev = be_ref[jnp.maximum(i - 1, 0)]
    fresh = (i == 0) | (be_ref[i] != prev)

    @pl.when(fresh)
    def _():
        wg_s[...] = wg_ref[...].astype(BF16)
        wu_s[...] = wu_ref[...].astype(BF16)
        wd_s[...] = wd_ref[...].astype(BF16)

    @pl.when(i < nu_ref[0])
    def _():
        xb = x_ref[...]
        hid = _silu(_dot(xb, wg_s[...])) * _dot(xb, wu_s[...])
        y_ref[...] = _dot(hid.astype(BF16), wd_s[...])

    @pl.when(i >= nu_ref[0])
    def _():
        y_ref[...] = jnp.zeros(y_ref.shape, F32)


def _moe(xs, block_expert, n_used, w_gate, w_up, w_down, layer):
    n_rows, d = xs.shape
    de = w_gate.shape[-1]
    return pl.pallas_call(
        _moe_kernel,
        grid_spec=pltpu.PrefetchScalarGridSpec(
            num_scalar_prefetch=2,
            grid=(n_rows // MOE_TILE,),
            in_specs=[pl.BlockSpec((MOE_TILE, d), lambda i, be, nu: (i, 0)),
                      pl.BlockSpec((None, None, d, de), lambda i, be, nu: (layer, be[i], 0, 0)),
                      pl.BlockSpec((None, None, d, de), lambda i, be, nu: (layer, be[i], 0, 0)),
                      pl.BlockSpec((None, None, de, d), lambda i, be, nu: (layer, be[i], 0, 0))],
            out_specs=pl.BlockSpec((MOE_TILE, d), lambda i, be, nu: (i, 0)),
            scratch_shapes=[pltpu.VMEM((d, de), BF16), pltpu.VMEM((d, de), BF16), pltpu.VMEM((de, d), BF16)]),
        out_shape=jax.ShapeDtypeStruct((n_rows, d), F32),
        compiler_params=_params("arbitrary"),
        name="moe_experts",
    )(block_expert, n_used, xs, w_gate, w_up, w_down)


def _final_kernel(x1_ref, y0_ref, y1_ref, gw_ref, g2_ref, nf_ref, o_ref, *, last):
    gw = gw_ref[...]
    f = gw[:, 0:1] * y0_ref[...] + gw[:, 1:2] * y1_ref[...]
    x2 = x1_ref[...] + g2_ref[0] * f
    if last:
        x2 = x2 * lax.rsqrt(jnp.mean(x2 * x2, axis=-1, keepdims=True) + NORM_EPS) * nf_ref[...]
    o_ref[...] = x2


def _final(x1, y0, y1, gw, mod_tiles, norm_final, tiles_per_seq, last):
    m, d = x1.shape
    n_mod = mod_tiles.shape[1]
    mod_map = lambda i: (jnp.minimum(i // tiles_per_seq, n_mod - 1), 0, 0)
    row = lambda i: (i, 0)
    rs = pl.BlockSpec((ROW_TILE, d), row)
    return pl.pallas_call(
        functools.partial(_final_kernel, last=last),
        grid=(m // ROW_TILE,),
        in_specs=[rs, rs, rs, pl.BlockSpec((ROW_TILE, LANES), row),
                  pl.BlockSpec((1, ROW_TILE, d), mod_map),
                  pl.BlockSpec((1, d), lambda i: (0, 0))],
        out_specs=rs,
        out_shape=jax.ShapeDtypeStruct((m, d), F32),
        compiler_params=_params("parallel"),
        name="combine_final",
    )(x1, y0, y1, gw, mod_tiles[5], norm_final.reshape(1, d))


def _dispatch(eid):
    m = eid.shape[0]
    n_assign = m * EXPERT_TOPK
    flat_e = eid.reshape(-1)
    flat_tok = jnp.repeat(jnp.arange(m, dtype=I32), EXPERT_TOPK)
    onehot = (flat_e[:, None] == jnp.arange(N_EXPERTS, dtype=I32)[None, :]).astype(I32)
    csum = jnp.cumsum(onehot, axis=0)
    rank = jnp.take_along_axis(csum, flat_e[:, None], axis=1)[:, 0] - 1
    counts = csum[-1]
    padded = (counts + MOE_TILE - 1) // MOE_TILE * MOE_TILE
    pad_end = jnp.cumsum(padded)
    pad_start = pad_end - padded
    dest = (pad_start[flat_e] + rank).astype(I32)
    n_blocks = -(-(n_assign + N_EXPERTS * (MOE_TILE - 1)) // MOE_TILE)
    row_tok = jnp.zeros((n_blocks * MOE_TILE,), I32).at[dest].set(flat_tok)
    block_expert = jnp.minimum(
        jnp.searchsorted(pad_end, jnp.arange(n_blocks, dtype=I32) * MOE_TILE, side='right'),
        N_EXPERTS - 1).astype(I32)
    n_used = (pad_end[-1:] // MOE_TILE).astype(I32)
    return dest.reshape(m, EXPERT_TOPK), row_tok, block_expert, n_used


def _mod_tiles(mod, n_prompt, reps):
    d = mod.shape[-1]
    mp = jnp.broadcast_to(mod[:n_prompt, :, None, :], (n_prompt, 6, ROW_TILE, d))
    ms = jnp.repeat(mod[n_prompt:], reps, axis=0).transpose(1, 0, 2)[None]
    return jnp.concatenate([mp, ms], axis=0).transpose(1, 0, 2, 3)


def kernel(x_prompt, x_sample, c_prompt, c_sample, cache_k, cache_v, state_hgrn, page_table,
           w_ada, b_ada, norm_mix, norm_ffn, w_in, hgrn_lb_logits, hgrn_norm,
           w_proj_a, w_proj_b, w_out, w_group, b_group, w_expert_router, b_expert_router,
           w_gate, w_up, w_down, norm_final):
    bp, tp, d = x_prompt.shape
    bs, ts, _ = x_sample.shape
    mp_rows, ms_rows = bp * tp, bs * ts
    assert ms_rows == ROW_TILE and tp % ROW_TILE == 0
    m = mp_rows + ms_rows
    tiles_per_seq = tp // ROW_TILE
    n_full = PAST_LEN // MOBA_BLOCK
    assert PAST_LEN % MOBA_BLOCK == 0 and n_full >= MOBA_TOPK

    x = jnp.concatenate([x_prompt.reshape(mp_rows, d), x_sample.reshape(ms_rows, d)], axis=0)
    c_all = jnp.concatenate([c_prompt, c_sample, jnp.zeros((-(bp + bs) % 8, d), F32)], axis=0)
    lb_all = jnp.cumsum(jax.nn.softmax(hgrn_lb_logits.astype(F32), axis=0), axis=0)
    kp_l, vp_l, sp_l, ks_l, vs_l, ss_l = [], [], [], [], [], []

    for l in range(DEPTH):
        mod = _modulation(c_all, w_ada[l], b_ada[l])[:bp + bs].reshape(bp + bs, 6, d)
        mt = _mod_tiles(mod, bp, ts)
        za, zb, zg = _inproj(x, mt, norm_mix[l], w_in[l], tiles_per_seq)

        oa_p, st_p = _hgrn(za, lb_all[l], hgrn_norm[l], 0, bp, tp, ROW_TILE, HGRN_CHUNK)
        oa_s, st_s = _hgrn(za, lb_all[l], hgrn_norm[l], mp_rows, bs, ts, ts, ts, s0=state_hgrn[l])
        oa = jnp.concatenate([oa_p, oa_s], axis=0)

        q_p, k_p, v_p = _rope_split(zb, 0, bp, tp, 512, jnp.arange(tp))
        q_s, k_s, v_s = _rope_split(zb, mp_rows, bs, ts, ts, PAST_LEN + jnp.arange(ts))
        ob_p = _moba_prompt(q_p, k_p, v_p)
        pm = _page_mean(cache_k, l)
        pm_g = pm[page_table[:, :n_full * PAGES_PER_BLOCK]].transpose(0, 2, 1, 3)
        sel = _select_blocks(q_s, pm_g[:, :, 0::2], pm_g[:, :, 1::2])[..., :MOBA_TOPK]
        logical = sel[..., None] * PAGES_PER_BLOCK + jnp.arange(PAGES_PER_BLOCK, dtype=I32)
        phys = page_table[jnp.arange(bs)[:, None, None, None, None], logical]
        ob_s = _moba_sample(q_s, k_s, v_s, cache_k, cache_v, phys.reshape(-1).astype(I32), l)
        ob = jnp.concatenate([ob_p.transpose(0, 2, 1, 3).reshape(mp_rows, B_WIDTH),
                              ob_s.transpose(0, 2, 1, 3).reshape(ms_rows, B_WIDTH)], axis=0)

        w_router = jnp.concatenate(
            [w_group[l], w_expert_router[l].transpose(1, 0, 2).reshape(d, N_EXPERTS),
             jnp.zeros((d, LANES - N_GROUPS - N_EXPERTS), F32)], axis=1)
        b_router = jnp.concatenate(
            [b_group[l], b_expert_router[l].reshape(-1),
             jnp.zeros((LANES - N_GROUPS - N_EXPERTS,), F32)]).reshape(1, LANES)
        x1, h2, eid, gw = _merge(x, oa, ob, zg, mt, norm_ffn[l], w_proj_a[l].astype(BF16),
                                 w_proj_b[l].astype(BF16), w_out[l].astype(BF16), w_router, b_router,
                                 tiles_per_seq)

        dest, row_tok, block_expert, n_used = _dispatch(eid[:, :EXPERT_TOPK])
        ys = _moe(h2[row_tok], block_expert, n_used, w_gate, w_up, w_down, l)
        x = _final(x1, ys[dest[:, 0]], ys[dest[:, 1]], gw, mt, norm_final, tiles_per_seq, l == DEPTH - 1)

        kp_l.append(k_p); vp_l.append(v_p); sp_l.append(st_p)
        ks_l.append(k_s); vs_l.append(v_s); ss_l.append(st_s)

    y_prompt = x[:mp_rows].reshape(bp, tp, d)
    y_sample = x[mp_rows:].reshape(bs, ts, d)
    return (y_prompt, y_sample, jnp.stack(kp_l), jnp.stack(vp_l), jnp.stack(sp_l),
            jnp.stack(ks_l), jnp.stack(vs_l), jnp.stack(ss_l))
```

```python
import functools

import numpy as np
import jax
import jax.numpy as jnp
from jax import lax
from jax.experimental import pallas as pl
from jax.experimental.pallas import tpu as pltpu

F32 = jnp.float32
BF16 = jnp.bfloat16
I32 = jnp.int32

D_MODEL = 1024
DEPTH = 1
PAST_LEN = 16384
PAGE_SIZE = 128
A_HEADS = 4
A_KDIM = 128
A_VDIM = 128
A_WIDTH = A_HEADS * A_KDIM
B_HEADS = 8
B_HEAD_DIM = 64
B_WIDTH = B_HEADS * B_HEAD_DIM
MOBA_BLOCK = 256
MOBA_TOPK = 3
ROT_DIM = B_HEAD_DIM // 4
ROPE_THETA = 500000.0
N_GROUPS = 4
EXPERTS_PER_GROUP = 8
N_EXPERTS = N_GROUPS * EXPERTS_PER_GROUP
EXPERT_TOPK = 2
D_EXPERT = D_MODEL // 2
NORM_EPS = 1e-6

PAGES_PER_BLOCK = MOBA_BLOCK // PAGE_SIZE
ROW_TILE = 256
HGRN_CHUNK = 16
MOE_TILE = 256
LANES = 128
NEG_BIG = -1e30
VMEM_LIMIT_BYTES = 52 * 1024 * 1024


def _params(*sem):
    return pltpu.CompilerParams(dimension_semantics=sem, vmem_limit_bytes=VMEM_LIMIT_BYTES)


def _dot(a, b):
    return jnp.dot(a, b, preferred_element_type=F32)


def _dot_nt(a, b):
    return lax.dot_general(a, b, (((1,), (1,)), ((), ())), preferred_element_type=F32)


def _dot_tn(a, b):
    return lax.dot_general(a, b, (((0,), (0,)), ((), ())), preferred_element_type=F32)


def _split(a):
    hi = a.astype(BF16)
    return hi, (a - hi.astype(F32)).astype(BF16)


def _dot3(a, b, dot=_dot):
    ah, al = _split(a)
    bh, bl = _split(b)
    return dot(ah, bh) + (dot(ah, bl) + dot(al, bh))


def _silu(x):
    return x * jax.nn.sigmoid(x)


def _mod_kernel(c_ref, w_ref, b_ref, o_ref):
    o_ref[...] = _dot3(_silu(c_ref[...]), w_ref[...]) + b_ref[...]


def _modulation(c_all, w, b):
    n = c_all.shape[0]
    d, dout = w.shape
    return pl.pallas_call(
        _mod_kernel,
        grid=(dout // d,),
        in_specs=[pl.BlockSpec((n, d), lambda j: (0, 0)),
                  pl.BlockSpec((d, d), lambda j: (0, j)),
                  pl.BlockSpec((1, d), lambda j: (0, j))],
        out_specs=pl.BlockSpec((n, d), lambda j: (0, j)),
        out_shape=jax.ShapeDtypeStruct((n, dout), F32),
        compiler_params=_params("parallel"),
        name="modulation",
    )(c_all, w, b.reshape(1, dout))


def _inproj_kernel(x_ref, sh_ref, sc_ref, g_ref, w_ref, wlo_ref, za_ref, zb_ref, zg_ref):
    x = x_ref[...]
    h = x * lax.rsqrt(jnp.mean(x * x, axis=-1, keepdims=True) + NORM_EPS) * g_ref[...]
    h = h * (1.0 + sc_ref[0]) + sh_ref[0]
    hh, hl = _split(h)
    wa = 4 * A_WIDTH
    for c in range(0, wa, 512):
        za_ref[:, c:c + 512] = _dot(hh, w_ref[:, c:c + 512])
    for c in range(0, 2 * B_WIDTH, 512):
        wc = w_ref[:, wa + c:wa + c + 512]
        zb_ref[:, c:c + 512] = _dot(hh, wc) + (_dot(hh, wlo_ref[:, c:c + 512]) + _dot(hl, wc))
    c = 2 * B_WIDTH
    zb_ref[:, c:c + 512] = _dot(hh, w_ref[:, wa + c:wa + c + 512])
    wg = wa + 3 * B_WIDTH
    for c in range(0, 2 * D_MODEL, 512):
        zg_ref[:, c:c + 512] = _dot(hh, w_ref[:, wg + c:wg + c + 512])


def _inproj(x_all, mod_tiles, norm_g, w_in, tiles_per_seq):
    m, d = x_all.shape
    n_mod = mod_tiles.shape[1]
    w_hi = w_in.astype(BF16)
    wa = 4 * A_WIDTH
    wqk = w_in[:, wa:wa + 2 * B_WIDTH]
    w_lo = (wqk - wqk.astype(BF16).astype(F32)).astype(BF16)
    mod_map = lambda i: (jnp.minimum(i // tiles_per_seq, n_mod - 1), 0, 0)
    const = lambda i: (0, 0)
    row = lambda i: (i, 0)
    return pl.pallas_call(
        _inproj_kernel,
        grid=(m // ROW_TILE,),
        in_specs=[pl.BlockSpec((ROW_TILE, d), row),
                  pl.BlockSpec((1, ROW_TILE, d), mod_map),
                  pl.BlockSpec((1, ROW_TILE, d), mod_map),
                  pl.BlockSpec((1, d), const),
                  pl.BlockSpec(w_hi.shape, const, pipeline_mode=pl.Buffered(1)),
                  pl.BlockSpec(w_lo.shape, const, pipeline_mode=pl.Buffered(1))],
        out_specs=[pl.BlockSpec((ROW_TILE, wa), row),
                   pl.BlockSpec((ROW_TILE, 3 * B_WIDTH), row),
                   pl.BlockSpec((ROW_TILE, 2 * D_MODEL), row)],
        out_shape=[jax.ShapeDtypeStruct((m, wa), F32),
                   jax.ShapeDtypeStruct((m, 3 * B_WIDTH), F32),
                   jax.ShapeDtypeStruct((m, 2 * D_MODEL), F32)],
        compiler_params=_params("parallel"),
        name="inproj",
    )(x_all, mod_tiles[0], mod_tiles[1], norm_g.reshape(1, d), w_hi, w_lo)


def _rope_kernel(q_ref, k_ref, v_ref, c_ref, s1_ref, s2_ref, qo_ref, ko_ref, vo_ref):
    cos, s1, s2 = c_ref[...], s1_ref[...], s2_ref[...]
    half = ROT_DIM // 2

    def rope(x):
        up = pltpu.roll(x, B_WIDTH - half, axis=1)
        dn = pltpu.roll(x, half, axis=1)
        return x * cos + up * s1 + dn * s2

    q = rope(q_ref[...]) * (B_HEAD_DIM ** -0.5)
    k = rope(k_ref[...])
    v = v_ref[...]
    for h in range(B_HEADS):
        ls = slice(h * B_HEAD_DIM, (h + 1) * B_HEAD_DIM)
        qo_ref[0, h] = q[:, ls]
        ko_ref[0, h] = k[:, ls]
        vo_ref[0, h] = v[:, ls]


def _rope_tables(pos):
    half = ROT_DIM // 2
    inv = jnp.power(ROPE_THETA, -(jnp.arange(half, dtype=F32) * 2.0 / ROT_DIM))
    ang = pos.astype(F32)[:, None] * inv[None, :]
    cos, sin = jnp.cos(ang), jnp.sin(ang)
    t = pos.shape[0]
    rest = B_HEAD_DIM - ROT_DIM
    c = jnp.concatenate([cos, cos, jnp.ones((t, rest), F32)], axis=-1)
    s1 = jnp.concatenate([-sin, jnp.zeros((t, half + rest), F32)], axis=-1)
    s2 = jnp.concatenate([jnp.zeros((t, half), F32), sin, jnp.zeros((t, rest), F32)], axis=-1)
    return [jnp.tile(a, (1, B_HEADS)) for a in (c, s1, s2)]


def _rope_split(zb, row0, n, t, tile, pos):
    tabs = _rope_tables(pos)
    tps = t // tile
    blk0 = row0 // tile
    zmap = lambda c: (lambda b, s: (blk0 + b * tps + s, c))
    tmap = lambda b, s: (s, 0)
    omap = lambda b, s: (b, 0, s, 0)
    oshape = jax.ShapeDtypeStruct((n, B_HEADS, t, B_HEAD_DIM), F32)
    ospec = pl.BlockSpec((1, B_HEADS, tile, B_HEAD_DIM), omap)
    return pl.pallas_call(
        _rope_kernel,
        grid=(n, tps),
        in_specs=[pl.BlockSpec((tile, B_WIDTH), zmap(0)),
                  pl.BlockSpec((tile, B_WIDTH), zmap(1)),
                  pl.BlockSpec((tile, B_WIDTH), zmap(2)),
                  pl.BlockSpec((tile, B_WIDTH), tmap),
                  pl.BlockSpec((tile, B_WIDTH), tmap),
                  pl.BlockSpec((tile, B_WIDTH), tmap)],
        out_specs=[ospec, ospec, ospec],
        out_shape=[oshape, oshape, oshape],
        compiler_params=_params("parallel", "parallel"),
        name="rope_split",
    )(zb, zb, zb, *tabs)


def _hgrn_kernel(*refs, chunk, has_s0):
    aq_ref, af_ref, ai_ref, ag_ref, lb_ref, gain_ref = refs[:6]
    rest = refs[6:]
    s0_ref = None
    if has_s0:
        s0_ref, rest = rest[0], rest[1:]
    o_ref, so_ref, st_ref, q_s, b_s, k_s = rest
    t = pl.program_id(1)
    tb = aq_ref.shape[0]

    @pl.when(t == 0)
    def _():
        for h in range(A_HEADS):
            if has_s0:
                st_ref[h] = s0_ref[0, h].T
            else:
                st_ref[h] = jnp.zeros((A_VDIM, A_KDIM), F32)

    lb = lb_ref[...]
    f = lb + (1.0 - lb) * jax.nn.sigmoid(af_ref[...])
    logf = jnp.log(f)
    q_s[...] = _silu(aq_ref[...])
    k_s[...] = 1.0 - f
    row = lax.broadcasted_iota(I32, logf.shape, 0) & (chunk - 1)
    b = logf
    sh = 1
    while sh < chunk:
        b = b + jnp.where(row >= sh, pltpu.roll(b, sh, axis=0), 0.0)
        sh *= 2
    b_s[...] = b
    rowc = lax.broadcasted_iota(I32, (chunk, A_KDIM), 0)

    def one_chunk(ci, carry):
        r0 = pl.multiple_of(ci * chunk, chunk)
        rs = pl.ds(r0, chunk)
        for h in range(A_HEADS):
            ls = slice(h * A_KDIM, (h + 1) * A_KDIM)
            qc, bc, kc, vc = q_s[rs, ls], b_s[rs, ls], k_s[rs, ls], ai_ref[rs, ls]
            st = st_ref[h]
            bl = bc[chunk - 1:chunk, :]
            o = _dot_nt((qc * jnp.exp(bc)).astype(BF16), st.astype(BF16))
            for s in range(chunk):
                e = jnp.exp(jnp.where(rowc >= s, bc - bc[s:s + 1, :], -jnp.inf))
                r = jnp.sum(qc * e * kc[s:s + 1, :], axis=1, keepdims=True)
                o = o + r * vc[s:s + 1, :]
            kp = kc * jnp.exp(bl - bc)
            st_ref[h] = st * jnp.exp(bl) + _dot_tn(vc.astype(BF16), kp.astype(BF16))
            o_ref[rs, ls] = o
        return carry

    lax.fori_loop(0, tb // chunk, one_chunk, 0)

    gain = gain_ref[...]
    for h in range(A_HEADS):
        ls = slice(h * A_VDIM, (h + 1) * A_VDIM)
        oh = o_ref[:, ls]
        y = oh * lax.rsqrt(jnp.mean(oh * oh, axis=-1, keepdims=True) + NORM_EPS) * gain
        o_ref[:, ls] = y * _silu(ag_ref[:, ls])

    @pl.when(t == pl.num_programs(1) - 1)
    def _():
        for h in range(A_HEADS):
            so_ref[0, h] = st_ref[h].T


def _hgrn(za, lb, gain, row0, n, t, tile, chunk, s0=None):
    tps = t // tile
    blk0 = row0 // tile
    zmap = lambda c: (lambda b, s: (blk0 + b * tps + s, c))
    const = lambda b, s: (0, 0)
    in_specs = [pl.BlockSpec((tile, A_WIDTH), zmap(c)) for c in range(4)]
    in_specs += [pl.BlockSpec((1, A_WIDTH), const), pl.BlockSpec((1, A_VDIM), const)]
    args = [za, za, za, za, lb.reshape(1, A_WIDTH), gain.reshape(1, A_VDIM)]
    if s0 is not None:
        in_specs.append(pl.BlockSpec((1, A_HEADS, A_KDIM, A_VDIM), lambda b, s: (b, 0, 0, 0)))
        args.append(s0)
    return pl.pallas_call(
        functools.partial(_hgrn_kernel, chunk=chunk, has_s0=s0 is not None),
        grid=(n, tps),
        in_specs=in_specs,
        out_specs=[pl.BlockSpec((tile, A_WIDTH), lambda b, s: (b * tps + s, 0)),
                   pl.BlockSpec((1, A_HEADS, A_KDIM, A_VDIM), lambda b, s: (b, 0, 0, 0))],
        out_shape=[jax.ShapeDtypeStruct((n * t, A_WIDTH), F32),
                   jax.ShapeDtypeStruct((n, A_HEADS, A_KDIM, A_VDIM), F32)],
        scratch_shapes=[pltpu.VMEM((A_HEADS, A_VDIM, A_KDIM), F32),
                        pltpu.VMEM((tile, A_WIDTH), F32),
                        pltpu.VMEM((tile, A_WIDTH), F32),
                        pltpu.VMEM((tile, A_WIDTH), F32)],
        compiler_params=_params("parallel", "arbitrary"),
        name="hgrn2",
    )(*args)


MOBA_CHUNK_BLOCKS = 4


def _moba_prompt_kernel(q_ref, k_ref, v_ref, o_ref, km_ref, ka_ref, vt_ref):
    j = pl.program_id(2)
    blk = MOBA_BLOCK
    dh = q_ref.shape[3]
    n_chunks, kc, _ = ka_ref.shape
    n_blk = n_chunks * MOBA_CHUNK_BLOCKS
    eye = (lax.broadcasted_iota(I32, (dh, dh), 0) == lax.broadcasted_iota(I32, (dh, dh), 1)).astype(BF16)

    @pl.when(j == 0)
    def _():
        for i in range(n_blk):
            km_ref[i:i + 1, :] = jnp.mean(k_ref[0, 0, i * blk:(i + 1) * blk, :], axis=0, keepdims=True)
        for c in range(n_chunks):
            rows = slice(c * kc, (c + 1) * kc)
            key_blk = lax.broadcasted_iota(I32, (kc, n_blk), 0) // blk + c * MOBA_CHUNK_BLOCKS
            onehot = key_blk == lax.broadcasted_iota(I32, (kc, n_blk), 1)
            ka_ref[c, :, :dh] = k_ref[0, 0, rows, :].astype(BF16)
            ka_ref[c, :, dh:] = jnp.where(onehot, 1.0, 0.0).astype(BF16)
            vt_ref[c] = _dot_nt(eye, v_ref[0, 0, rows, :].astype(BF16)).astype(BF16)

    q = q_ref[0, 0]
    bidx = lax.broadcasted_iota(I32, (n_blk, blk), 0)
    valid = bidx < j
    sc = jnp.where(valid, _dot3(km_ref[...], q, _dot_nt), -jnp.inf)
    rank = jnp.zeros((n_blk, blk), I32)
    for i in range(n_blk - 1):
        row = sc[i:i + 1, :]
        rank = rank + jnp.where(row > sc, 1, jnp.where(row == sc, jnp.where(bidx > i, 1, 0), 0))
    bias_t = jnp.where(valid, jnp.where(rank < MOBA_TOPK, 0.0, NEG_BIG), jnp.where(bidx == j, 0.0, NEG_BIG))
    q_t = _dot_nt(eye, q.astype(BF16))
    qa = jnp.concatenate([q_t, bias_t], axis=0).astype(BF16)

    cj = j // MOBA_CHUNK_BLOCKS
    key_pos = lax.broadcasted_iota(I32, (kc, blk), 0) + cj * kc
    q_pos = lax.broadcasted_iota(I32, (kc, blk), 1) + j * blk
    s = jnp.where(key_pos <= q_pos, _dot(ka_ref[cj], qa), NEG_BIG)
    m = jnp.max(s, axis=0, keepdims=True)
    p = jnp.exp(s - m)
    l = jnp.sum(p, axis=0, keepdims=True)
    acc = _dot(vt_ref[cj], p.astype(BF16))

    def past_chunk(c, carry):
        m, l, acc = carry
        s = _dot(ka_ref[c], qa)
        m_new = jnp.maximum(m, jnp.max(s, axis=0, keepdims=True))
        p = jnp.exp(s - m_new)
        alpha = jnp.exp(m - m_new)
        return (m_new, alpha * l + jnp.sum(p, axis=0, keepdims=True),
                alpha * acc + _dot(vt_ref[c], p.astype(BF16)))

    _, l, acc = lax.fori_loop(0, cj, past_chunk, (m, l, acc))
    o_ref[0, 0] = acc / l


def _moba_prompt(q, k, v):
    b, h, s, dh = q.shape
    n_blk = s // MOBA_BLOCK
    assert n_blk % MOBA_CHUNK_BLOCKS == 0
    n_chunks = n_blk // MOBA_CHUNK_BLOCKS
    kc = MOBA_CHUNK_BLOCKS * MOBA_BLOCK
    full = pl.BlockSpec((1, 1, s, dh), lambda bi, hi, j: (bi, hi, 0, 0))
    return pl.pallas_call(
        _moba_prompt_kernel,
        grid=(b, h, n_blk),
        in_specs=[pl.BlockSpec((1, 1, MOBA_BLOCK, dh), lambda bi, hi, j: (bi, hi, j, 0)), full, full],
        out_specs=pl.BlockSpec((1, 1, dh, MOBA_BLOCK), lambda bi, hi, j: (bi, hi, 0, j)),
        out_shape=jax.ShapeDtypeStruct((b, h, dh, s), F32),
        scratch_shapes=[pltpu.VMEM((n_blk, dh), F32),
                        pltpu.VMEM((n_chunks, kc, dh + n_blk), BF16),
                        pltpu.VMEM((n_chunks, dh, kc), BF16)],
        compiler_params=_params("parallel", "parallel", "arbitrary"),
        name="moba_prompt",
    )(q, k, v)


PAGE_MEAN_TILE = 16


def _page_mean_kernel(c_ref, o_ref):
    x = c_ref[0]
    dh, ps = x.shape[-2], x.shape[-1]
    r = jnp.sum(x, axis=-1, keepdims=True) * (1.0 / ps)
    diag = lax.broadcasted_iota(I32, (dh, ps), 0) == lax.broadcasted_iota(I32, (dh, ps), 1)
    o_ref[...] = jnp.sum(jnp.where(diag, r, 0.0), axis=2)


def _page_mean(cache_kt, layer):
    _, n_pool, h, dh, ps = cache_kt.shape
    assert dh <= ps
    return pl.pallas_call(
        _page_mean_kernel,
        grid=(n_pool // PAGE_MEAN_TILE,),
        in_specs=[pl.BlockSpec((1, PAGE_MEAN_TILE, h, dh, ps), lambda i: (layer, i, 0, 0, 0))],
        out_specs=pl.BlockSpec((PAGE_MEAN_TILE, h, ps), lambda i: (i, 0, 0)),
        out_shape=jax.ShapeDtypeStruct((n_pool, h, ps), F32),
        compiler_params=_params("parallel"),
        name="page_mean",
    )(cache_kt)


def _select_kernel(q_ref, pe_ref, po_ref, o_ref):
    n_blk = pe_ref.shape[2]
    t = q_ref.shape[2]
    lane = lax.broadcasted_iota(I32, (t, n_blk), 1).astype(F32)
    for h in range(B_HEADS):
        bm = 0.5 * (pe_ref[0, h] + po_ref[0, h])
        sc = _dot3(q_ref[0, h], bm, _dot_nt)
        res = jnp.zeros((t, n_blk), F32)
        for r in range(MOBA_TOPK):
            m = jnp.max(sc, axis=1, keepdims=True)
            idx = jnp.min(jnp.where(sc == m, lane, float(n_blk)), axis=1, keepdims=True)
            res = jnp.where(lane == r, idx, res)
            sc = jnp.where(lane == idx, -jnp.inf, sc)
        o_ref[0, h] = res.astype(I32)


def _select_blocks(q, pm_even, pm_odd):
    n, h, t, dh = q.shape
    n_blk = pm_even.shape[2]
    qs = pl.BlockSpec((1, h, t, dh), lambda i: (i, 0, 0, 0))
    ps = pl.BlockSpec((1, h, n_blk, dh), lambda i: (i, 0, 0, 0))
    return pl.pallas_call(
        _select_kernel,
        grid=(n,),
        in_specs=[qs, ps, ps],
        out_specs=pl.BlockSpec((1, h, t, n_blk), lambda i: (i, 0, 0, 0)),
        out_shape=jax.ShapeDtypeStruct((n, h, t, n_blk), I32),
        compiler_params=_params("parallel"),
        name="moba_select",
    )(q, pm_even, pm_odd)


def _moba_sample_kernel(pg_ref, q_ref, kn_ref, vn_ref, ck_ref, cv_ref, o_ref, kbuf, vbuf, sem, *, layer):
    n_heads = pl.num_programs(1)
    step = pl.program_id(0) * n_heads + pl.program_id(1)
    n_steps = pl.num_programs(0) * n_heads
    t = q_ref.shape[2]
    n_pages = kbuf.shape[1]
    pages_per_q = n_pages // t
    slot = step % 2

    def page_copies(st, sl, i):
        pg = pg_ref[st * n_pages + i]
        hh = st % n_heads
        return (pltpu.make_async_copy(ck_ref.at[layer, pg, hh], kbuf.at[sl, i], sem.at[0, sl]),
                pltpu.make_async_copy(cv_ref.at[layer, pg, hh], vbuf.at[sl, i], sem.at[1, sl]))

    def fetch(st, sl):
        for i in range(n_pages):
            for cp in page_copies(st, sl, i):
                cp.start()

    @pl.when(step == 0)
    def _():
        fetch(step, slot)

    @pl.when(step + 1 < n_steps)
    def _():
        fetch(step + 1, 1 - slot)

    for i in range(n_pages):
        for cp in page_copies(step, slot, i):
            cp.wait()

    q = q_ref[0, 0]
    qb = q.astype(BF16)
    s = jnp.concatenate([_dot(qb, kbuf[slot, i].astype(BF16)) for i in range(n_pages)], axis=1)
    per_q = pages_per_q * PAGE_SIZE
    col = lax.broadcasted_iota(I32, s.shape, 1)
    lo = lax.broadcasted_iota(I32, s.shape, 0) * per_q
    own = (col >= lo) & (col < lo + per_q)
    sn = _dot_nt(q, kn_ref[0, 0])
    causal = lax.broadcasted_iota(I32, (t, t), 1) <= lax.broadcasted_iota(I32, (t, t), 0)
    m = jnp.maximum(jnp.max(jnp.where(own, s, NEG_BIG), axis=1, keepdims=True),
                    jnp.max(jnp.where(causal, sn, NEG_BIG), axis=1, keepdims=True))
    p = jnp.where(own, jnp.exp(s - m), 0.0).astype(BF16)
    pn = jnp.where(causal, jnp.exp(sn - m), 0.0)
    l = jnp.sum(p.astype(F32), axis=1, keepdims=True) + jnp.sum(pn, axis=1, keepdims=True)
    o = _dot(pn, vn_ref[0, 0])
    for i in range(n_pages):
        o = o + _dot_nt(p[:, i * PAGE_SIZE:(i + 1) * PAGE_SIZE], vbuf[slot, i].astype(BF16))
    o_ref[0, 0] = o / l


def _moba_sample(q, k_new, v_new, cache_kt, cache_vt, pages, layer):
    n, h, t, dh = q.shape
    n_pages = t * MOBA_TOPK * PAGES_PER_BLOCK
    blk = pl.BlockSpec((1, 1, t, dh), lambda i, j, pg: (i, j, 0, 0))
    anyspec = pl.BlockSpec(memory_space=pl.ANY)
    return pl.pallas_call(
        functools.partial(_moba_sample_kernel, layer=layer),
        grid_spec=pltpu.PrefetchScalarGridSpec(
            num_scalar_prefetch=1,
            grid=(n, h),
            in_specs=[blk, blk, blk, anyspec, anyspec],
            out_specs=blk,
            scratch_shapes=[pltpu.VMEM((2, n_pages, dh, PAGE_SIZE), F32),
                            pltpu.VMEM((2, n_pages, dh, PAGE_SIZE), F32),
                            pltpu.SemaphoreType.DMA((2, 2))]),
        out_shape=jax.ShapeDtypeStruct((n, h, t, dh), F32),
        compiler_params=_params("arbitrary", "arbitrary"),
        name="moba_sample",
    )(pages, q, k_new, v_new, cache_kt, cache_vt)


def _merge_kernel(x_ref, oa_ref, ob_ref, ga_ref, gb_ref, g1_ref, sh_ref, sc_ref, nf_ref,
                  wpa_ref, wpb_ref, wo_ref, wr_ref, br_ref, x1_ref, h2_ref, eid_ref, gw_ref):
    pa = _dot(oa_ref[...].astype(BF16), wpa_ref[...])
    pb = _dot(ob_ref[...].astype(BF16), wpb_ref[...])
    mix = jax.nn.sigmoid(ga_ref[...]) * pa + jax.nn.sigmoid(gb_ref[...]) * pb
    x1 = x_ref[...] + g1_ref[0] * _dot(mix.astype(BF16), wo_ref[...])
    x1_ref[...] = x1
    h2 = x1 * lax.rsqrt(jnp.mean(x1 * x1, axis=-1, keepdims=True) + NORM_EPS) * nf_ref[...]
    h2 = h2 * (1.0 + sc_ref[0]) + sh_ref[0]
    h2_ref[...] = h2.astype(BF16)

    logits = _dot3(h2, wr_ref[...]) + br_ref[...]
    lane = lax.broadcasted_iota(I32, logits.shape, 1)
    lanef = lane.astype(F32)
    is_g = lane < N_GROUPS
    gl = jnp.where(is_g, logits, -jnp.inf)
    gmax = jnp.max(gl, axis=1, keepdims=True)
    g_sel = jnp.min(jnp.where(gl == gmax, lanef, float(LANES)), axis=1, keepdims=True)
    p_group = 1.0 / jnp.sum(jnp.exp(gl - gmax), axis=1, keepdims=True)
    e_lo = N_GROUPS + g_sel * EXPERTS_PER_GROUP
    in_grp = (lanef >= e_lo) & (lanef < e_lo + EXPERTS_PER_GROUP)
    el = jnp.where(in_grp, logits, -jnp.inf)
    m1 = jnp.max(el, axis=1, keepdims=True)
    i1 = jnp.min(jnp.where(el == m1, lanef, float(LANES)), axis=1, keepdims=True)
    el2 = jnp.where(lanef == i1, -jnp.inf, el)
    m2 = jnp.max(el2, axis=1, keepdims=True)
    i2 = jnp.min(jnp.where(el2 == m2, lanef, float(LANES)), axis=1, keepdims=True)
    e2 = jnp.exp(m2 - m1)
    w1 = p_group / (1.0 + e2)
    w2 = p_group * e2 / (1.0 + e2)
    eid = jnp.where(lane == 0, i1 - N_GROUPS, jnp.where(lane == 1, i2 - N_GROUPS, 0.0))
    eid_ref[...] = eid.astype(I32)
    gw_ref[...] = jnp.where(lane == 0, w1, jnp.where(lane == 1, w2, 0.0))


def _merge(x_all, oa, ob, zg, mod_tiles, norm_ffn, wpa, wpb, wo, w_router, b_router, tiles_per_seq):
    m, d = x_all.shape
    n_mod = mod_tiles.shape[1]
    mod_map = lambda i: (jnp.minimum(i // tiles_per_seq, n_mod - 1), 0, 0)
    const = lambda i: (0, 0)
    row = lambda i: (i, 0)
    mod = pl.BlockSpec((1, ROW_TILE, d), mod_map)
    wspec = lambda w: pl.BlockSpec(w.shape, const, pipeline_mode=pl.Buffered(1))
    return pl.pallas_call(
        _merge_kernel,
        grid=(m // ROW_TILE,),
        in_specs=[pl.BlockSpec((ROW_TILE, d), row),
                  pl.BlockSpec((ROW_TILE, A_WIDTH), row),
                  pl.BlockSpec((ROW_TILE, B_WIDTH), row),
                  pl.BlockSpec((ROW_TILE, d), lambda i: (i, 0)),
                  pl.BlockSpec((ROW_TILE, d), lambda i: (i, 1)),
                  mod, mod, mod,
                  pl.BlockSpec((1, d), const),
                  wspec(wpa), wspec(wpb), wspec(wo), wspec(w_router),
                  pl.BlockSpec((1, LANES), const)],
        out_specs=[pl.BlockSpec((ROW_TILE, d), row),
                   pl.BlockSpec((ROW_TILE, d), row),
                   pl.BlockSpec((ROW_TILE, LANES), row),
                   pl.BlockSpec((ROW_TILE, LANES), row)],
        out_shape=[jax.ShapeDtypeStruct((m, d), F32),
                   jax.ShapeDtypeStruct((m, d), BF16),
                   jax.ShapeDtypeStruct((m, LANES), I32),
                   jax.ShapeDtypeStruct((m, LANES), F32)],
        compiler_params=_params("parallel"),
        name="merge_router",
    )(x_all, oa, ob, zg, zg, mod_tiles[2], mod_tiles[3], mod_tiles[4], norm_ffn.reshape(1, d),
      wpa, wpb, wo, w_router, b_router)


def _moe_kernel(be_ref, nu_ref, x_ref, wg_ref, wu_ref, wd_ref, y_ref, wg_s, wu_s, wd_s):
    i = pl.program_id(0)
    prev = be_ref[jnp.maximum(i - 1, 0)]
    fresh = (i == 0) | (be_ref[i] != prev)

    @pl.when(fresh)
    def _():
        wg_s[...] = wg_ref[...].astype(BF16)
        wu_s[...] = wu_ref[...].astype(BF16)
        wd_s[...] = wd_ref[...].astype(BF16)

    @pl.when(i < nu_ref[0])
    def _():
        xb = x_ref[...]
        hid = _silu(_dot(xb, wg_s[...])) * _dot(xb, wu_s[...])
        y_ref[...] = _dot(hid.astype(BF16), wd_s[...])

    @pl.when(i >= nu_ref[0])
    def _():
        y_ref[...] = jnp.zeros(y_ref.shape, F32)


def _moe(xs, block_expert, n_used, w_gate, w_up, w_down, layer):
    n_rows, d = xs.shape
    de = w_gate.shape[-1]
    return pl.pallas_call(
        _moe_kernel,
        grid_spec=pltpu.PrefetchScalarGridSpec(
            num_scalar_prefetch=2,
            grid=(n_rows // MOE_TILE,),
            in_specs=[pl.BlockSpec((MOE_TILE, d), lambda i, be, nu: (i, 0)),
                      pl.BlockSpec((None, None, d, de), lambda i, be, nu: (layer, be[i], 0, 0)),
                      pl.BlockSpec((None, None, d, de), lambda i, be, nu: (layer, be[i], 0, 0)),
                      pl.BlockSpec((None, None, de, d), lambda i, be, nu: (layer, be[i], 0, 0))],
            out_specs=pl.BlockSpec((MOE_TILE, d), lambda i, be, nu: (i, 0)),
            scratch_shapes=[pltpu.VMEM((d, de), BF16), pltpu.VMEM((d, de), BF16), pltpu.VMEM((de, d), BF16)]),
        out_shape=jax.ShapeDtypeStruct((n_rows, d), F32),
        compiler_params=_params("arbitrary"),
        name="moe_experts",
    )(block_expert, n_used, xs, w_gate, w_up, w_down)


def _final_kernel(x1_ref, y0_ref, y1_ref, gw_ref, g2_ref, nf_ref, o_ref, *, last):
    gw = gw_ref[...]
    f = gw[:, 0:1] * y0_ref[...] + gw[:, 1:2] * y1_ref[...]
    x2 = x1_ref[...] + g2_ref[0] * f
    if last:
        x2 = x2 * lax.rsqrt(jnp.mean(x2 * x2, axis=-1, keepdims=True) + NORM_EPS) * nf_ref[...]
    o_ref[...] = x2


def _final(x1, y0, y1, gw, mod_tiles, norm_final, tiles_per_seq, last):
    m, d = x1.shape
    n_mod = mod_tiles.shape[1]
    mod_map = lambda i: (jnp.minimum(i // tiles_per_seq, n_mod - 1), 0, 0)
    row = lambda i: (i, 0)
    rs = pl.BlockSpec((ROW_TILE, d), row)
    return pl.pallas_call(
        functools.partial(_final_kernel, last=last),
        grid=(m // ROW_TILE,),
        in_specs=[rs, rs, rs, pl.BlockSpec((ROW_TILE, LANES), row),
                  pl.BlockSpec((1, ROW_TILE, d), mod_map),
                  pl.BlockSpec((1, d), lambda i: (0, 0))],
        out_specs=rs,
        out_shape=jax.ShapeDtypeStruct((m, d), F32),
        compiler_params=_params("parallel"),
        name="combine_final",
    )(x1, y0, y1, gw, mod_tiles[5], norm_final.reshape(1, d))


def _dispatch(eid):
    m = eid.shape[0]
    n_assign = m * EXPERT_TOPK
    flat_e = eid.reshape(-1)
    flat_tok = jnp.repeat(jnp.arange(m, dtype=I32), EXPERT_TOPK)
    onehot = (flat_e[:, None] == jnp.arange(N_EXPERTS, dtype=I32)[None, :]).astype(I32)
    csum = jnp.cumsum(onehot, axis=0)
    rank = jnp.take_along_axis(csum, flat_e[:, None], axis=1)[:, 0] - 1
    counts = csum[-1]
    padded = (counts + MOE_TILE - 1) // MOE_TILE * MOE_TILE
    pad_end = jnp.cumsum(padded)
    pad_start = pad_end - padded
    dest = (pad_start[flat_e] + rank).astype(I32)
    n_blocks = -(-(n_assign + N_EXPERTS * (MOE_TILE - 1)) // MOE_TILE)
    row_tok = jnp.zeros((n_blocks * MOE_TILE,), I32).at[dest].set(flat_tok)
    blk_start = jnp.arange(n_blocks, dtype=I32) * MOE_TILE
    block_expert = jnp.minimum(jnp.sum((pad_end[None, :] <= blk_start[:, None]).astype(I32), axis=1),
                               N_EXPERTS - 1).astype(I32)
    n_used = (pad_end[-1:] // MOE_TILE).astype(I32)
    return dest.reshape(m, EXPERT_TOPK), row_tok, block_expert, n_used


def _mod_tiles(mod, n_prompt, reps):
    d = mod.shape[-1]
    mp = jnp.broadcast_to(mod[:n_prompt, :, None, :], (n_prompt, 6, ROW_TILE, d))
    ms = jnp.repeat(mod[n_prompt:], reps, axis=0).transpose(1, 0, 2)[None]
    return jnp.concatenate([mp, ms], axis=0).transpose(1, 0, 2, 3)


def kernel(x_prompt, x_sample, c_prompt, c_sample, cache_k, cache_v, state_hgrn, page_table,
           w_ada, b_ada, norm_mix, norm_ffn, w_in, hgrn_lb_logits, hgrn_norm,
           w_proj_a, w_proj_b, w_out, w_group, b_group, w_expert_router, b_expert_router,
           w_gate, w_up, w_down, norm_final):
    bp, tp, d = x_prompt.shape
    bs, ts, _ = x_sample.shape
    mp_rows, ms_rows = bp * tp, bs * ts
    assert ms_rows == ROW_TILE and tp % ROW_TILE == 0
    m = mp_rows + ms_rows
    tiles_per_seq = tp // ROW_TILE
    n_full = PAST_LEN // MOBA_BLOCK
    assert PAST_LEN % MOBA_BLOCK == 0 and n_full >= MOBA_TOPK

    x = jnp.concatenate([x_prompt.reshape(mp_rows, d), x_sample.reshape(ms_rows, d)], axis=0)
    c_all = jnp.concatenate([c_prompt, c_sample, jnp.zeros((-(bp + bs) % 8, d), F32)], axis=0)
    lb_all = jnp.cumsum(jax.nn.softmax(hgrn_lb_logits.astype(F32), axis=0), axis=0)
    cache_kt = cache_k.transpose(0, 1, 2, 4, 3)
    cache_vt = cache_v.transpose(0, 1, 2, 4, 3)
    kp_l, vp_l, sp_l, ks_l, vs_l, ss_l = [], [], [], [], [], []

    for l in range(DEPTH):
        mod = _modulation(c_all, w_ada[l], b_ada[l])[:bp + bs].reshape(bp + bs, 6, d)
        mt = _mod_tiles(mod, bp, ts)
        za, zb, zg = _inproj(x, mt, norm_mix[l], w_in[l], tiles_per_seq)

        oa_p, st_p = _hgrn(za, lb_all[l], hgrn_norm[l], 0, bp, tp, ROW_TILE, HGRN_CHUNK)
        oa_s, st_s = _hgrn(za, lb_all[l], hgrn_norm[l], mp_rows, bs, ts, ts, ts, s0=state_hgrn[l])
        oa = jnp.concatenate([oa_p, oa_s], axis=0)

        q_p, k_p, v_p = _rope_split(zb, 0, bp, tp, 512, jnp.arange(tp))
        q_s, k_s, v_s = _rope_split(zb, mp_rows, bs, ts, ts, PAST_LEN + jnp.arange(ts))
        ob_p = _moba_prompt(q_p, k_p, v_p)
        pm = _page_mean(cache_kt, l)
        pm_g = pm[page_table[:, :n_full * PAGES_PER_BLOCK]][..., :B_HEAD_DIM].transpose(0, 2, 1, 3)
        sel = _select_blocks(q_s, pm_g[:, :, 0::2], pm_g[:, :, 1::2])[..., :MOBA_TOPK]
        logical = sel[..., None] * PAGES_PER_BLOCK + jnp.arange(PAGES_PER_BLOCK, dtype=I32)
        phys = page_table[jnp.arange(bs)[:, None, None, None, None], logical]
        ob_s = _moba_sample(q_s, k_s, v_s, cache_kt, cache_vt, phys.reshape(-1).astype(I32), l)
        ob = jnp.concatenate([ob_p.transpose(0, 3, 1, 2).reshape(mp_rows, B_WIDTH),
                              ob_s.transpose(0, 2, 1, 3).reshape(ms_rows, B_WIDTH)], axis=0)

        w_router = jnp.concatenate(
            [w_group[l], w_expert_router[l].transpose(1, 0, 2).reshape(d, N_EXPERTS),
             jnp.zeros((d, LANES - N_GROUPS - N_EXPERTS), F32)], axis=1)
        b_router = jnp.concatenate(
            [b_group[l], b_expert_router[l].reshape(-1),
             jnp.zeros((LANES - N_GROUPS - N_EXPERTS,), F32)]).reshape(1, LANES)
        x1, h2, eid, gw = _merge(x, oa, ob, zg, mt, norm_ffn[l], w_proj_a[l].astype(BF16),
                                 w_proj_b[l].astype(BF16), w_out[l].astype(BF16), w_router, b_router,
                                 tiles_per_seq)

        dest, row_tok, block_expert, n_used = _dispatch(eid[:, :EXPERT_TOPK])
        ys = _moe(h2[row_tok], block_expert, n_used, w_gate, w_up, w_down, l)
        x = _final(x1, ys[dest[:, 0]], ys[dest[:, 1]], gw, mt, norm_final, tiles_per_seq, l == DEPTH - 1)

        kp_l.append(k_p); vp_l.append(v_p); sp_l.append(st_p)
        ks_l.append(k_s); vs_l.append(v_s); ss_l.append(st_s)

    y_prompt = x[:mp_rows].reshape(bp, tp, d)
    y_sample = x[mp_rows:].reshape(bs, ts, d)
    return (y_prompt, y_sample, jnp.stack(kp_l), jnp.stack(vp_l), jnp.stack(sp_l),
            jnp.stack(ks_l), jnp.stack(vs_l), jnp.stack(ss_l))
```

```python
import functools

import numpy as np
import jax
import jax.numpy as jnp
from jax import lax
from jax.experimental import pallas as pl
from jax.experimental.pallas import tpu as pltpu

F32 = jnp.float32
BF16 = jnp.bfloat16
I32 = jnp.int32

D_MODEL = 1024
DEPTH = 1
PAST_LEN = 16384
PAGE_SIZE = 128
A_HEADS = 4
A_KDIM = 128
A_VDIM = 128
A_WIDTH = A_HEADS * A_KDIM
B_HEADS = 8
B_HEAD_DIM = 64
B_WIDTH = B_HEADS * B_HEAD_DIM
MOBA_BLOCK = 256
MOBA_TOPK = 3
ROT_DIM = B_HEAD_DIM // 4
ROPE_THETA = 500000.0
N_GROUPS = 4
EXPERTS_PER_GROUP = 8
N_EXPERTS = N_GROUPS * EXPERTS_PER_GROUP
EXPERT_TOPK = 2
D_EXPERT = D_MODEL // 2
NORM_EPS = 1e-6

PAGES_PER_BLOCK = MOBA_BLOCK // PAGE_SIZE
ROW_TILE = 256
HGRN_CHUNK = 16
MOE_TILE = 256
LANES = 128
NEG_BIG = -1e30
VMEM_LIMIT_BYTES = 52 * 1024 * 1024


def _params(*sem):
    return pltpu.CompilerParams(dimension_semantics=sem, vmem_limit_bytes=VMEM_LIMIT_BYTES)


def _dot(a, b):
    return jnp.dot(a, b, preferred_element_type=F32)


def _dot_nt(a, b):
    return lax.dot_general(a, b, (((1,), (1,)), ((), ())), preferred_element_type=F32)


def _dot_tn(a, b):
    return lax.dot_general(a, b, (((0,), (0,)), ((), ())), preferred_element_type=F32)


def _split(a):
    hi = a.astype(BF16)
    return hi, (a - hi.astype(F32)).astype(BF16)


def _dot3(a, b, dot=_dot):
    ah, al = _split(a)
    bh, bl = _split(b)
    return dot(ah, bh) + (dot(ah, bl) + dot(al, bh))


def _silu(x):
    return x * jax.nn.sigmoid(x)


def _mod_kernel(c_ref, w_ref, b_ref, o_ref):
    o_ref[...] = _dot3(_silu(c_ref[...]), w_ref[...]) + b_ref[...]


def _modulation(c_all, w, b):
    n = c_all.shape[0]
    d, dout = w.shape
    return pl.pallas_call(
        _mod_kernel,
        grid=(dout // d,),
        in_specs=[pl.BlockSpec((n, d), lambda j: (0, 0)),
                  pl.BlockSpec((d, d), lambda j: (0, j)),
                  pl.BlockSpec((1, d), lambda j: (0, j))],
        out_specs=pl.BlockSpec((n, d), lambda j: (0, j)),
        out_shape=jax.ShapeDtypeStruct((n, dout), F32),
        compiler_params=_params("parallel"),
        name="modulation",
    )(c_all, w, b.reshape(1, dout))


def _pick_rows(n_first, first_ref, second_ref):
    return jnp.where(pl.program_id(0) < n_first, first_ref[...], second_ref[...])


def _two_group_specs(width, n_first):
    return [pl.BlockSpec((ROW_TILE, width), lambda i: (jnp.minimum(i, n_first - 1), 0)),
            pl.BlockSpec((ROW_TILE, width), lambda i: (0, 0))]


def _inproj_kernel(xp_ref, xs_ref, sh_ref, sc_ref, g_ref, w_ref, wlo_ref, za_ref, zb_ref, zg_ref, *, n_first):
    x = _pick_rows(n_first, xp_ref, xs_ref)
    h = x * lax.rsqrt(jnp.mean(x * x, axis=-1, keepdims=True) + NORM_EPS) * g_ref[...]
    h = h * (1.0 + sc_ref[0]) + sh_ref[0]
    hh, hl = _split(h)
    wa = 4 * A_WIDTH
    for c in range(0, wa, 512):
        za_ref[:, c:c + 512] = _dot(hh, w_ref[:, c:c + 512])
    for c in range(0, 2 * B_WIDTH, 512):
        wc = w_ref[:, wa + c:wa + c + 512]
        zb_ref[:, c:c + 512] = _dot(hh, wc) + (_dot(hh, wlo_ref[:, c:c + 512]) + _dot(hl, wc))
    c = 2 * B_WIDTH
    zb_ref[:, c:c + 512] = _dot(hh, w_ref[:, wa + c:wa + c + 512])
    wg = wa + 3 * B_WIDTH
    for c in range(0, 2 * D_MODEL, 512):
        zg_ref[:, c:c + 512] = _dot(hh, w_ref[:, wg + c:wg + c + 512])


def _inproj(xp, xs, mod_tiles, norm_g, w_in, tiles_per_seq):
    d = xp.shape[1]
    n_first = xp.shape[0] // ROW_TILE
    m = xp.shape[0] + xs.shape[0]
    n_mod = mod_tiles.shape[1]
    w_hi = w_in.astype(BF16)
    wa = 4 * A_WIDTH
    wqk = w_in[:, wa:wa + 2 * B_WIDTH]
    w_lo = (wqk - wqk.astype(BF16).astype(F32)).astype(BF16)
    mod_map = lambda i: (jnp.minimum(i // tiles_per_seq, n_mod - 1), 0, 0)
    const = lambda i: (0, 0)
    row = lambda i: (i, 0)
    return pl.pallas_call(
        functools.partial(_inproj_kernel, n_first=n_first),
        grid=(m // ROW_TILE,),
        in_specs=_two_group_specs(d, n_first) + [
                  pl.BlockSpec((1, ROW_TILE, d), mod_map),
                  pl.BlockSpec((1, ROW_TILE, d), mod_map),
                  pl.BlockSpec((1, d), const),
                  pl.BlockSpec(w_hi.shape, const, pipeline_mode=pl.Buffered(1)),
                  pl.BlockSpec(w_lo.shape, const, pipeline_mode=pl.Buffered(1))],
        out_specs=[pl.BlockSpec((ROW_TILE, wa), row),
                   pl.BlockSpec((ROW_TILE, 3 * B_WIDTH), row),
                   pl.BlockSpec((ROW_TILE, 2 * D_MODEL), row)],
        out_shape=[jax.ShapeDtypeStruct((m, wa), F32),
                   jax.ShapeDtypeStruct((m, 3 * B_WIDTH), F32),
                   jax.ShapeDtypeStruct((m, 2 * D_MODEL), F32)],
        compiler_params=_params("parallel"),
        name="inproj",
    )(xp, xs, mod_tiles[0], mod_tiles[1], norm_g.reshape(1, d), w_hi, w_lo)


def _rope_kernel(q_ref, k_ref, v_ref, c_ref, s1_ref, s2_ref, qo_ref, ko_ref, vo_ref):
    cos, s1, s2 = c_ref[...], s1_ref[...], s2_ref[...]
    half = ROT_DIM // 2

    def rope(x):
        up = pltpu.roll(x, B_WIDTH - half, axis=1)
        dn = pltpu.roll(x, half, axis=1)
        return x * cos + up * s1 + dn * s2

    q = rope(q_ref[...]) * (B_HEAD_DIM ** -0.5)
    k = rope(k_ref[...])
    v = v_ref[...]
    for h in range(B_HEADS):
        ls = slice(h * B_HEAD_DIM, (h + 1) * B_HEAD_DIM)
        qo_ref[0, h] = q[:, ls]
        ko_ref[0, h] = k[:, ls]
        vo_ref[0, h] = v[:, ls]


def _rope_tables(pos):
    half = ROT_DIM // 2
    inv = jnp.power(ROPE_THETA, -(jnp.arange(half, dtype=F32) * 2.0 / ROT_DIM))
    ang = pos.astype(F32)[:, None] * inv[None, :]
    cos, sin = jnp.cos(ang), jnp.sin(ang)
    t = pos.shape[0]
    rest = B_HEAD_DIM - ROT_DIM
    c = jnp.concatenate([cos, cos, jnp.ones((t, rest), F32)], axis=-1)
    s1 = jnp.concatenate([-sin, jnp.zeros((t, half + rest), F32)], axis=-1)
    s2 = jnp.concatenate([jnp.zeros((t, half), F32), sin, jnp.zeros((t, rest), F32)], axis=-1)
    return [jnp.tile(a, (1, B_HEADS)) for a in (c, s1, s2)]


def _rope_split(zb, row0, n, t, tile, pos):
    tabs = _rope_tables(pos)
    tps = t // tile
    blk0 = row0 // tile
    zmap = lambda c: (lambda b, s: (blk0 + b * tps + s, c))
    tmap = lambda b, s: (s, 0)
    omap = lambda b, s: (b, 0, s, 0)
    oshape = jax.ShapeDtypeStruct((n, B_HEADS, t, B_HEAD_DIM), F32)
    ospec = pl.BlockSpec((1, B_HEADS, tile, B_HEAD_DIM), omap)
    return pl.pallas_call(
        _rope_kernel,
        grid=(n, tps),
        in_specs=[pl.BlockSpec((tile, B_WIDTH), zmap(0)),
                  pl.BlockSpec((tile, B_WIDTH), zmap(1)),
                  pl.BlockSpec((tile, B_WIDTH), zmap(2)),
                  pl.BlockSpec((tile, B_WIDTH), tmap),
                  pl.BlockSpec((tile, B_WIDTH), tmap),
                  pl.BlockSpec((tile, B_WIDTH), tmap)],
        out_specs=[ospec, ospec, ospec],
        out_shape=[oshape, oshape, oshape],
        compiler_params=_params("parallel", "parallel"),
        name="rope_split",
    )(zb, zb, zb, *tabs)


def _hgrn_kernel(*refs, chunk, has_s0):
    aq_ref, af_ref, ai_ref, ag_ref, lb_ref, gain_ref = refs[:6]
    rest = refs[6:]
    s0_ref = None
    if has_s0:
        s0_ref, rest = rest[0], rest[1:]
    o_ref, so_ref, st_ref, q_s, b_s, k_s = rest
    t = pl.program_id(1)
    tb = aq_ref.shape[0]

    @pl.when(t == 0)
    def _():
        for h in range(A_HEADS):
            if has_s0:
                st_ref[h] = s0_ref[0, h].T
            else:
                st_ref[h] = jnp.zeros((A_VDIM, A_KDIM), F32)

    lb = lb_ref[...]
    f = lb + (1.0 - lb) * jax.nn.sigmoid(af_ref[...])
    logf = jnp.log(f)
    q_s[...] = _silu(aq_ref[...])
    k_s[...] = 1.0 - f
    row = lax.broadcasted_iota(I32, logf.shape, 0) & (chunk - 1)
    b = logf
    sh = 1
    while sh < chunk:
        b = b + jnp.where(row >= sh, pltpu.roll(b, sh, axis=0), 0.0)
        sh *= 2
    b_s[...] = b
    rowc = lax.broadcasted_iota(I32, (chunk, A_KDIM), 0)

    def one_chunk(ci, carry):
        r0 = pl.multiple_of(ci * chunk, chunk)
        rs = pl.ds(r0, chunk)
        for h in range(A_HEADS):
            ls = slice(h * A_KDIM, (h + 1) * A_KDIM)
            qc, bc, kc, vc = q_s[rs, ls], b_s[rs, ls], k_s[rs, ls], ai_ref[rs, ls]
            st = st_ref[h]
            bl = bc[chunk - 1:chunk, :]
            o = _dot_nt((qc * jnp.exp(bc)).astype(BF16), st.astype(BF16))
            for s in range(chunk):
                e = jnp.exp(jnp.where(rowc >= s, bc - bc[s:s + 1, :], -jnp.inf))
                r = jnp.sum(qc * e * kc[s:s + 1, :], axis=1, keepdims=True)
                o = o + r * vc[s:s + 1, :]
            kp = kc * jnp.exp(bl - bc)
            st_ref[h] = st * jnp.exp(bl) + _dot_tn(vc.astype(BF16), kp.astype(BF16))
            o_ref[rs, ls] = o
        return carry

    lax.fori_loop(0, tb // chunk, one_chunk, 0)

    gain = gain_ref[...]
    for h in range(A_HEADS):
        ls = slice(h * A_VDIM, (h + 1) * A_VDIM)
        oh = o_ref[:, ls]
        y = oh * lax.rsqrt(jnp.mean(oh * oh, axis=-1, keepdims=True) + NORM_EPS) * gain
        o_ref[:, ls] = y * _silu(ag_ref[:, ls])

    @pl.when(t == pl.num_programs(1) - 1)
    def _():
        for h in range(A_HEADS):
            so_ref[0, h] = st_ref[h].T


def _hgrn(za, lb, gain, row0, n, t, tile, chunk, s0=None):
    tps = t // tile
    blk0 = row0 // tile
    zmap = lambda c: (lambda b, s: (blk0 + b * tps + s, c))
    const = lambda b, s: (0, 0)
    in_specs = [pl.BlockSpec((tile, A_WIDTH), zmap(c)) for c in range(4)]
    in_specs += [pl.BlockSpec((1, A_WIDTH), const), pl.BlockSpec((1, A_VDIM), const)]
    args = [za, za, za, za, lb.reshape(1, A_WIDTH), gain.reshape(1, A_VDIM)]
    if s0 is not None:
        in_specs.append(pl.BlockSpec((1, A_HEADS, A_KDIM, A_VDIM), lambda b, s: (b, 0, 0, 0)))
        args.append(s0)
    return pl.pallas_call(
        functools.partial(_hgrn_kernel, chunk=chunk, has_s0=s0 is not None),
        grid=(n, tps),
        in_specs=in_specs,
        out_specs=[pl.BlockSpec((tile, A_WIDTH), lambda b, s: (b * tps + s, 0)),
                   pl.BlockSpec((1, A_HEADS, A_KDIM, A_VDIM), lambda b, s: (b, 0, 0, 0))],
        out_shape=[jax.ShapeDtypeStruct((n * t, A_WIDTH), F32),
                   jax.ShapeDtypeStruct((n, A_HEADS, A_KDIM, A_VDIM), F32)],
        scratch_shapes=[pltpu.VMEM((A_HEADS, A_VDIM, A_KDIM), F32),
                        pltpu.VMEM((tile, A_WIDTH), F32),
                        pltpu.VMEM((tile, A_WIDTH), F32),
                        pltpu.VMEM((tile, A_WIDTH), F32)],
        compiler_params=_params("parallel", "arbitrary"),
        name="hgrn2",
    )(*args)


MOBA_CHUNK_BLOCKS = 4
MOBA_HEADS_PER_STEP = 2


def _moba_prompt_kernel(q_ref, k_ref, v_ref, o_ref, km_ref, ka_ref, vt_ref, eye_ref):
    j = pl.program_id(2)
    blk = MOBA_BLOCK
    hp, dh = q_ref.shape[1], q_ref.shape[3]
    n_chunks, kc = ka_ref.shape[1], ka_ref.shape[2]
    n_blk = n_chunks * MOBA_CHUNK_BLOCKS
    eye = (lax.broadcasted_iota(I32, (dh, dh), 0) == lax.broadcasted_iota(I32, (dh, dh), 1)).astype(BF16)

    @pl.when(j == 0)
    def _():
        eye_ref[...] = (lax.broadcasted_iota(I32, (blk, blk), 0)
                        == lax.broadcasted_iota(I32, (blk, blk), 1)).astype(BF16)
        for hh in range(hp):
            for i in range(n_blk):
                km_ref[hh, i:i + 1, :] = jnp.mean(k_ref[0, hh, i * blk:(i + 1) * blk, :], axis=0, keepdims=True)
            for c in range(n_chunks):
                rows = slice(c * kc, (c + 1) * kc)
                key_blk = lax.broadcasted_iota(I32, (kc, n_blk), 0) // blk + c * MOBA_CHUNK_BLOCKS
                onehot = key_blk == lax.broadcasted_iota(I32, (kc, n_blk), 1)
                ka_ref[hh, c, :, :dh] = k_ref[0, hh, rows, :].astype(BF16)
                ka_ref[hh, c, :, dh:] = jnp.where(onehot, 1.0, 0.0).astype(BF16)
                vt_ref[hh, c] = _dot_nt(eye, v_ref[0, hh, rows, :].astype(BF16)).astype(BF16)

    cj = j // MOBA_CHUNK_BLOCKS
    bidx = lax.broadcasted_iota(I32, (n_blk, blk), 0)
    valid = bidx < j
    key_pos = lax.broadcasted_iota(I32, (kc, blk), 0) + cj * kc
    q_pos = lax.broadcasted_iota(I32, (kc, blk), 1) + j * blk
    qas, ms, ls, accs = [], [], [], []
    for hh in range(hp):
        q = q_ref[0, hh]
        sc = jnp.where(valid, _dot3(km_ref[hh], q, _dot_nt), -jnp.inf)
        rank = jnp.zeros((n_blk, blk), I32)
        for i in range(n_blk - 1):
            row = sc[i:i + 1, :]
            rank = rank + jnp.where(row > sc, 1, jnp.where(row == sc, jnp.where(bidx > i, 1, 0), 0))
        bias_t = jnp.where(valid, jnp.where(rank < MOBA_TOPK, 0.0, NEG_BIG),
                           jnp.where(bidx == j, 0.0, NEG_BIG))
        q_t = _dot_nt(eye, q.astype(BF16))
        qa = jnp.concatenate([q_t, bias_t], axis=0).astype(BF16)
        s = jnp.where(key_pos <= q_pos, _dot(ka_ref[hh, cj], qa), NEG_BIG)
        m = jnp.max(s, axis=0, keepdims=True)
        p = jnp.exp(s - m)
        qas.append(qa)
        ms.append(m)
        ls.append(jnp.sum(p, axis=0, keepdims=True))
        accs.append(_dot(vt_ref[hh, cj], p.astype(BF16)))

    def past_chunk(c, carry):
        m_all, l_all, acc_all = carry
        ms, ls, accs = [], [], []
        for hh in range(hp):
            m, l, acc = m_all[hh:hh + 1], l_all[hh:hh + 1], acc_all[hh * dh:(hh + 1) * dh]
            s = _dot(ka_ref[hh, c], qas[hh])
            m_new = jnp.maximum(m, jnp.max(s, axis=0, keepdims=True))
            p = jnp.exp(s - m_new)
            alpha = jnp.exp(m - m_new)
            ms.append(m_new)
            ls.append(alpha * l + jnp.sum(p, axis=0, keepdims=True))
            accs.append(alpha * acc + _dot(vt_ref[hh, c], p.astype(BF16)))
        return jnp.concatenate(ms, axis=0), jnp.concatenate(ls, axis=0), jnp.concatenate(accs, axis=0)

    _, l_all, acc_all = lax.fori_loop(
        0, cj, past_chunk,
        (jnp.concatenate(ms, axis=0), jnp.concatenate(ls, axis=0), jnp.concatenate(accs, axis=0)))
    outs = [_dot_nt(eye_ref[...], (acc_all[hh * dh:(hh + 1) * dh] / l_all[hh:hh + 1]).astype(BF16))
            for hh in range(hp)]
    o_ref[...] = jnp.concatenate(outs, axis=1).astype(BF16)


def _moba_prompt(q, k, v):
    b, h, s, dh = q.shape
    hp = MOBA_HEADS_PER_STEP
    n_blk = s // MOBA_BLOCK
    assert n_blk % MOBA_CHUNK_BLOCKS == 0 and h % hp == 0 and hp * dh == LANES
    n_chunks = n_blk // MOBA_CHUNK_BLOCKS
    kc = MOBA_CHUNK_BLOCKS * MOBA_BLOCK
    full = pl.BlockSpec((1, hp, s, dh), lambda bi, hi, j: (bi, hi, 0, 0))
    return pl.pallas_call(
        _moba_prompt_kernel,
        grid=(b, h // hp, n_blk),
        in_specs=[pl.BlockSpec((1, hp, MOBA_BLOCK, dh), lambda bi, hi, j: (bi, hi, j, 0)), full, full],
        out_specs=pl.BlockSpec((MOBA_BLOCK, hp * dh), lambda bi, hi, j: (bi * n_blk + j, hi)),
        out_shape=jax.ShapeDtypeStruct((b * s, h * dh), BF16),
        scratch_shapes=[pltpu.VMEM((hp, n_blk, dh), F32),
                        pltpu.VMEM((hp, n_chunks, kc, dh + n_blk), BF16),
                        pltpu.VMEM((hp, n_chunks, dh, kc), BF16),
                        pltpu.VMEM((MOBA_BLOCK, MOBA_BLOCK), BF16)],
        compiler_params=_params("parallel", "parallel", "arbitrary"),
        name="moba_prompt",
    )(q, k, v)


PAGE_MEAN_TILE = 16


def _page_mean_kernel(c_ref, o_ref):
    x = c_ref[0]
    dh, ps = x.shape[-2], x.shape[-1]
    r = jnp.sum(x, axis=-1, keepdims=True) * (1.0 / ps)
    diag = lax.broadcasted_iota(I32, (dh, ps), 0) == lax.broadcasted_iota(I32, (dh, ps), 1)
    o_ref[...] = jnp.sum(jnp.where(diag, r, 0.0), axis=2)


def _page_mean(cache_kt, layer):
    _, n_pool, h, dh, ps = cache_kt.shape
    assert dh <= ps
    return pl.pallas_call(
        _page_mean_kernel,
        grid=(n_pool // PAGE_MEAN_TILE,),
        in_specs=[pl.BlockSpec((1, PAGE_MEAN_TILE, h, dh, ps), lambda i: (layer, i, 0, 0, 0))],
        out_specs=pl.BlockSpec((PAGE_MEAN_TILE, h, ps), lambda i: (i, 0, 0)),
        out_shape=jax.ShapeDtypeStruct((n_pool, h, ps), F32),
        compiler_params=_params("parallel"),
        name="page_mean",
    )(cache_kt)


def _select_kernel(q_ref, pe_ref, po_ref, o_ref):
    n_blk = pe_ref.shape[2]
    t = q_ref.shape[2]
    lane = lax.broadcasted_iota(I32, (t, n_blk), 1).astype(F32)
    for h in range(B_HEADS):
        bm = 0.5 * (pe_ref[0, h] + po_ref[0, h])
        sc = _dot3(q_ref[0, h], bm, _dot_nt)
        res = jnp.zeros((t, n_blk), F32)
        for r in range(MOBA_TOPK):
            m = jnp.max(sc, axis=1, keepdims=True)
            idx = jnp.min(jnp.where(sc == m, lane, float(n_blk)), axis=1, keepdims=True)
            res = jnp.where(lane == r, idx, res)
            sc = jnp.where(lane == idx, -jnp.inf, sc)
        o_ref[0, h] = res.astype(I32)


def _select_blocks(q, pm_even, pm_odd):
    n, h, t, dh = q.shape
    n_blk = pm_even.shape[2]
    qs = pl.BlockSpec((1, h, t, dh), lambda i: (i, 0, 0, 0))
    ps = pl.BlockSpec((1, h, n_blk, dh), lambda i: (i, 0, 0, 0))
    return pl.pallas_call(
        _select_kernel,
        grid=(n,),
        in_specs=[qs, ps, ps],
        out_specs=pl.BlockSpec((1, h, t, n_blk), lambda i: (i, 0, 0, 0)),
        out_shape=jax.ShapeDtypeStruct((n, h, t, n_blk), I32),
        compiler_params=_params("parallel"),
        name="moba_select",
    )(q, pm_even, pm_odd)


def _moba_sample_kernel(sel_ref, pt_ref, q_ref, kn_ref, vn_ref, ck_ref, cv_ref, o_ref, kbuf, vbuf, sem, *,
                        layer, pages_per_seq):
    n_heads = pl.num_programs(1)
    step = pl.program_id(0) * n_heads + pl.program_id(1)
    n_steps = pl.num_programs(0) * n_heads
    t = q_ref.shape[2]
    n_pages = kbuf.shape[1]
    pages_per_q = n_pages // t
    slot = step % 2

    def page_copies(st, sl, i, pg):
        hh = st % n_heads
        return (pltpu.make_async_copy(ck_ref.at[layer, pg, hh], kbuf.at[sl, i], sem.at[0, sl]),
                pltpu.make_async_copy(cv_ref.at[layer, pg, hh], vbuf.at[sl, i], sem.at[1, sl]))

    def fetch(st, sl):
        pt_base = (st // n_heads) * pages_per_seq
        for b in range(n_pages // PAGES_PER_BLOCK):
            blk = sel_ref[st * (n_pages // PAGES_PER_BLOCK) + b]
            for pi in range(PAGES_PER_BLOCK):
                pg = pt_ref[pt_base + blk * PAGES_PER_BLOCK + pi]
                for cp in page_copies(st, sl, b * PAGES_PER_BLOCK + pi, pg):
                    cp.start()

    @pl.when(step == 0)
    def _():
        fetch(step, slot)

    @pl.when(step + 1 < n_steps)
    def _():
        fetch(step + 1, 1 - slot)

    for i in range(n_pages):
        for cp in page_copies(step, slot, i, 0):
            cp.wait()

    q = q_ref[0, 0]
    qb = q.astype(BF16)
    s = jnp.concatenate([_dot(qb, kbuf[slot, i].astype(BF16)) for i in range(n_pages)], axis=1)
    per_q = pages_per_q * PAGE_SIZE
    col = lax.broadcasted_iota(I32, s.shape, 1)
    lo = lax.broadcasted_iota(I32, s.shape, 0) * per_q
    own = (col >= lo) & (col < lo + per_q)
    sn = _dot_nt(q, kn_ref[0, 0])
    causal = lax.broadcasted_iota(I32, (t, t), 1) <= lax.broadcasted_iota(I32, (t, t), 0)
    m = jnp.maximum(jnp.max(jnp.where(own, s, NEG_BIG), axis=1, keepdims=True),
                    jnp.max(jnp.where(causal, sn, NEG_BIG), axis=1, keepdims=True))
    p = jnp.where(own, jnp.exp(s - m), 0.0).astype(BF16)
    pn = jnp.where(causal, jnp.exp(sn - m), 0.0)
    l = jnp.sum(p.astype(F32), axis=1, keepdims=True) + jnp.sum(pn, axis=1, keepdims=True)
    o = _dot(pn, vn_ref[0, 0])
    for i in range(n_pages):
        o = o + _dot_nt(p[:, i * PAGE_SIZE:(i + 1) * PAGE_SIZE], vbuf[slot, i].astype(BF16))
    o_ref[0, 0] = o / l


def _moba_sample(q, k_new, v_new, cache_kt, cache_vt, sel, page_table, layer):
    n, h, t, dh = q.shape
    n_pages = t * MOBA_TOPK * PAGES_PER_BLOCK
    blk = pl.BlockSpec((1, 1, t, dh), lambda i, j, sl, pt: (i, j, 0, 0))
    anyspec = pl.BlockSpec(memory_space=pl.ANY)
    return pl.pallas_call(
        functools.partial(_moba_sample_kernel, layer=layer, pages_per_seq=page_table.shape[1]),
        grid_spec=pltpu.PrefetchScalarGridSpec(
            num_scalar_prefetch=2,
            grid=(n, h),
            in_specs=[blk, blk, blk, anyspec, anyspec],
            out_specs=blk,
            scratch_shapes=[pltpu.VMEM((2, n_pages, dh, PAGE_SIZE), F32),
                            pltpu.VMEM((2, n_pages, dh, PAGE_SIZE), F32),
                            pltpu.SemaphoreType.DMA((2, 2))]),
        out_shape=jax.ShapeDtypeStruct((n, h, t, dh), F32),
        compiler_params=_params("arbitrary", "arbitrary"),
        name="moba_sample",
    )(sel.reshape(-1), page_table.reshape(-1), q, k_new, v_new, cache_kt, cache_vt)


def _merge_kernel(xp_ref, xs_ref, oap_ref, oas_ref, obp_ref, obs_ref, ga_ref, gb_ref, g1_ref, sh_ref, sc_ref,
                  nf_ref, wpa_ref, wpb_ref, wo_ref, wr_ref, br_ref, x1_ref, h2_ref, eid_ref, gw_ref, *, n_first):
    pa = _dot(_pick_rows(n_first, oap_ref, oas_ref).astype(BF16), wpa_ref[...])
    pb = _dot(_pick_rows(n_first, obp_ref, obs_ref), wpb_ref[...])
    mix = jax.nn.sigmoid(ga_ref[...]) * pa + jax.nn.sigmoid(gb_ref[...]) * pb
    x1 = _pick_rows(n_first, xp_ref, xs_ref) + g1_ref[0] * _dot(mix.astype(BF16), wo_ref[...])
    x1_ref[...] = x1
    h2 = x1 * lax.rsqrt(jnp.mean(x1 * x1, axis=-1, keepdims=True) + NORM_EPS) * nf_ref[...]
    h2 = h2 * (1.0 + sc_ref[0]) + sh_ref[0]
    h2_ref[...] = h2

    logits = _dot3(h2, wr_ref[...]) + br_ref[...]
    lane = lax.broadcasted_iota(I32, logits.shape, 1)
    lanef = lane.astype(F32)
    is_g = lane < N_GROUPS
    gl = jnp.where(is_g, logits, -jnp.inf)
    gmax = jnp.max(gl, axis=1, keepdims=True)
    g_sel = jnp.min(jnp.where(gl == gmax, lanef, float(LANES)), axis=1, keepdims=True)
    p_group = 1.0 / jnp.sum(jnp.exp(gl - gmax), axis=1, keepdims=True)
    e_lo = N_GROUPS + g_sel * EXPERTS_PER_GROUP
    in_grp = (lanef >= e_lo) & (lanef < e_lo + EXPERTS_PER_GROUP)
    el = jnp.where(in_grp, logits, -jnp.inf)
    m1 = jnp.max(el, axis=1, keepdims=True)
    i1 = jnp.min(jnp.where(el == m1, lanef, float(LANES)), axis=1, keepdims=True)
    el2 = jnp.where(lanef == i1, -jnp.inf, el)
    m2 = jnp.max(el2, axis=1, keepdims=True)
    i2 = jnp.min(jnp.where(el2 == m2, lanef, float(LANES)), axis=1, keepdims=True)
    e2 = jnp.exp(m2 - m1)
    w1 = p_group / (1.0 + e2)
    w2 = p_group * e2 / (1.0 + e2)
    eid = jnp.where(lane == 0, i1 - N_GROUPS, jnp.where(lane == 1, i2 - N_GROUPS, 0.0))
    eid_ref[...] = eid.astype(I32)
    gw_ref[...] = jnp.where(lane == 0, w1, jnp.where(lane == 1, w2, 0.0))


def _merge(xp, xs, oa_p, oa_s, ob_p, ob_s, zg, mod_tiles, norm_ffn, wpa, wpb, wo, w_router, b_router,
           tiles_per_seq):
    d = xp.shape[1]
    n_first = xp.shape[0] // ROW_TILE
    m = xp.shape[0] + xs.shape[0]
    n_mod = mod_tiles.shape[1]
    mod_map = lambda i: (jnp.minimum(i // tiles_per_seq, n_mod - 1), 0, 0)
    const = lambda i: (0, 0)
    row = lambda i: (i, 0)
    mod = pl.BlockSpec((1, ROW_TILE, d), mod_map)
    wspec = lambda w: pl.BlockSpec(w.shape, const, pipeline_mode=pl.Buffered(1))
    return pl.pallas_call(
        functools.partial(_merge_kernel, n_first=n_first),
        grid=(m // ROW_TILE,),
        in_specs=_two_group_specs(d, n_first) + _two_group_specs(A_WIDTH, n_first)
                 + _two_group_specs(B_WIDTH, n_first) + [
                  pl.BlockSpec((ROW_TILE, d), lambda i: (i, 0)),
                  pl.BlockSpec((ROW_TILE, d), lambda i: (i, 1)),
                  mod, mod, mod,
                  pl.BlockSpec((1, d), const),
                  wspec(wpa), wspec(wpb), wspec(wo), wspec(w_router),
                  pl.BlockSpec((1, LANES), const)],
        out_specs=[pl.BlockSpec((ROW_TILE, d), row),
                   pl.BlockSpec((ROW_TILE, d), row),
                   pl.BlockSpec((ROW_TILE, LANES), row),
                   pl.BlockSpec((ROW_TILE, LANES), row)],
        out_shape=[jax.ShapeDtypeStruct((m, d), F32),
                   jax.ShapeDtypeStruct((m, d), F32),
                   jax.ShapeDtypeStruct((m, LANES), I32),
                   jax.ShapeDtypeStruct((m, LANES), F32)],
        compiler_params=_params("parallel"),
        name="merge_router",
    )(xp, xs, oa_p, oa_s, ob_p, ob_s, zg, zg, mod_tiles[2], mod_tiles[3], mod_tiles[4],
      norm_ffn.reshape(1, d), wpa, wpb, wo, w_router, b_router)


def _moe_kernel(be_ref, nu_ref, tok_ref, h_ref, wg_ref, wu_ref, wd_ref, y_ref, xbuf, sem, wg_s, wu_s, wd_s):
    i = pl.program_id(0)
    slot = i % 2

    def row_copy(blk, sl, r):
        tok = tok_ref[blk * MOE_TILE + r]
        return pltpu.make_async_copy(h_ref.at[pl.ds(tok, 1), :], xbuf.at[sl, pl.ds(r, 1), :], sem.at[sl])

    def fetch(blk, sl):
        for r in range(MOE_TILE):
            row_copy(blk, sl, r).start()

    @pl.when(i == 0)
    def _():
        fetch(i, slot)

    @pl.when(i + 1 < pl.num_programs(0))
    def _():
        fetch(i + 1, 1 - slot)

    prev = be_ref[jnp.maximum(i - 1, 0)]
    fresh = (i == 0) | (be_ref[i] != prev)

    @pl.when(fresh)
    def _():
        wg_s[...] = wg_ref[...].astype(BF16)
        wu_s[...] = wu_ref[...].astype(BF16)
        wd_s[...] = wd_ref[...].astype(BF16)

    for r in range(MOE_TILE):
        row_copy(i, slot, r).wait()

    @pl.when(i < nu_ref[0])
    def _():
        xb = xbuf[slot].astype(BF16)
        hid = _silu(_dot(xb, wg_s[...])) * _dot(xb, wu_s[...])
        y_ref[...] = _dot(hid.astype(BF16), wd_s[...])

    @pl.when(i >= nu_ref[0])
    def _():
        y_ref[...] = jnp.zeros(y_ref.shape, F32)


def _moe(h, row_tok, block_expert, n_used, w_gate, w_up, w_down, layer):
    d = h.shape[1]
    n_rows = row_tok.shape[0]
    de = w_gate.shape[-1]
    return pl.pallas_call(
        _moe_kernel,
        grid_spec=pltpu.PrefetchScalarGridSpec(
            num_scalar_prefetch=3,
            grid=(n_rows // MOE_TILE,),
            in_specs=[pl.BlockSpec(memory_space=pl.ANY),
                      pl.BlockSpec((None, None, d, de), lambda i, be, nu, tk: (layer, be[i], 0, 0)),
                      pl.BlockSpec((None, None, d, de), lambda i, be, nu, tk: (layer, be[i], 0, 0)),
                      pl.BlockSpec((None, None, de, d), lambda i, be, nu, tk: (layer, be[i], 0, 0))],
            out_specs=pl.BlockSpec((MOE_TILE, d), lambda i, be, nu, tk: (i, 0)),
            scratch_shapes=[pltpu.VMEM((2, MOE_TILE, d), F32),
                            pltpu.SemaphoreType.DMA((2,)),
                            pltpu.VMEM((d, de), BF16), pltpu.VMEM((d, de), BF16), pltpu.VMEM((de, d), BF16)]),
        out_shape=jax.ShapeDtypeStruct((n_rows, d), F32),
        compiler_params=_params("arbitrary"),
        name="moe_experts",
    )(block_expert, n_used, row_tok, h, w_gate, w_up, w_down)


def _final_kernel(x1_ref, y0_ref, y1_ref, gw_ref, g2_ref, nf_ref, op_ref, os_ref, *, last, n_first):
    gw = gw_ref[...]
    f = gw[:, 0:1] * y0_ref[...] + gw[:, 1:2] * y1_ref[...]
    x2 = x1_ref[...] + g2_ref[0] * f
    if last:
        x2 = x2 * lax.rsqrt(jnp.mean(x2 * x2, axis=-1, keepdims=True) + NORM_EPS) * nf_ref[...]
    i = pl.program_id(0)

    @pl.when(i < n_first)
    def _():
        op_ref[...] = x2

    @pl.when(i >= n_first)
    def _():
        os_ref[...] = x2


def _final(x1, y0, y1, gw, mod_tiles, norm_final, tiles_per_seq, last, n_first):
    m, d = x1.shape
    n_mod = mod_tiles.shape[1]
    mod_map = lambda i: (jnp.minimum(i // tiles_per_seq, n_mod - 1), 0, 0)
    row = lambda i: (i, 0)
    rs = pl.BlockSpec((ROW_TILE, d), row)
    return pl.pallas_call(
        functools.partial(_final_kernel, last=last, n_first=n_first),
        grid=(m // ROW_TILE,),
        in_specs=[rs, rs, rs, pl.BlockSpec((ROW_TILE, LANES), row),
                  pl.BlockSpec((1, ROW_TILE, d), mod_map),
                  pl.BlockSpec((1, d), lambda i: (0, 0))],
        out_specs=_two_group_specs(d, n_first),
        out_shape=[jax.ShapeDtypeStruct((n_first * ROW_TILE, d), F32),
                   jax.ShapeDtypeStruct((m - n_first * ROW_TILE, d), F32)],
        compiler_params=_params("arbitrary"),
        name="combine_final",
    )(x1, y0, y1, gw, mod_tiles[5], norm_final.reshape(1, d))


def _dispatch(eid):
    m = eid.shape[0]
    n_assign = m * EXPERT_TOPK
    flat_e = eid.reshape(-1)
    flat_tok = jnp.repeat(jnp.arange(m, dtype=I32), EXPERT_TOPK)
    onehot = (flat_e[:, None] == jnp.arange(N_EXPERTS, dtype=I32)[None, :]).astype(I32)
    csum = jnp.cumsum(onehot, axis=0)
    rank = jnp.take_along_axis(csum, flat_e[:, None], axis=1)[:, 0] - 1
    counts = csum[-1]
    padded = (counts + MOE_TILE - 1) // MOE_TILE * MOE_TILE
    pad_end = jnp.cumsum(padded)
    pad_start = pad_end - padded
    dest = (pad_start[flat_e] + rank).astype(I32)
    n_blocks = -(-(n_assign + N_EXPERTS * (MOE_TILE - 1)) // MOE_TILE)
    row_tok = jnp.zeros((n_blocks * MOE_TILE,), I32).at[dest].set(flat_tok)
    blk_start = jnp.arange(n_blocks, dtype=I32) * MOE_TILE
    block_expert = jnp.minimum(jnp.sum((pad_end[None, :] <= blk_start[:, None]).astype(I32), axis=1),
                               N_EXPERTS - 1).astype(I32)
    n_used = (pad_end[-1:] // MOE_TILE).astype(I32)
    return dest.reshape(m, EXPERT_TOPK), row_tok, block_expert, n_used


def _mod_tiles(mod, n_prompt, reps):
    d = mod.shape[-1]
    mp = jnp.broadcast_to(mod[:n_prompt, :, None, :], (n_prompt, 6, ROW_TILE, d))
    ms = jnp.repeat(mod[n_prompt:], reps, axis=0).transpose(1, 0, 2)[None]
    return jnp.concatenate([mp, ms], axis=0).transpose(1, 0, 2, 3)


def kernel(x_prompt, x_sample, c_prompt, c_sample, cache_k, cache_v, state_hgrn, page_table,
           w_ada, b_ada, norm_mix, norm_ffn, w_in, hgrn_lb_logits, hgrn_norm,
           w_proj_a, w_proj_b, w_out, w_group, b_group, w_expert_router, b_expert_router,
           w_gate, w_up, w_down, norm_final):
    bp, tp, d = x_prompt.shape
    bs, ts, _ = x_sample.shape
    mp_rows, ms_rows = bp * tp, bs * ts
    assert ms_rows == ROW_TILE and tp % ROW_TILE == 0
    m = mp_rows + ms_rows
    tiles_per_seq = tp // ROW_TILE
    n_full = PAST_LEN // MOBA_BLOCK
    assert PAST_LEN % MOBA_BLOCK == 0 and n_full >= MOBA_TOPK

    xp, xs = x_prompt.reshape(mp_rows, d), x_sample.reshape(ms_rows, d)
    c_all = jnp.concatenate([c_prompt, c_sample, jnp.zeros((-(bp + bs) % 8, d), F32)], axis=0)
    lb_all = jnp.cumsum(jax.nn.softmax(hgrn_lb_logits.astype(F32), axis=0), axis=0)
    cache_kt = cache_k.transpose(0, 1, 2, 4, 3)
    cache_vt = cache_v.transpose(0, 1, 2, 4, 3)
    kp_l, vp_l, sp_l, ks_l, vs_l, ss_l = [], [], [], [], [], []

    for l in range(DEPTH):
        mod = _modulation(c_all, w_ada[l], b_ada[l])[:bp + bs].reshape(bp + bs, 6, d)
        mt = _mod_tiles(mod, bp, ts)
        za, zb, zg = _inproj(xp, xs, mt, norm_mix[l], w_in[l], tiles_per_seq)

        oa_p, st_p = _hgrn(za, lb_all[l], hgrn_norm[l], 0, bp, tp, ROW_TILE, HGRN_CHUNK)
        oa_s, st_s = _hgrn(za, lb_all[l], hgrn_norm[l], mp_rows, bs, ts, ts, ts, s0=state_hgrn[l])

        q_p, k_p, v_p = _rope_split(zb, 0, bp, tp, 512, jnp.arange(tp))
        q_s, k_s, v_s = _rope_split(zb, mp_rows, bs, ts, ts, PAST_LEN + jnp.arange(ts))
        ob_p = _moba_prompt(q_p, k_p, v_p)
        pm = _page_mean(cache_kt, l)
        pm_g = pm[page_table[:, :n_full * PAGES_PER_BLOCK]][..., :B_HEAD_DIM].transpose(0, 2, 1, 3)
        sel = _select_blocks(q_s, pm_g[:, :, 0::2], pm_g[:, :, 1::2])[..., :MOBA_TOPK]
        ob_s = _moba_sample(q_s, k_s, v_s, cache_kt, cache_vt, sel, page_table, l)
        ob_s = ob_s.transpose(0, 2, 1, 3).reshape(ms_rows, B_WIDTH).astype(BF16)

        w_router = jnp.concatenate(
            [w_group[l], w_expert_router[l].transpose(1, 0, 2).reshape(d, N_EXPERTS),
             jnp.zeros((d, LANES - N_GROUPS - N_EXPERTS), F32)], axis=1)
        b_router = jnp.concatenate(
            [b_group[l], b_expert_router[l].reshape(-1),
             jnp.zeros((LANES - N_GROUPS - N_EXPERTS,), F32)]).reshape(1, LANES)
        x1, h2, eid, gw = _merge(xp, xs, oa_p, oa_s, ob_p, ob_s, zg, mt, norm_ffn[l], w_proj_a[l].astype(BF16),
                                 w_proj_b[l].astype(BF16), w_out[l].astype(BF16), w_router, b_router,
                                 tiles_per_seq)

        dest, row_tok, block_expert, n_used = _dispatch(eid[:, :EXPERT_TOPK])
        ys = _moe(h2, row_tok, block_expert, n_used, w_gate, w_up, w_down, l)
        xp, xs = _final(x1, ys[dest[:, 0]], ys[dest[:, 1]], gw, mt, norm_final, tiles_per_seq,
                        l == DEPTH - 1, mp_rows // ROW_TILE)

        kp_l.append(k_p); vp_l.append(v_p); sp_l.append(st_p)
        ks_l.append(k_s); vs_l.append(v_s); ss_l.append(st_s)

    y_prompt = xp.reshape(bp, tp, d)
    y_sample = xs.reshape(bs, ts, d)
    return (y_prompt, y_sample, jnp.stack(kp_l), jnp.stack(vp_l), jnp.stack(sp_l),
            jnp.stack(ks_l), jnp.stack(vs_l), jnp.stack(ss_l))
```

```python
import functools

import numpy as np
import jax
import jax.numpy as jnp
from jax import lax
from jax.experimental import pallas as pl
from jax.experimental.pallas import tpu as pltpu

F32 = jnp.float32
BF16 = jnp.bfloat16
I32 = jnp.int32

D_MODEL = 1024
DEPTH = 1
PAST_LEN = 16384
PAGE_SIZE = 128
A_HEADS = 4
A_KDIM = 128
A_VDIM = 128
A_WIDTH = A_HEADS * A_KDIM
B_HEADS = 8
B_HEAD_DIM = 64
B_WIDTH = B_HEADS * B_HEAD_DIM
MOBA_BLOCK = 256
MOBA_TOPK = 3
ROT_DIM = B_HEAD_DIM // 4
ROPE_THETA = 500000.0
N_GROUPS = 4
EXPERTS_PER_GROUP = 8
N_EXPERTS = N_GROUPS * EXPERTS_PER_GROUP
EXPERT_TOPK = 2
D_EXPERT = D_MODEL // 2
NORM_EPS = 1e-6

PAGES_PER_BLOCK = MOBA_BLOCK // PAGE_SIZE
ROW_TILE = 256
HGRN_CHUNK = 16
MOE_TILE = 256
LANES = 128
NEG_BIG = -1e30
LOG2_E = 1.4426950408889634
VMEM_LIMIT_BYTES = 52 * 1024 * 1024


def _params(*sem):
    return pltpu.CompilerParams(dimension_semantics=sem, vmem_limit_bytes=VMEM_LIMIT_BYTES)


def _dot(a, b):
    return jnp.dot(a, b, preferred_element_type=F32)


def _dot_nt(a, b):
    return lax.dot_general(a, b, (((1,), (1,)), ((), ())), preferred_element_type=F32)


def _dot_tn(a, b):
    return lax.dot_general(a, b, (((0,), (0,)), ((), ())), preferred_element_type=F32)


def _split(a):
    hi = a.astype(BF16)
    return hi, (a - hi.astype(F32)).astype(BF16)


def _dot3(a, b, dot=_dot):
    ah, al = _split(a)
    bh, bl = _split(b)
    return dot(ah, bh) + (dot(ah, bl) + dot(al, bh))


def _silu(x):
    return x * jax.nn.sigmoid(x)


def _mod_kernel(c_ref, w_ref, b_ref, o_ref):
    o_ref[...] = _dot3(_silu(c_ref[...]), w_ref[...]) + b_ref[...]


def _modulation(c_all, w, b):
    n = c_all.shape[0]
    d, dout = w.shape
    return pl.pallas_call(
        _mod_kernel,
        grid=(dout // d,),
        in_specs=[pl.BlockSpec((n, d), lambda j: (0, 0)),
                  pl.BlockSpec((d, d), lambda j: (0, j)),
                  pl.BlockSpec((1, d), lambda j: (0, j))],
        out_specs=pl.BlockSpec((n, d), lambda j: (0, j)),
        out_shape=jax.ShapeDtypeStruct((n, dout), F32),
        compiler_params=_params("parallel"),
        name="modulation",
    )(c_all, w, b.reshape(1, dout))


def _pick_rows(n_first, first_ref, second_ref):
    return jnp.where(pl.program_id(0) < n_first, first_ref[...], second_ref[...])


def _two_group_specs(width, n_first):
    return [pl.BlockSpec((ROW_TILE, width), lambda i: (jnp.minimum(i, n_first - 1), 0)),
            pl.BlockSpec((ROW_TILE, width), lambda i: (0, 0))]


def _inproj_kernel(xp_ref, xs_ref, sh_ref, sc_ref, g_ref, w_ref, wlo_ref, za_ref, zb_ref, zg_ref, *, n_first):
    x = _pick_rows(n_first, xp_ref, xs_ref)
    h = x * lax.rsqrt(jnp.mean(x * x, axis=-1, keepdims=True) + NORM_EPS) * g_ref[...]
    h = h * (1.0 + sc_ref[0]) + sh_ref[0]
    hh, hl = _split(h)
    wa = 4 * A_WIDTH
    for c in range(0, wa, 512):
        za_ref[:, c:c + 512] = _dot(hh, w_ref[:, c:c + 512])
    for c in range(0, 2 * B_WIDTH, 512):
        wc = w_ref[:, wa + c:wa + c + 512]
        zb_ref[:, c:c + 512] = _dot(hh, wc) + (_dot(hh, wlo_ref[:, c:c + 512]) + _dot(hl, wc))
    c = 2 * B_WIDTH
    zb_ref[:, c:c + 512] = _dot(hh, w_ref[:, wa + c:wa + c + 512])
    wg = wa + 3 * B_WIDTH
    for c in range(0, 2 * D_MODEL, 512):
        zg_ref[:, c:c + 512] = _dot(hh, w_ref[:, wg + c:wg + c + 512])


def _inproj(xp, xs, mod_tiles, norm_g, w_in, tiles_per_seq):
    d = xp.shape[1]
    n_first = xp.shape[0] // ROW_TILE
    m = xp.shape[0] + xs.shape[0]
    n_mod = mod_tiles.shape[1]
    w_hi = w_in.astype(BF16)
    wa = 4 * A_WIDTH
    wqk = w_in[:, wa:wa + 2 * B_WIDTH]
    w_lo = (wqk - wqk.astype(BF16).astype(F32)).astype(BF16)
    mod_map = lambda i: (jnp.minimum(i // tiles_per_seq, n_mod - 1), 0, 0)
    const = lambda i: (0, 0)
    row = lambda i: (i, 0)
    return pl.pallas_call(
        functools.partial(_inproj_kernel, n_first=n_first),
        grid=(m // ROW_TILE,),
        in_specs=_two_group_specs(d, n_first) + [
                  pl.BlockSpec((1, ROW_TILE, d), mod_map),
                  pl.BlockSpec((1, ROW_TILE, d), mod_map),
                  pl.BlockSpec((1, d), const),
                  pl.BlockSpec(w_hi.shape, const, pipeline_mode=pl.Buffered(1)),
                  pl.BlockSpec(w_lo.shape, const, pipeline_mode=pl.Buffered(1))],
        out_specs=[pl.BlockSpec((ROW_TILE, wa), row),
                   pl.BlockSpec((ROW_TILE, 3 * B_WIDTH), row),
                   pl.BlockSpec((ROW_TILE, 2 * D_MODEL), row)],
        out_shape=[jax.ShapeDtypeStruct((m, wa), F32),
                   jax.ShapeDtypeStruct((m, 3 * B_WIDTH), F32),
                   jax.ShapeDtypeStruct((m, 2 * D_MODEL), F32)],
        compiler_params=_params("parallel"),
        name="inproj",
    )(xp, xs, mod_tiles[0], mod_tiles[1], norm_g.reshape(1, d), w_hi, w_lo)


def _rope_kernel(q_ref, k_ref, v_ref, c_ref, s1_ref, s2_ref, qo_ref, ko_ref, vo_ref):
    cos, s1, s2 = c_ref[...], s1_ref[...], s2_ref[...]
    half = ROT_DIM // 2

    def rope(x):
        up = pltpu.roll(x, B_WIDTH - half, axis=1)
        dn = pltpu.roll(x, half, axis=1)
        return x * cos + up * s1 + dn * s2

    q = rope(q_ref[...]) * (B_HEAD_DIM ** -0.5)
    k = rope(k_ref[...])
    v = v_ref[...]
    for h in range(B_HEADS):
        ls = slice(h * B_HEAD_DIM, (h + 1) * B_HEAD_DIM)
        qo_ref[0, h] = q[:, ls]
        ko_ref[0, h] = k[:, ls]
        vo_ref[0, h] = v[:, ls]


def _rope_tables(pos):
    half = ROT_DIM // 2
    inv = jnp.power(ROPE_THETA, -(jnp.arange(half, dtype=F32) * 2.0 / ROT_DIM))
    ang = pos.astype(F32)[:, None] * inv[None, :]
    cos, sin = jnp.cos(ang), jnp.sin(ang)
    t = pos.shape[0]
    rest = B_HEAD_DIM - ROT_DIM
    c = jnp.concatenate([cos, cos, jnp.ones((t, rest), F32)], axis=-1)
    s1 = jnp.concatenate([-sin, jnp.zeros((t, half + rest), F32)], axis=-1)
    s2 = jnp.concatenate([jnp.zeros((t, half), F32), sin, jnp.zeros((t, rest), F32)], axis=-1)
    return [jnp.tile(a, (1, B_HEADS)) for a in (c, s1, s2)]


def _rope_split(zb, row0, n, t, tile, pos):
    tabs = _rope_tables(pos)
    tps = t // tile
    blk0 = row0 // tile
    zmap = lambda c: (lambda b, s: (blk0 + b * tps + s, c))
    tmap = lambda b, s: (s, 0)
    omap = lambda b, s: (b, 0, s, 0)
    oshape = jax.ShapeDtypeStruct((n, B_HEADS, t, B_HEAD_DIM), F32)
    ospec = pl.BlockSpec((1, B_HEADS, tile, B_HEAD_DIM), omap)
    return pl.pallas_call(
        _rope_kernel,
        grid=(n, tps),
        in_specs=[pl.BlockSpec((tile, B_WIDTH), zmap(0)),
                  pl.BlockSpec((tile, B_WIDTH), zmap(1)),
                  pl.BlockSpec((tile, B_WIDTH), zmap(2)),
                  pl.BlockSpec((tile, B_WIDTH), tmap),
                  pl.BlockSpec((tile, B_WIDTH), tmap),
                  pl.BlockSpec((tile, B_WIDTH), tmap)],
        out_specs=[ospec, ospec, ospec],
        out_shape=[oshape, oshape, oshape],
        compiler_params=_params("parallel", "parallel"),
        name="rope_split",
    )(zb, zb, zb, *tabs)


def _hgrn_kernel(*refs, chunk, has_s0):
    aq_ref, af_ref, ai_ref, ag_ref, lb_ref, gain_ref = refs[:6]
    rest = refs[6:]
    s0_ref = None
    if has_s0:
        s0_ref, rest = rest[0], rest[1:]
    o_ref, so_ref, st_ref, q_s, b_s, k_s = rest
    t = pl.program_id(1)
    tb = aq_ref.shape[0]

    @pl.when(t == 0)
    def _():
        for h in range(A_HEADS):
            if has_s0:
                st_ref[h] = s0_ref[0, h].T
            else:
                st_ref[h] = jnp.zeros((A_VDIM, A_KDIM), F32)

    lb = lb_ref[...]
    f = lb + (1.0 - lb) * jax.nn.sigmoid(af_ref[...])
    logf = jnp.log(f)
    q_s[...] = _silu(aq_ref[...])
    k_s[...] = 1.0 - f
    row = lax.broadcasted_iota(I32, logf.shape, 0) & (chunk - 1)
    b = logf
    sh = 1
    while sh < chunk:
        b = b + jnp.where(row >= sh, pltpu.roll(b, sh, axis=0), 0.0)
        sh *= 2
    b_s[...] = b
    rowc = lax.broadcasted_iota(I32, (chunk, A_KDIM), 0)

    def one_chunk(ci, carry):
        r0 = pl.multiple_of(ci * chunk, chunk)
        rs = pl.ds(r0, chunk)
        for h in range(A_HEADS):
            ls = slice(h * A_KDIM, (h + 1) * A_KDIM)
            qc, bc, kc, vc = q_s[rs, ls], b_s[rs, ls], k_s[rs, ls], ai_ref[rs, ls]
            st = st_ref[h]
            bl = bc[chunk - 1:chunk, :]
            o = _dot_nt((qc * jnp.exp(bc)).astype(BF16), st.astype(BF16))
            for s in range(chunk):
                e = jnp.exp(jnp.where(rowc >= s, bc - bc[s:s + 1, :], -jnp.inf))
                r = jnp.sum(qc * e * kc[s:s + 1, :], axis=1, keepdims=True)
                o = o + r * vc[s:s + 1, :]
            kp = kc * jnp.exp(bl - bc)
            st_ref[h] = st * jnp.exp(bl) + _dot_tn(vc.astype(BF16), kp.astype(BF16))
            o_ref[rs, ls] = o
        return carry

    lax.fori_loop(0, tb // chunk, one_chunk, 0)

    gain = gain_ref[...]
    for h in range(A_HEADS):
        ls = slice(h * A_VDIM, (h + 1) * A_VDIM)
        oh = o_ref[:, ls]
        y = oh * lax.rsqrt(jnp.mean(oh * oh, axis=-1, keepdims=True) + NORM_EPS) * gain
        o_ref[:, ls] = y * _silu(ag_ref[:, ls])

    @pl.when(t == pl.num_programs(1) - 1)
    def _():
        for h in range(A_HEADS):
            so_ref[0, h] = st_ref[h].T


def _hgrn(za, lb, gain, row0, n, t, tile, chunk, s0=None):
    tps = t // tile
    blk0 = row0 // tile
    zmap = lambda c: (lambda b, s: (blk0 + b * tps + s, c))
    const = lambda b, s: (0, 0)
    in_specs = [pl.BlockSpec((tile, A_WIDTH), zmap(c)) for c in range(4)]
    in_specs += [pl.BlockSpec((1, A_WIDTH), const), pl.BlockSpec((1, A_VDIM), const)]
    args = [za, za, za, za, lb.reshape(1, A_WIDTH), gain.reshape(1, A_VDIM)]
    if s0 is not None:
        in_specs.append(pl.BlockSpec((1, A_HEADS, A_KDIM, A_VDIM), lambda b, s: (b, 0, 0, 0)))
        args.append(s0)
    return pl.pallas_call(
        functools.partial(_hgrn_kernel, chunk=chunk, has_s0=s0 is not None),
        grid=(n, tps),
        in_specs=in_specs,
        out_specs=[pl.BlockSpec((tile, A_WIDTH), lambda b, s: (b * tps + s, 0)),
                   pl.BlockSpec((1, A_HEADS, A_KDIM, A_VDIM), lambda b, s: (b, 0, 0, 0))],
        out_shape=[jax.ShapeDtypeStruct((n * t, A_WIDTH), F32),
                   jax.ShapeDtypeStruct((n, A_HEADS, A_KDIM, A_VDIM), F32)],
        scratch_shapes=[pltpu.VMEM((A_HEADS, A_VDIM, A_KDIM), F32),
                        pltpu.VMEM((tile, A_WIDTH), F32),
                        pltpu.VMEM((tile, A_WIDTH), F32),
                        pltpu.VMEM((tile, A_WIDTH), F32)],
        compiler_params=_params("parallel", "arbitrary"),
        name="hgrn2",
    )(*args)


MOBA_CHUNK_BLOCKS = 4
MOBA_HEADS_PER_STEP = 2


def _moba_prompt_kernel(q_ref, k_ref, v_ref, o_ref, km_ref, ka_ref, vt_ref, eye_ref):
    j = pl.program_id(2)
    blk = MOBA_BLOCK
    hp, dh = q_ref.shape[1], q_ref.shape[3]
    n_chunks, kc = ka_ref.shape[1], ka_ref.shape[2]
    n_blk = n_chunks * MOBA_CHUNK_BLOCKS
    eye = (lax.broadcasted_iota(I32, (dh, dh), 0) == lax.broadcasted_iota(I32, (dh, dh), 1)).astype(BF16)

    @pl.when(j == 0)
    def _():
        eye_ref[...] = (lax.broadcasted_iota(I32, (blk, blk), 0)
                        == lax.broadcasted_iota(I32, (blk, blk), 1)).astype(BF16)
        for hh in range(hp):
            for i in range(n_blk):
                km_ref[hh, i:i + 1, :] = jnp.mean(k_ref[0, hh, i * blk:(i + 1) * blk, :], axis=0, keepdims=True)
            for c in range(n_chunks):
                rows = slice(c * kc, (c + 1) * kc)
                key_blk = lax.broadcasted_iota(I32, (kc, n_blk), 0) // blk + c * MOBA_CHUNK_BLOCKS
                onehot = key_blk == lax.broadcasted_iota(I32, (kc, n_blk), 1)
                ka_ref[hh, c, :, :dh] = k_ref[0, hh, rows, :].astype(BF16)
                ka_ref[hh, c, :, dh:] = jnp.where(onehot, 1.0, 0.0).astype(BF16)
                vt_ref[hh, c] = _dot_nt(eye, v_ref[0, hh, rows, :].astype(BF16)).astype(BF16)

    cj = j // MOBA_CHUNK_BLOCKS
    bidx = lax.broadcasted_iota(I32, (n_blk, blk), 0)
    valid = bidx < j
    key_pos = lax.broadcasted_iota(I32, (kc, blk), 0) + cj * kc
    q_pos = lax.broadcasted_iota(I32, (kc, blk), 1) + j * blk
    qas, ms, ls, accs = [], [], [], []
    for hh in range(hp):
        q = q_ref[0, hh]
        sc = jnp.where(valid, _dot3(km_ref[hh], q, _dot_nt), -jnp.inf)
        rank = jnp.zeros((n_blk, blk), I32)
        for i in range(n_blk - 1):
            row = sc[i:i + 1, :]
            rank = rank + jnp.where(row > sc, 1, jnp.where(row == sc, jnp.where(bidx > i, 1, 0), 0))
        bias_t = jnp.where(valid, jnp.where(rank < MOBA_TOPK, 0.0, NEG_BIG),
                           jnp.where(bidx == j, 0.0, NEG_BIG))
        q_t = _dot_nt(eye, (q * LOG2_E).astype(BF16))
        qa = jnp.concatenate([q_t, bias_t], axis=0).astype(BF16)
        s = jnp.concatenate([_dot(ka_ref[hh, cj, :kc // 2], qa), _dot(ka_ref[hh, cj, kc // 2:], qa)], axis=0)
        s = jnp.where(key_pos <= q_pos, s, NEG_BIG)
        m = jnp.max(s, axis=0, keepdims=True)
        p = jnp.exp2(s - m)
        qas.append(qa)
        ms.append(m)
        ls.append(jnp.sum(p, axis=0, keepdims=True))
        accs.append(_dot(vt_ref[hh, cj], p.astype(BF16)))

    def past_chunk(c, carry):
        m_all, l_all, acc_all = carry
        ms, ls, accs = [], [], []
        for hh in range(hp):
            m, l, acc = m_all[hh:hh + 1], l_all[hh:hh + 1], acc_all[hh * dh:(hh + 1) * dh]
            s = jnp.concatenate([_dot(ka_ref[hh, c, :kc // 2], qas[hh]), _dot(ka_ref[hh, c, kc // 2:], qas[hh])],
                                axis=0)
            m_new = jnp.maximum(m, jnp.max(s, axis=0, keepdims=True))
            p = jnp.exp2(s - m_new)
            alpha = jnp.exp2(m - m_new)
            ms.append(m_new)
            ls.append(alpha * l + jnp.sum(p, axis=0, keepdims=True))
            accs.append(alpha * acc + _dot(vt_ref[hh, c], p.astype(BF16)))
        return jnp.concatenate(ms, axis=0), jnp.concatenate(ls, axis=0), jnp.concatenate(accs, axis=0)

    _, l_all, acc_all = lax.fori_loop(
        0, cj, past_chunk,
        (jnp.concatenate(ms, axis=0), jnp.concatenate(ls, axis=0), jnp.concatenate(accs, axis=0)))
    outs = [_dot_nt(eye_ref[...], (acc_all[hh * dh:(hh + 1) * dh] / l_all[hh:hh + 1]).astype(BF16))
            for hh in range(hp)]
    o_ref[...] = jnp.concatenate(outs, axis=1).astype(BF16)


def _moba_prompt(q, k, v):
    b, h, s, dh = q.shape
    hp = MOBA_HEADS_PER_STEP
    n_blk = s // MOBA_BLOCK
    assert n_blk % MOBA_CHUNK_BLOCKS == 0 and h % hp == 0 and hp * dh == LANES
    n_chunks = n_blk // MOBA_CHUNK_BLOCKS
    kc = MOBA_CHUNK_BLOCKS * MOBA_BLOCK
    full = pl.BlockSpec((1, hp, s, dh), lambda bi, hi, j: (bi, hi, 0, 0))
    return pl.pallas_call(
        _moba_prompt_kernel,
        grid=(b, h // hp, n_blk),
        in_specs=[pl.BlockSpec((1, hp, MOBA_BLOCK, dh), lambda bi, hi, j: (bi, hi, j, 0)), full, full],
        out_specs=pl.BlockSpec((MOBA_BLOCK, hp * dh), lambda bi, hi, j: (bi * n_blk + j, hi)),
        out_shape=jax.ShapeDtypeStruct((b * s, h * dh), BF16),
        scratch_shapes=[pltpu.VMEM((hp, n_blk, dh), F32),
                        pltpu.VMEM((hp, n_chunks, kc, dh + n_blk), BF16),
                        pltpu.VMEM((hp, n_chunks, dh, kc), BF16),
                        pltpu.VMEM((MOBA_BLOCK, MOBA_BLOCK), BF16)],
        compiler_params=_params("parallel", "parallel", "arbitrary"),
        name="moba_prompt",
    )(q, k, v)


SELECT_PAGES_PER_STEP = 16


def _select_kernel(pt_ref, q_ref, ck_ref, o_ref, buf, pm_ref, sem, *, layer, pages_per_seq):
    n_grp = pl.num_programs(1)
    g = pl.program_id(1)
    step = pl.program_id(0) * n_grp + g
    n_steps = pl.num_programs(0) * n_grp
    slot = step % 2
    pps, n_heads, dh, ps = buf.shape[1], buf.shape[2], buf.shape[3], buf.shape[4]
    t = q_ref.shape[2]

    def page_copy(st, sl, i, pg):
        return pltpu.make_async_copy(ck_ref.at[layer, pg], buf.at[sl, i], sem.at[sl])

    def fetch(st, sl):
        base = (st // n_grp) * pages_per_seq + (st % n_grp) * pps
        for i in range(pps):
            page_copy(st, sl, i, pt_ref[base + i]).start()

    @pl.when(step == 0)
    def _():
        fetch(step, slot)

    @pl.when(step + 1 < n_steps)
    def _():
        fetch(step + 1, 1 - slot)

    for i in range(pps):
        page_copy(step, slot, i, 0).wait()

    diag = lax.broadcasted_iota(I32, (dh, ps), 0) == lax.broadcasted_iota(I32, (dh, ps), 1)
    x = buf[slot]
    r = jnp.sum(x, axis=-1, keepdims=True) * (1.0 / ps)
    pm = jnp.sum(jnp.where(diag, r, 0.0), axis=2)
    pm_ref[pl.ds(pl.multiple_of(g * pps * n_heads, pps * n_heads), pps * n_heads), :] = (
        pm.reshape(pps * n_heads, ps))

    @pl.when(g == n_grp - 1)
    def _():
        n_blk = pages_per_seq // PAGES_PER_BLOCK
        lane = lax.broadcasted_iota(I32, (t, n_blk), 1).astype(F32)
        for h in range(n_heads):
            bm = pm_ref[pl.ds(h, n_blk, stride=PAGES_PER_BLOCK * n_heads), :]
            for pi in range(1, PAGES_PER_BLOCK):
                bm = bm + pm_ref[pl.ds(pi * n_heads + h, n_blk, stride=PAGES_PER_BLOCK * n_heads), :]
            bm = bm * (1.0 / PAGES_PER_BLOCK)
            sc = _dot3(q_ref[0, h], bm[:, :dh], _dot_nt)
            res = jnp.zeros((t, n_blk), F32)
            for k in range(MOBA_TOPK):
                m = jnp.max(sc, axis=1, keepdims=True)
                idx = jnp.min(jnp.where(sc == m, lane, float(n_blk)), axis=1, keepdims=True)
                res = jnp.where(lane == k, idx, res)
                sc = jnp.where(lane == idx, -jnp.inf, sc)
            o_ref[0, h] = res.astype(I32)


def _select_blocks(q, cache_kt, page_table, layer):
    n, h, t, dh = q.shape
    ps = cache_kt.shape[-1]
    pages_per_seq = page_table.shape[1]
    pps = SELECT_PAGES_PER_STEP
    assert pages_per_seq % pps == 0 and pages_per_seq % PAGES_PER_BLOCK == 0 and dh <= ps
    n_blk = pages_per_seq // PAGES_PER_BLOCK
    return pl.pallas_call(
        functools.partial(_select_kernel, layer=layer, pages_per_seq=pages_per_seq),
        grid_spec=pltpu.PrefetchScalarGridSpec(
            num_scalar_prefetch=1,
            grid=(n, pages_per_seq // pps),
            in_specs=[pl.BlockSpec((1, h, t, dh), lambda i, g, pt: (i, 0, 0, 0)),
                      pl.BlockSpec(memory_space=pl.ANY)],
            out_specs=pl.BlockSpec((1, h, t, n_blk), lambda i, g, pt: (i, 0, 0, 0)),
            scratch_shapes=[pltpu.VMEM((2, pps, h, dh, ps), F32),
                            pltpu.VMEM((pages_per_seq * h, ps), F32),
                            pltpu.SemaphoreType.DMA((2,))]),
        out_shape=jax.ShapeDtypeStruct((n, h, t, n_blk), I32),
        compiler_params=_params("arbitrary", "arbitrary"),
        name="moba_select",
    )(page_table.reshape(-1), q, cache_kt)


def _moba_sample_kernel(sel_ref, pt_ref, q_ref, kn_ref, vn_ref, ck_ref, cv_ref, o_ref, kbuf, vbuf, sem, *,
                        layer, pages_per_seq):
    n_heads = pl.num_programs(1)
    step = pl.program_id(0) * n_heads + pl.program_id(1)
    n_steps = pl.num_programs(0) * n_heads
    t = q_ref.shape[2]
    n_pages = kbuf.shape[1]
    pages_per_q = n_pages // t
    slot = step % 2

    def page_copies(st, sl, i, pg):
        hh = st % n_heads
        return (pltpu.make_async_copy(ck_ref.at[layer, pg, hh], kbuf.at[sl, i], sem.at[0, sl]),
                pltpu.make_async_copy(cv_ref.at[layer, pg, hh], vbuf.at[sl, i], sem.at[1, sl]))

    def fetch(st, sl):
        pt_base = (st // n_heads) * pages_per_seq
        for b in range(n_pages // PAGES_PER_BLOCK):
            blk = sel_ref[st * (n_pages // PAGES_PER_BLOCK) + b]
            for pi in range(PAGES_PER_BLOCK):
                pg = pt_ref[pt_base + blk * PAGES_PER_BLOCK + pi]
                for cp in page_copies(st, sl, b * PAGES_PER_BLOCK + pi, pg):
                    cp.start()

    @pl.when(step == 0)
    def _():
        fetch(step, slot)

    @pl.when(step + 1 < n_steps)
    def _():
        fetch(step + 1, 1 - slot)

    for i in range(n_pages):
        for cp in page_copies(step, slot, i, 0):
            cp.wait()

    q = q_ref[0, 0]
    qb = q.astype(BF16)
    s = jnp.concatenate([_dot(qb, kbuf[slot, i].astype(BF16)) for i in range(n_pages)], axis=1)
    per_q = pages_per_q * PAGE_SIZE
    col = lax.broadcasted_iota(I32, s.shape, 1)
    lo = lax.broadcasted_iota(I32, s.shape, 0) * per_q
    own = (col >= lo) & (col < lo + per_q)
    sn = _dot_nt(q, kn_ref[0, 0])
    causal = lax.broadcasted_iota(I32, (t, t), 1) <= lax.broadcasted_iota(I32, (t, t), 0)
    m = jnp.maximum(jnp.max(jnp.where(own, s, NEG_BIG), axis=1, keepdims=True),
                    jnp.max(jnp.where(causal, sn, NEG_BIG), axis=1, keepdims=True))
    p = jnp.where(own, jnp.exp(s - m), 0.0).astype(BF16)
    pn = jnp.where(causal, jnp.exp(sn - m), 0.0)
    l = jnp.sum(p.astype(F32), axis=1, keepdims=True) + jnp.sum(pn, axis=1, keepdims=True)
    o = _dot(pn, vn_ref[0, 0])
    for i in range(n_pages):
        o = o + _dot_nt(p[:, i * PAGE_SIZE:(i + 1) * PAGE_SIZE], vbuf[slot, i].astype(BF16))
    o_ref[0, 0] = o / l


def _moba_sample(q, k_new, v_new, cache_kt, cache_vt, sel, page_table, layer):
    n, h, t, dh = q.shape
    n_pages = t * MOBA_TOPK * PAGES_PER_BLOCK
    blk = pl.BlockSpec((1, 1, t, dh), lambda i, j, sl, pt: (i, j, 0, 0))
    anyspec = pl.BlockSpec(memory_space=pl.ANY)
    return pl.pallas_call(
        functools.partial(_moba_sample_kernel, layer=layer, pages_per_seq=page_table.shape[1]),
        grid_spec=pltpu.PrefetchScalarGridSpec(
            num_scalar_prefetch=2,
            grid=(n, h),
            in_specs=[blk, blk, blk, anyspec, anyspec],
            out_specs=blk,
            scratch_shapes=[pltpu.VMEM((2, n_pages, dh, PAGE_SIZE), F32),
                            pltpu.VMEM((2, n_pages, dh, PAGE_SIZE), F32),
                            pltpu.SemaphoreType.DMA((2, 2))]),
        out_shape=jax.ShapeDtypeStruct((n, h, t, dh), F32),
        compiler_params=_params("arbitrary", "arbitrary"),
        name="moba_sample",
    )(sel.reshape(-1), page_table.reshape(-1), q, k_new, v_new, cache_kt, cache_vt)


def _merge_kernel(xp_ref, xs_ref, oap_ref, oas_ref, obp_ref, obs_ref, ga_ref, gb_ref, g1_ref, sh_ref, sc_ref,
                  nf_ref, wpa_ref, wpb_ref, wo_ref, wr_ref, br_ref, x1_ref, h2_ref, eid_ref, gw_ref, *, n_first):
    pa = _dot(_pick_rows(n_first, oap_ref, oas_ref).astype(BF16), wpa_ref[...])
    pb = _dot(_pick_rows(n_first, obp_ref, obs_ref), wpb_ref[...])
    mix = jax.nn.sigmoid(ga_ref[...]) * pa + jax.nn.sigmoid(gb_ref[...]) * pb
    x1 = _pick_rows(n_first, xp_ref, xs_ref) + g1_ref[0] * _dot(mix.astype(BF16), wo_ref[...])
    x1_ref[...] = x1
    h2 = x1 * lax.rsqrt(jnp.mean(x1 * x1, axis=-1, keepdims=True) + NORM_EPS) * nf_ref[...]
    h2 = h2 * (1.0 + sc_ref[0]) + sh_ref[0]
    for s in range(h2_ref.shape[1]):
        h2_ref[:, s, :] = h2[:, s * LANES:(s + 1) * LANES]

    logits = _dot3(h2, wr_ref[...]) + br_ref[...]
    lane = lax.broadcasted_iota(I32, logits.shape, 1)
    lanef = lane.astype(F32)
    is_g = lane < N_GROUPS
    gl = jnp.where(is_g, logits, -jnp.inf)
    gmax = jnp.max(gl, axis=1, keepdims=True)
    g_sel = jnp.min(jnp.where(gl == gmax, lanef, float(LANES)), axis=1, keepdims=True)
    p_group = 1.0 / jnp.sum(jnp.exp(gl - gmax), axis=1, keepdims=True)
    e_lo = N_GROUPS + g_sel * EXPERTS_PER_GROUP
    in_grp = (lanef >= e_lo) & (lanef < e_lo + EXPERTS_PER_GROUP)
    el = jnp.where(in_grp, logits, -jnp.inf)
    m1 = jnp.max(el, axis=1, keepdims=True)
    i1 = jnp.min(jnp.where(el == m1, lanef, float(LANES)), axis=1, keepdims=True)
    el2 = jnp.where(lanef == i1, -jnp.inf, el)
    m2 = jnp.max(el2, axis=1, keepdims=True)
    i2 = jnp.min(jnp.where(el2 == m2, lanef, float(LANES)), axis=1, keepdims=True)
    e2 = jnp.exp(m2 - m1)
    w1 = p_group / (1.0 + e2)
    w2 = p_group * e2 / (1.0 + e2)
    eid = jnp.where(lane == 0, i1 - N_GROUPS, jnp.where(lane == 1, i2 - N_GROUPS, 0.0))
    eid_ref[...] = eid.astype(I32)
    gw_ref[...] = jnp.where(lane == 0, w1, jnp.where(lane == 1, w2, 0.0))


def _merge(xp, xs, oa_p, oa_s, ob_p, ob_s, zg, mod_tiles, norm_ffn, wpa, wpb, wo, w_router, b_router,
           tiles_per_seq):
    d = xp.shape[1]
    n_first = xp.shape[0] // ROW_TILE
    m = xp.shape[0] + xs.shape[0]
    n_mod = mod_tiles.shape[1]
    mod_map = lambda i: (jnp.minimum(i // tiles_per_seq, n_mod - 1), 0, 0)
    const = lambda i: (0, 0)
    row = lambda i: (i, 0)
    mod = pl.BlockSpec((1, ROW_TILE, d), mod_map)
    wspec = lambda w: pl.BlockSpec(w.shape, const, pipeline_mode=pl.Buffered(1))
    return pl.pallas_call(
        functools.partial(_merge_kernel, n_first=n_first),
        grid=(m // ROW_TILE,),
        in_specs=_two_group_specs(d, n_first) + _two_group_specs(A_WIDTH, n_first)
                 + _two_group_specs(B_WIDTH, n_first) + [
                  pl.BlockSpec((ROW_TILE, d), lambda i: (i, 0)),
                  pl.BlockSpec((ROW_TILE, d), lambda i: (i, 1)),
                  mod, mod, mod,
                  pl.BlockSpec((1, d), const),
                  wspec(wpa), wspec(wpb), wspec(wo), wspec(w_router),
                  pl.BlockSpec((1, LANES), const)],
        out_specs=[pl.BlockSpec((ROW_TILE, d), row),
                   pl.BlockSpec((ROW_TILE, d // LANES, LANES), lambda i: (i, 0, 0)),
                   pl.BlockSpec((ROW_TILE, LANES), row),
                   pl.BlockSpec((ROW_TILE, LANES), row)],
        out_shape=[jax.ShapeDtypeStruct((m, d), F32),
                   jax.ShapeDtypeStruct((m, d // LANES, LANES), F32),
                   jax.ShapeDtypeStruct((m, LANES), I32),
                   jax.ShapeDtypeStruct((m, LANES), F32)],
        compiler_params=_params("parallel"),
        name="merge_router",
    )(xp, xs, oa_p, oa_s, ob_p, ob_s, zg, zg, mod_tiles[2], mod_tiles[3], mod_tiles[4],
      norm_ffn.reshape(1, d), wpa, wpb, wo, w_router, b_router)


def _moe_kernel(be_ref, tok_ref, h_ref, wg_ref, wu_ref, wd_ref, y_ref, xbuf, sem, wg_s, wu_s, wd_s):
    i = pl.program_id(0)
    last = pl.num_programs(0) - 1
    slot = i % 2
    n_sub = h_ref.shape[1]

    def row_copy(blk, sl, r):
        return pltpu.make_async_copy(h_ref.at[tok_ref[blk * MOE_TILE + r]],
                                     xbuf.at[sl, pl.ds(r * n_sub, n_sub), :], sem.at[sl])

    def fetch(blk, sl):
        for r in range(MOE_TILE):
            row_copy(blk, sl, r).start()

    def drain(sl):
        for r in range(MOE_TILE):
            row_copy(0, sl, r).wait()

    @pl.when(i == 0)
    def _():
        fetch(i, slot)

    prev = be_ref[jnp.maximum(i - 1, 0)]
    fresh = (i == 0) | (be_ref[i] != prev)

    @pl.when(fresh)
    def _():
        wg_s[...] = wg_ref[...].astype(BF16)
        wu_s[...] = wu_ref[...].astype(BF16)
        wd_s[...] = wd_ref[...].astype(BF16)

    drain(slot)
    fetch(jnp.minimum(i + 1, last), 1 - slot)
    xb = jnp.concatenate([xbuf[slot, pl.ds(s, MOE_TILE, stride=n_sub), :].astype(BF16) for s in range(n_sub)],
                         axis=1)
    hid = _silu(_dot(xb, wg_s[...])) * _dot(xb, wu_s[...])
    y_ref[...] = _dot(hid.astype(BF16), wd_s[...])

    @pl.when(i == last)
    def _():
        drain(1 - slot)


def _moe(h, row_tok, block_expert, w_gate, w_up, w_down, layer):
    n_sub, lanes = h.shape[1], h.shape[2]
    d = n_sub * lanes
    n_rows = row_tok.shape[0]
    de = w_gate.shape[-1]
    return pl.pallas_call(
        _moe_kernel,
        grid_spec=pltpu.PrefetchScalarGridSpec(
            num_scalar_prefetch=2,
            grid=(n_rows // MOE_TILE,),
            in_specs=[pl.BlockSpec(memory_space=pl.ANY),
                      pl.BlockSpec((None, None, d, de), lambda i, be, tk: (layer, be[i], 0, 0)),
                      pl.BlockSpec((None, None, d, de), lambda i, be, tk: (layer, be[i], 0, 0)),
                      pl.BlockSpec((None, None, de, d), lambda i, be, tk: (layer, be[i], 0, 0))],
            out_specs=pl.BlockSpec((MOE_TILE, d), lambda i, be, tk: (i, 0)),
            scratch_shapes=[pltpu.VMEM((2, MOE_TILE * n_sub, lanes), F32),
                            pltpu.SemaphoreType.DMA((2,)),
                            pltpu.VMEM((d, de), BF16), pltpu.VMEM((d, de), BF16), pltpu.VMEM((de, d), BF16)]),
        out_shape=jax.ShapeDtypeStruct((n_rows, d), F32),
        compiler_params=_params("arbitrary"),
        name="moe_experts",
    )(block_expert, row_tok, h, w_gate, w_up, w_down)


def _final_kernel(x1_ref, y0_ref, y1_ref, gw_ref, g2_ref, nf_ref, op_ref, os_ref, *, last, n_first):
    gw = gw_ref[...]
    f = gw[:, 0:1] * y0_ref[...] + gw[:, 1:2] * y1_ref[...]
    x2 = x1_ref[...] + g2_ref[0] * f
    if last:
        x2 = x2 * lax.rsqrt(jnp.mean(x2 * x2, axis=-1, keepdims=True) + NORM_EPS) * nf_ref[...]
    i = pl.program_id(0)

    @pl.when(i < n_first)
    def _():
        op_ref[...] = x2

    @pl.when(i >= n_first)
    def _():
        os_ref[...] = x2


def _final(x1, y0, y1, gw, mod_tiles, norm_final, tiles_per_seq, last, n_first):
    m, d = x1.shape
    n_mod = mod_tiles.shape[1]
    mod_map = lambda i: (jnp.minimum(i // tiles_per_seq, n_mod - 1), 0, 0)
    row = lambda i: (i, 0)
    rs = pl.BlockSpec((ROW_TILE, d), row)
    return pl.pallas_call(
        functools.partial(_final_kernel, last=last, n_first=n_first),
        grid=(m // ROW_TILE,),
        in_specs=[rs, rs, rs, pl.BlockSpec((ROW_TILE, LANES), row),
                  pl.BlockSpec((1, ROW_TILE, d), mod_map),
                  pl.BlockSpec((1, d), lambda i: (0, 0))],
        out_specs=_two_group_specs(d, n_first),
        out_shape=[jax.ShapeDtypeStruct((n_first * ROW_TILE, d), F32),
                   jax.ShapeDtypeStruct((m - n_first * ROW_TILE, d), F32)],
        compiler_params=_params("arbitrary"),
        name="combine_final",
    )(x1, y0, y1, gw, mod_tiles[5], norm_final.reshape(1, d))


def _dispatch(eid):
    m = eid.shape[0]
    n_assign = m * EXPERT_TOPK
    flat_e = eid.reshape(-1)
    flat_tok = jnp.repeat(jnp.arange(m, dtype=I32), EXPERT_TOPK)
    onehot = (flat_e[:, None] == jnp.arange(N_EXPERTS, dtype=I32)[None, :]).astype(I32)
    csum = jnp.cumsum(onehot, axis=0)
    rank = jnp.take_along_axis(csum, flat_e[:, None], axis=1)[:, 0] - 1
    counts = csum[-1]
    padded = (counts + MOE_TILE - 1) // MOE_TILE * MOE_TILE
    pad_end = jnp.cumsum(padded)
    pad_start = pad_end - padded
    dest = (pad_start[flat_e] + rank).astype(I32)
    n_blocks = -(-(n_assign + N_EXPERTS * (MOE_TILE - 1)) // MOE_TILE)
    row_tok = jnp.zeros((n_blocks * MOE_TILE,), I32).at[dest].set(flat_tok)
    blk_start = jnp.arange(n_blocks, dtype=I32) * MOE_TILE
    block_expert = jnp.minimum(jnp.sum((pad_end[None, :] <= blk_start[:, None]).astype(I32), axis=1),
                               N_EXPERTS - 1).astype(I32)
    return dest.reshape(m, EXPERT_TOPK), row_tok, block_expert


def _mod_tiles(mod, n_prompt, reps):
    d = mod.shape[-1]
    mp = jnp.broadcast_to(mod[:n_prompt, :, None, :], (n_prompt, 6, ROW_TILE, d))
    ms = jnp.repeat(mod[n_prompt:], reps, axis=0).transpose(1, 0, 2)[None]
    return jnp.concatenate([mp, ms], axis=0).transpose(1, 0, 2, 3)


def kernel(x_prompt, x_sample, c_prompt, c_sample, cache_k, cache_v, state_hgrn, page_table,
           w_ada, b_ada, norm_mix, norm_ffn, w_in, hgrn_lb_logits, hgrn_norm,
           w_proj_a, w_proj_b, w_out, w_group, b_group, w_expert_router, b_expert_router,
           w_gate, w_up, w_down, norm_final):
    bp, tp, d = x_prompt.shape
    bs, ts, _ = x_sample.shape
    mp_rows, ms_rows = bp * tp, bs * ts
    assert ms_rows == ROW_TILE and tp % ROW_TILE == 0
    m = mp_rows + ms_rows
    tiles_per_seq = tp // ROW_TILE
    n_full = PAST_LEN // MOBA_BLOCK
    assert PAST_LEN % MOBA_BLOCK == 0 and n_full >= MOBA_TOPK

    xp, xs = x_prompt.reshape(mp_rows, d), x_sample.reshape(ms_rows, d)
    c_all = jnp.concatenate([c_prompt, c_sample, jnp.zeros((-(bp + bs) % 8, d), F32)], axis=0)
    lb_all = jnp.cumsum(jax.nn.softmax(hgrn_lb_logits.astype(F32), axis=0), axis=0)
    cache_kt = cache_k.transpose(0, 1, 2, 4, 3)
    cache_vt = cache_v.transpose(0, 1, 2, 4, 3)
    kp_l, vp_l, sp_l, ks_l, vs_l, ss_l = [], [], [], [], [], []

    for l in range(DEPTH):
        mod = _modulation(c_all, w_ada[l], b_ada[l])[:bp + bs].reshape(bp + bs, 6, d)
        mt = _mod_tiles(mod, bp, ts)
        za, zb, zg = _inproj(xp, xs, mt, norm_mix[l], w_in[l], tiles_per_seq)

        oa_p, st_p = _hgrn(za, lb_all[l], hgrn_norm[l], 0, bp, tp, ROW_TILE, HGRN_CHUNK)
        oa_s, st_s = _hgrn(za, lb_all[l], hgrn_norm[l], mp_rows, bs, ts, ts, ts, s0=state_hgrn[l])

        q_p, k_p, v_p = _rope_split(zb, 0, bp, tp, 512, jnp.arange(tp))
        q_s, k_s, v_s = _rope_split(zb, mp_rows, bs, ts, ts, PAST_LEN + jnp.arange(ts))
        ob_p = _moba_prompt(q_p, k_p, v_p)
        sel = _select_blocks(q_s, cache_kt, page_table[:, :n_full * PAGES_PER_BLOCK], l)[..., :MOBA_TOPK]
        ob_s = _moba_sample(q_s, k_s, v_s, cache_kt, cache_vt, sel, page_table, l)
        ob_s = ob_s.transpose(0, 2, 1, 3).reshape(ms_rows, B_WIDTH).astype(BF16)

        w_router = jnp.concatenate(
            [w_group[l], w_expert_router[l].transpose(1, 0, 2).reshape(d, N_EXPERTS),
             jnp.zeros((d, LANES - N_GROUPS - N_EXPERTS), F32)], axis=1)
        b_router = jnp.concatenate(
            [b_group[l], b_expert_router[l].reshape(-1),
             jnp.zeros((LANES - N_GROUPS - N_EXPERTS,), F32)]).reshape(1, LANES)
        x1, h2, eid, gw = _merge(xp, xs, oa_p, oa_s, ob_p, ob_s, zg, mt, norm_ffn[l], w_proj_a[l].astype(BF16),
                                 w_proj_b[l].astype(BF16), w_out[l].astype(BF16), w_router, b_router,
                                 tiles_per_seq)

        dest, row_tok, block_expert = _dispatch(eid[:, :EXPERT_TOPK])
        ys = _moe(h2, row_tok, block_expert, w_gate, w_up, w_down, l)
        xp, xs = _final(x1, ys[dest[:, 0]], ys[dest[:, 1]], gw, mt, norm_final, tiles_per_seq,
                        l == DEPTH - 1, mp_rows // ROW_TILE)

        kp_l.append(k_p); vp_l.append(v_p); sp_l.append(st_p)
        ks_l.append(k_s); vs_l.append(v_s); ss_l.append(st_s)

    y_prompt = xp.reshape(bp, tp, d)
    y_sample = xs.reshape(bs, ts, d)
    return (y_prompt, y_sample, jnp.stack(kp_l), jnp.stack(vp_l), jnp.stack(sp_l),
            jnp.stack(ks_l), jnp.stack(vs_l), jnp.stack(ss_l))
```

```python
import functools

import numpy as np
import jax
import jax.numpy as jnp
from jax import lax
from jax.experimental import pallas as pl
from jax.experimental.pallas import tpu as pltpu

F32 = jnp.float32
BF16 = jnp.bfloat16
I32 = jnp.int32

D_MODEL = 1024
DEPTH = 1
PAST_LEN = 16384
PAGE_SIZE = 128
A_HEADS = 4
A_KDIM = 128
A_VDIM = 128
A_WIDTH = A_HEADS * A_KDIM
B_HEADS = 8
B_HEAD_DIM = 64
B_WIDTH = B_HEADS * B_HEAD_DIM
MOBA_BLOCK = 256
MOBA_TOPK = 3
ROT_DIM = B_HEAD_DIM // 4
ROPE_THETA = 500000.0
N_GROUPS = 4
EXPERTS_PER_GROUP = 8
N_EXPERTS = N_GROUPS * EXPERTS_PER_GROUP
EXPERT_TOPK = 2
D_EXPERT = D_MODEL // 2
NORM_EPS = 1e-6

PAGES_PER_BLOCK = MOBA_BLOCK // PAGE_SIZE
ROW_TILE = 256
HGRN_CHUNK = 16
MOE_TILE = 256
LANES = 128
NEG_BIG = -1e30
LOG2_E = 1.4426950408889634
VMEM_LIMIT_BYTES = 52 * 1024 * 1024


def _params(*sem):
    return pltpu.CompilerParams(dimension_semantics=sem, vmem_limit_bytes=VMEM_LIMIT_BYTES)


def _dot(a, b):
    return jnp.dot(a, b, preferred_element_type=F32)


def _dot_nt(a, b):
    return lax.dot_general(a, b, (((1,), (1,)), ((), ())), preferred_element_type=F32)


def _dot_tn(a, b):
    return lax.dot_general(a, b, (((0,), (0,)), ((), ())), preferred_element_type=F32)


def _split(a):
    hi = a.astype(BF16)
    return hi, (a - hi.astype(F32)).astype(BF16)


def _dot3(a, b, dot=_dot):
    ah, al = _split(a)
    bh, bl = _split(b)
    return dot(ah, bh) + (dot(ah, bl) + dot(al, bh))


def _silu(x):
    return x * jax.nn.sigmoid(x)


def _mod_kernel(c_ref, w_ref, b_ref, o_ref):
    o_ref[...] = _dot3(_silu(c_ref[...]), w_ref[...]) + b_ref[...]


def _modulation(c_all, w, b):
    n = c_all.shape[0]
    d, dout = w.shape
    return pl.pallas_call(
        _mod_kernel,
        grid=(dout // d,),
        in_specs=[pl.BlockSpec((n, d), lambda j: (0, 0)),
                  pl.BlockSpec((d, d), lambda j: (0, j)),
                  pl.BlockSpec((1, d), lambda j: (0, j))],
        out_specs=pl.BlockSpec((n, d), lambda j: (0, j)),
        out_shape=jax.ShapeDtypeStruct((n, dout), F32),
        compiler_params=_params("parallel"),
        name="modulation",
    )(c_all, w, b.reshape(1, dout))


def _pick_rows(n_first, first_ref, second_ref):
    return jnp.where(pl.program_id(0) < n_first, first_ref[...], second_ref[...])


def _two_group_specs(width, n_first):
    return [pl.BlockSpec((ROW_TILE, width), lambda i: (jnp.minimum(i, n_first - 1), 0)),
            pl.BlockSpec((ROW_TILE, width), lambda i: (0, 0))]


def _inproj_kernel(xp_ref, xs_ref, sh_ref, sc_ref, g_ref, w_ref, wlo_ref, za_ref, zb_ref, zg_ref, *, n_first):
    x = _pick_rows(n_first, xp_ref, xs_ref)
    h = x * lax.rsqrt(jnp.mean(x * x, axis=-1, keepdims=True) + NORM_EPS) * g_ref[...]
    h = h * (1.0 + sc_ref[0]) + sh_ref[0]
    hh, hl = _split(h)
    wa = 4 * A_WIDTH
    for c in range(0, wa, 512):
        za_ref[:, c:c + 512] = _dot(hh, w_ref[:, c:c + 512])
    for c in range(0, 2 * B_WIDTH, 512):
        wc = w_ref[:, wa + c:wa + c + 512]
        zb_ref[:, c:c + 512] = _dot(hh, wc) + (_dot(hh, wlo_ref[:, c:c + 512]) + _dot(hl, wc))
    c = 2 * B_WIDTH
    zb_ref[:, c:c + 512] = _dot(hh, w_ref[:, wa + c:wa + c + 512])
    wg = wa + 3 * B_WIDTH
    for c in range(0, 2 * D_MODEL, 512):
        zg_ref[:, c:c + 512] = _dot(hh, w_ref[:, wg + c:wg + c + 512])


def _inproj(xp, xs, mod_tiles, norm_g, w_in, tiles_per_seq):
    d = xp.shape[1]
    n_first = xp.shape[0] // ROW_TILE
    m = xp.shape[0] + xs.shape[0]
    n_mod = mod_tiles.shape[1]
    w_hi = w_in.astype(BF16)
    wa = 4 * A_WIDTH
    wqk = w_in[:, wa:wa + 2 * B_WIDTH]
    w_lo = (wqk - wqk.astype(BF16).astype(F32)).astype(BF16)
    mod_map = lambda i: (jnp.minimum(i // tiles_per_seq, n_mod - 1), 0, 0)
    const = lambda i: (0, 0)
    row = lambda i: (i, 0)
    return pl.pallas_call(
        functools.partial(_inproj_kernel, n_first=n_first),
        grid=(m // ROW_TILE,),
        in_specs=_two_group_specs(d, n_first) + [
                  pl.BlockSpec((1, ROW_TILE, d), mod_map),
                  pl.BlockSpec((1, ROW_TILE, d), mod_map),
                  pl.BlockSpec((1, d), const),
                  pl.BlockSpec(w_hi.shape, const, pipeline_mode=pl.Buffered(1)),
                  pl.BlockSpec(w_lo.shape, const, pipeline_mode=pl.Buffered(1))],
        out_specs=[pl.BlockSpec((ROW_TILE, wa), row),
                   pl.BlockSpec((ROW_TILE, 3 * B_WIDTH), row),
                   pl.BlockSpec((ROW_TILE, 2 * D_MODEL), row)],
        out_shape=[jax.ShapeDtypeStruct((m, wa), F32),
                   jax.ShapeDtypeStruct((m, 3 * B_WIDTH), F32),
                   jax.ShapeDtypeStruct((m, 2 * D_MODEL), F32)],
        compiler_params=_params("parallel"),
        name="inproj",
    )(xp, xs, mod_tiles[0], mod_tiles[1], norm_g.reshape(1, d), w_hi, w_lo)


def _rope_kernel(q_ref, k_ref, v_ref, c_ref, s1_ref, s2_ref, qo_ref, ko_ref, vo_ref):
    cos, s1, s2 = c_ref[...], s1_ref[...], s2_ref[...]
    half = ROT_DIM // 2

    def rope(x):
        up = pltpu.roll(x, B_WIDTH - half, axis=1)
        dn = pltpu.roll(x, half, axis=1)
        return x * cos + up * s1 + dn * s2

    q = rope(q_ref[...]) * (B_HEAD_DIM ** -0.5)
    k = rope(k_ref[...])
    v = v_ref[...]
    for h in range(B_HEADS):
        ls = slice(h * B_HEAD_DIM, (h + 1) * B_HEAD_DIM)
        qo_ref[0, h] = q[:, ls]
        ko_ref[0, h] = k[:, ls]
        vo_ref[0, h] = v[:, ls]


def _rope_tables(pos):
    half = ROT_DIM // 2
    inv = jnp.power(ROPE_THETA, -(jnp.arange(half, dtype=F32) * 2.0 / ROT_DIM))
    ang = pos.astype(F32)[:, None] * inv[None, :]
    cos, sin = jnp.cos(ang), jnp.sin(ang)
    t = pos.shape[0]
    rest = B_HEAD_DIM - ROT_DIM
    c = jnp.concatenate([cos, cos, jnp.ones((t, rest), F32)], axis=-1)
    s1 = jnp.concatenate([-sin, jnp.zeros((t, half + rest), F32)], axis=-1)
    s2 = jnp.concatenate([jnp.zeros((t, half), F32), sin, jnp.zeros((t, rest), F32)], axis=-1)
    return [jnp.tile(a, (1, B_HEADS)) for a in (c, s1, s2)]


def _rope_split(zb, row0, n, t, tile, pos):
    tabs = _rope_tables(pos)
    tps = t // tile
    blk0 = row0 // tile
    zmap = lambda c: (lambda b, s: (blk0 + b * tps + s, c))
    tmap = lambda b, s: (s, 0)
    omap = lambda b, s: (b, 0, s, 0)
    oshape = jax.ShapeDtypeStruct((n, B_HEADS, t, B_HEAD_DIM), F32)
    ospec = pl.BlockSpec((1, B_HEADS, tile, B_HEAD_DIM), omap)
    return pl.pallas_call(
        _rope_kernel,
        grid=(n, tps),
        in_specs=[pl.BlockSpec((tile, B_WIDTH), zmap(0)),
                  pl.BlockSpec((tile, B_WIDTH), zmap(1)),
                  pl.BlockSpec((tile, B_WIDTH), zmap(2)),
                  pl.BlockSpec((tile, B_WIDTH), tmap),
                  pl.BlockSpec((tile, B_WIDTH), tmap),
                  pl.BlockSpec((tile, B_WIDTH), tmap)],
        out_specs=[ospec, ospec, ospec],
        out_shape=[oshape, oshape, oshape],
        compiler_params=_params("parallel", "parallel"),
        name="rope_split",
    )(zb, zb, zb, *tabs)


def _hgrn_kernel(*refs, chunk, has_s0):
    aq_ref, af_ref, ai_ref, ag_ref, lb_ref, gain_ref = refs[:6]
    rest = refs[6:]
    s0_ref = None
    if has_s0:
        s0_ref, rest = rest[0], rest[1:]
    o_ref, so_ref, st_ref, q_s, b_s, k_s = rest
    t = pl.program_id(1)
    tb = aq_ref.shape[0]

    @pl.when(t == 0)
    def _():
        for h in range(A_HEADS):
            if has_s0:
                st_ref[h] = s0_ref[0, h].T
            else:
                st_ref[h] = jnp.zeros((A_VDIM, A_KDIM), F32)

    lb = lb_ref[...]
    f = lb + (1.0 - lb) * jax.nn.sigmoid(af_ref[...])
    logf = jnp.log(f)
    q_s[...] = _silu(aq_ref[...])
    k_s[...] = 1.0 - f
    row = lax.broadcasted_iota(I32, logf.shape, 0) & (chunk - 1)
    b = logf
    sh = 1
    while sh < chunk:
        b = b + jnp.where(row >= sh, pltpu.roll(b, sh, axis=0), 0.0)
        sh *= 2
    b_s[...] = b
    rowc = lax.broadcasted_iota(I32, (chunk, A_KDIM), 0)

    def one_chunk(ci, carry):
        r0 = pl.multiple_of(ci * chunk, chunk)
        rs = pl.ds(r0, chunk)
        for h in range(A_HEADS):
            ls = slice(h * A_KDIM, (h + 1) * A_KDIM)
            qc, bc, kc, vc = q_s[rs, ls], b_s[rs, ls], k_s[rs, ls], ai_ref[rs, ls]
            st = st_ref[h]
            bl = bc[chunk - 1:chunk, :]
            o = _dot_nt((qc * jnp.exp(bc)).astype(BF16), st.astype(BF16))
            grp = 8
            parts = [o[g * grp:(g + 1) * grp] for g in range(chunk // grp)]
            for s in range(chunk):
                for g in range(s // grp, chunk // grp):
                    rows = slice(g * grp, (g + 1) * grp)
                    diff = bc[rows] - bc[s:s + 1, :]
                    if g == s // grp:
                        diff = jnp.where(rowc[rows] >= s, diff, -jnp.inf)
                    e = jnp.exp(diff)
                    r = jnp.sum(qc[rows] * e * kc[s:s + 1, :], axis=1, keepdims=True)
                    parts[g] = parts[g] + r * vc[s:s + 1, :]
            o = jnp.concatenate(parts, axis=0) if len(parts) > 1 else parts[0]
            kp = kc * jnp.exp(bl - bc)
            st_ref[h] = st * jnp.exp(bl) + _dot_tn(vc.astype(BF16), kp.astype(BF16))
            o_ref[rs, ls] = o
        return carry

    n_chunks = tb // chunk
    lax.fori_loop(0, n_chunks, one_chunk, 0, unroll=4 if n_chunks % 4 == 0 else 1)

    gain = gain_ref[...]
    for h in range(A_HEADS):
        ls = slice(h * A_VDIM, (h + 1) * A_VDIM)
        oh = o_ref[:, ls]
        y = oh * lax.rsqrt(jnp.mean(oh * oh, axis=-1, keepdims=True) + NORM_EPS) * gain
        o_ref[:, ls] = y * _silu(ag_ref[:, ls])

    @pl.when(t == pl.num_programs(1) - 1)
    def _():
        for h in range(A_HEADS):
            so_ref[0, h] = st_ref[h].T


def _hgrn(za, lb, gain, row0, n, t, tile, chunk, s0=None):
    tps = t // tile
    blk0 = row0 // tile
    zmap = lambda c: (lambda b, s: (blk0 + b * tps + s, c))
    const = lambda b, s: (0, 0)
    in_specs = [pl.BlockSpec((tile, A_WIDTH), zmap(c)) for c in range(4)]
    in_specs += [pl.BlockSpec((1, A_WIDTH), const), pl.BlockSpec((1, A_VDIM), const)]
    args = [za, za, za, za, lb.reshape(1, A_WIDTH), gain.reshape(1, A_VDIM)]
    if s0 is not None:
        in_specs.append(pl.BlockSpec((1, A_HEADS, A_KDIM, A_VDIM), lambda b, s: (b, 0, 0, 0)))
        args.append(s0)
    return pl.pallas_call(
        functools.partial(_hgrn_kernel, chunk=chunk, has_s0=s0 is not None),
        grid=(n, tps),
        in_specs=in_specs,
        out_specs=[pl.BlockSpec((tile, A_WIDTH), lambda b, s: (b * tps + s, 0)),
                   pl.BlockSpec((1, A_HEADS, A_KDIM, A_VDIM), lambda b, s: (b, 0, 0, 0))],
        out_shape=[jax.ShapeDtypeStruct((n * t, A_WIDTH), F32),
                   jax.ShapeDtypeStruct((n, A_HEADS, A_KDIM, A_VDIM), F32)],
        scratch_shapes=[pltpu.VMEM((A_HEADS, A_VDIM, A_KDIM), F32),
                        pltpu.VMEM((tile, A_WIDTH), F32),
                        pltpu.VMEM((tile, A_WIDTH), F32),
                        pltpu.VMEM((tile, A_WIDTH), F32)],
        compiler_params=_params("parallel", "arbitrary"),
        name="hgrn2",
    )(*args)


MOBA_CHUNK_BLOCKS = 4
MOBA_HEADS_PER_STEP = 2


def _moba_prompt_kernel(q_ref, k_ref, v_ref, o_ref, km_ref, ka_ref, vt_ref, eye_ref):
    j = pl.program_id(2)
    blk = MOBA_BLOCK
    hp, dh = q_ref.shape[1], q_ref.shape[3]
    n_chunks, kc = ka_ref.shape[1], ka_ref.shape[2]
    n_blk = n_chunks * MOBA_CHUNK_BLOCKS
    eye = (lax.broadcasted_iota(I32, (dh, dh), 0) == lax.broadcasted_iota(I32, (dh, dh), 1)).astype(BF16)

    @pl.when(j == 0)
    def _():
        eye_ref[...] = (lax.broadcasted_iota(I32, (blk, blk), 0)
                        == lax.broadcasted_iota(I32, (blk, blk), 1)).astype(BF16)
        for hh in range(hp):
            for i in range(n_blk):
                km_ref[hh, i:i + 1, :] = jnp.mean(k_ref[0, hh, i * blk:(i + 1) * blk, :], axis=0, keepdims=True)
            for c in range(n_chunks):
                rows = slice(c * kc, (c + 1) * kc)
                key_blk = lax.broadcasted_iota(I32, (kc, n_blk), 0) // blk + c * MOBA_CHUNK_BLOCKS
                onehot = key_blk == lax.broadcasted_iota(I32, (kc, n_blk), 1)
                ka_ref[hh, c, :, :dh] = k_ref[0, hh, rows, :].astype(BF16)
                ka_ref[hh, c, :, dh:] = jnp.where(onehot, 1.0, 0.0).astype(BF16)
                vt_ref[hh, c] = _dot_nt(eye, v_ref[0, hh, rows, :].astype(BF16)).astype(BF16)

    cj = j // MOBA_CHUNK_BLOCKS
    bidx = lax.broadcasted_iota(I32, (n_blk, blk), 0)
    valid = bidx < j
    key_pos = lax.broadcasted_iota(I32, (kc, blk), 0) + cj * kc
    q_pos = lax.broadcasted_iota(I32, (kc, blk), 1) + j * blk
    qas, ms, ls, accs = [], [], [], []
    for hh in range(hp):
        q = q_ref[0, hh]
        sc = jnp.where(valid, _dot3(km_ref[hh], q, _dot_nt), -jnp.inf)
        rank = jnp.zeros((n_blk, blk), I32)
        for i in range(n_blk - 1):
            row = sc[i:i + 1, :]
            rank = rank + jnp.where(row > sc, 1, jnp.where(row == sc, jnp.where(bidx > i, 1, 0), 0))
        bias_t = jnp.where(valid, jnp.where(rank < MOBA_TOPK, 0.0, NEG_BIG),
                           jnp.where(bidx == j, 0.0, NEG_BIG))
        q_t = _dot_nt(eye, (q * LOG2_E).astype(BF16))
        qa = jnp.concatenate([q_t, bias_t], axis=0).astype(BF16)
        s = jnp.concatenate([_dot(ka_ref[hh, cj, :kc // 2], qa), _dot(ka_ref[hh, cj, kc // 2:], qa)], axis=0)
        s = jnp.where(key_pos <= q_pos, s, NEG_BIG)
        m = jnp.max(s, axis=0, keepdims=True)
        p = jnp.exp2(s - m)
        qas.append(qa)
        ms.append(m)
        ls.append(jnp.sum(p, axis=0, keepdims=True))
        accs.append(_dot(vt_ref[hh, cj], p.astype(BF16)))

    def past_chunk(c, carry):
        m_all, l_all, acc_all = carry
        ms, ls, accs = [], [], []
        for hh in range(hp):
            m, l, acc = m_all[hh:hh + 1], l_all[hh:hh + 1], acc_all[hh * dh:(hh + 1) * dh]
            s = jnp.concatenate([_dot(ka_ref[hh, c, :kc // 2], qas[hh]), _dot(ka_ref[hh, c, kc // 2:], qas[hh])],
                                axis=0)
            m_new = jnp.maximum(m, jnp.max(s, axis=0, keepdims=True))
            p = jnp.exp2(s - m_new)
            alpha = jnp.exp2(m - m_new)
            ms.append(m_new)
            ls.append(alpha * l + jnp.sum(p, axis=0, keepdims=True))
            accs.append(alpha * acc + _dot(vt_ref[hh, c], p.astype(BF16)))
        return jnp.concatenate(ms, axis=0), jnp.concatenate(ls, axis=0), jnp.concatenate(accs, axis=0)

    _, l_all, acc_all = lax.fori_loop(
        0, cj, past_chunk,
        (jnp.concatenate(ms, axis=0), jnp.concatenate(ls, axis=0), jnp.concatenate(accs, axis=0)))
    outs = [_dot_nt(eye_ref[...], (acc_all[hh * dh:(hh + 1) * dh] / l_all[hh:hh + 1]).astype(BF16))
            for hh in range(hp)]
    o_ref[...] = jnp.concatenate(outs, axis=1).astype(BF16)


def _moba_prompt(q, k, v):
    b, h, s, dh = q.shape
    hp = MOBA_HEADS_PER_STEP
    n_blk = s // MOBA_BLOCK
    assert n_blk % MOBA_CHUNK_BLOCKS == 0 and h % hp == 0 and hp * dh == LANES
    n_chunks = n_blk // MOBA_CHUNK_BLOCKS
    kc = MOBA_CHUNK_BLOCKS * MOBA_BLOCK
    full = pl.BlockSpec((1, hp, s, dh), lambda bi, hi, j: (bi, hi, 0, 0))
    return pl.pallas_call(
        _moba_prompt_kernel,
        grid=(b, h // hp, n_blk),
        in_specs=[pl.BlockSpec((1, hp, MOBA_BLOCK, dh), lambda bi, hi, j: (bi, hi, j, 0)), full, full],
        out_specs=pl.BlockSpec((MOBA_BLOCK, hp * dh), lambda bi, hi, j: (bi * n_blk + j, hi)),
        out_shape=jax.ShapeDtypeStruct((b * s, h * dh), BF16),
        scratch_shapes=[pltpu.VMEM((hp, n_blk, dh), F32),
                        pltpu.VMEM((hp, n_chunks, kc, dh + n_blk), BF16),
                        pltpu.VMEM((hp, n_chunks, dh, kc), BF16),
                        pltpu.VMEM((MOBA_BLOCK, MOBA_BLOCK), BF16)],
        compiler_params=_params("parallel", "parallel", "arbitrary"),
        name="moba_prompt",
    )(q, k, v)


SELECT_PAGES_PER_STEP = 16


def _select_kernel(pt_ref, q_ref, ck_ref, o_ref, buf, pm_ref, sem, *, layer, pages_per_seq):
    n_grp = pl.num_programs(1)
    g = pl.program_id(1)
    step = pl.program_id(0) * n_grp + g
    n_steps = pl.num_programs(0) * n_grp
    slot = step % 2
    pps, n_heads, dh, ps = buf.shape[1], buf.shape[2], buf.shape[3], buf.shape[4]
    t = q_ref.shape[2]

    def page_copy(st, sl, i, pg):
        return pltpu.make_async_copy(ck_ref.at[layer, pg], buf.at[sl, i], sem.at[sl])

    def fetch(st, sl):
        base = (st // n_grp) * pages_per_seq + (st % n_grp) * pps
        for i in range(pps):
            page_copy(st, sl, i, pt_ref[base + i]).start()

    @pl.when(step == 0)
    def _():
        fetch(step, slot)

    @pl.when(step + 1 < n_steps)
    def _():
        fetch(step + 1, 1 - slot)

    for i in range(pps):
        page_copy(step, slot, i, 0).wait()

    diag = lax.broadcasted_iota(I32, (dh, ps), 0) == lax.broadcasted_iota(I32, (dh, ps), 1)
    x = buf[slot]
    r = jnp.sum(x, axis=-1, keepdims=True) * (1.0 / ps)
    pm = jnp.sum(jnp.where(diag, r, 0.0), axis=2)
    pm_ref[pl.ds(pl.multiple_of(g * pps * n_heads, pps * n_heads), pps * n_heads), :] = (
        pm.reshape(pps * n_heads, ps))

    @pl.when(g == n_grp - 1)
    def _():
        n_blk = pages_per_seq // PAGES_PER_BLOCK
        lane = lax.broadcasted_iota(I32, (t, n_blk), 1).astype(F32)
        for h in range(n_heads):
            bm = pm_ref[pl.ds(h, n_blk, stride=PAGES_PER_BLOCK * n_heads), :]
            for pi in range(1, PAGES_PER_BLOCK):
                bm = bm + pm_ref[pl.ds(pi * n_heads + h, n_blk, stride=PAGES_PER_BLOCK * n_heads), :]
            bm = bm * (1.0 / PAGES_PER_BLOCK)
            sc = _dot3(q_ref[0, h], bm[:, :dh], _dot_nt)
            res = jnp.zeros((t, n_blk), F32)
            for k in range(MOBA_TOPK):
                m = jnp.max(sc, axis=1, keepdims=True)
                idx = jnp.min(jnp.where(sc == m, lane, float(n_blk)), axis=1, keepdims=True)
                res = jnp.where(lane == k, idx, res)
                sc = jnp.where(lane == idx, -jnp.inf, sc)
            o_ref[0, h] = res.astype(I32)


def _select_blocks(q, cache_kt, page_table, layer):
    n, h, t, dh = q.shape
    ps = cache_kt.shape[-1]
    pages_per_seq = page_table.shape[1]
    pps = SELECT_PAGES_PER_STEP
    assert pages_per_seq % pps == 0 and pages_per_seq % PAGES_PER_BLOCK == 0 and dh <= ps
    n_blk = pages_per_seq // PAGES_PER_BLOCK
    return pl.pallas_call(
        functools.partial(_select_kernel, layer=layer, pages_per_seq=pages_per_seq),
        grid_spec=pltpu.PrefetchScalarGridSpec(
            num_scalar_prefetch=1,
            grid=(n, pages_per_seq // pps),
            in_specs=[pl.BlockSpec((1, h, t, dh), lambda i, g, pt: (i, 0, 0, 0)),
                      pl.BlockSpec(memory_space=pl.ANY)],
            out_specs=pl.BlockSpec((1, h, t, n_blk), lambda i, g, pt: (i, 0, 0, 0)),
            scratch_shapes=[pltpu.VMEM((2, pps, h, dh, ps), F32),
                            pltpu.VMEM((pages_per_seq * h, ps), F32),
                            pltpu.SemaphoreType.DMA((2,))]),
        out_shape=jax.ShapeDtypeStruct((n, h, t, n_blk), I32),
        compiler_params=_params("arbitrary", "arbitrary"),
        name="moba_select",
    )(page_table.reshape(-1), q, cache_kt)


def _moba_sample_kernel(sel_ref, pt_ref, q_ref, kn_ref, vn_ref, ck_ref, cv_ref, o_ref, kbuf, vbuf, sem, *,
                        layer, pages_per_seq):
    n_heads = pl.num_programs(1)
    step = pl.program_id(0) * n_heads + pl.program_id(1)
    n_steps = pl.num_programs(0) * n_heads
    t = q_ref.shape[2]
    n_pages = kbuf.shape[1]
    pages_per_q = n_pages // t
    slot = step % 2

    def page_copies(st, sl, i, pg):
        hh = st % n_heads
        return (pltpu.make_async_copy(ck_ref.at[layer, pg, hh], kbuf.at[sl, i], sem.at[0, sl]),
                pltpu.make_async_copy(cv_ref.at[layer, pg, hh], vbuf.at[sl, i], sem.at[1, sl]))

    def fetch(st, sl):
        pt_base = (st // n_heads) * pages_per_seq
        for b in range(n_pages // PAGES_PER_BLOCK):
            blk = sel_ref[st * (n_pages // PAGES_PER_BLOCK) + b]
            for pi in range(PAGES_PER_BLOCK):
                pg = pt_ref[pt_base + blk * PAGES_PER_BLOCK + pi]
                for cp in page_copies(st, sl, b * PAGES_PER_BLOCK + pi, pg):
                    cp.start()

    @pl.when(step == 0)
    def _():
        fetch(step, slot)

    @pl.when(step + 1 < n_steps)
    def _():
        fetch(step + 1, 1 - slot)

    for i in range(n_pages):
        for cp in page_copies(step, slot, i, 0):
            cp.wait()

    q = q_ref[0, 0]
    qb = q.astype(BF16)
    s = jnp.concatenate([_dot(qb, kbuf[slot, i].astype(BF16)) for i in range(n_pages)], axis=1)
    per_q = pages_per_q * PAGE_SIZE
    col = lax.broadcasted_iota(I32, s.shape, 1)
    lo = lax.broadcasted_iota(I32, s.shape, 0) * per_q
    own = (col >= lo) & (col < lo + per_q)
    sn = _dot_nt(q, kn_ref[0, 0])
    causal = lax.broadcasted_iota(I32, (t, t), 1) <= lax.broadcasted_iota(I32, (t, t), 0)
    m = jnp.maximum(jnp.max(jnp.where(own, s, NEG_BIG), axis=1, keepdims=True),
                    jnp.max(jnp.where(causal, sn, NEG_BIG), axis=1, keepdims=True))
    p = jnp.where(own, jnp.exp(s - m), 0.0).astype(BF16)
    pn = jnp.where(causal, jnp.exp(sn - m), 0.0)
    l = jnp.sum(p.astype(F32), axis=1, keepdims=True) + jnp.sum(pn, axis=1, keepdims=True)
    o = _dot(pn, vn_ref[0, 0])
    for i in range(n_pages):
        o = o + _dot_nt(p[:, i * PAGE_SIZE:(i + 1) * PAGE_SIZE], vbuf[slot, i].astype(BF16))
    o_ref[0, 0] = o / l


def _moba_sample(q, k_new, v_new, cache_kt, cache_vt, sel, page_table, layer):
    n, h, t, dh = q.shape
    n_pages = t * MOBA_TOPK * PAGES_PER_BLOCK
    blk = pl.BlockSpec((1, 1, t, dh), lambda i, j, sl, pt: (i, j, 0, 0))
    anyspec = pl.BlockSpec(memory_space=pl.ANY)
    return pl.pallas_call(
        functools.partial(_moba_sample_kernel, layer=layer, pages_per_seq=page_table.shape[1]),
        grid_spec=pltpu.PrefetchScalarGridSpec(
            num_scalar_prefetch=2,
            grid=(n, h),
            in_specs=[blk, blk, blk, anyspec, anyspec],
            out_specs=blk,
            scratch_shapes=[pltpu.VMEM((2, n_pages, dh, PAGE_SIZE), F32),
                            pltpu.VMEM((2, n_pages, dh, PAGE_SIZE), F32),
                            pltpu.SemaphoreType.DMA((2, 2))]),
        out_shape=jax.ShapeDtypeStruct((n, h, t, dh), F32),
        compiler_params=_params("arbitrary", "arbitrary"),
        name="moba_sample",
    )(sel.reshape(-1), page_table.reshape(-1), q, k_new, v_new, cache_kt, cache_vt)


def _merge_kernel(xp_ref, xs_ref, oap_ref, oas_ref, obp_ref, obs_ref, ga_ref, gb_ref, g1_ref, sh_ref, sc_ref,
                  nf_ref, wpa_ref, wpb_ref, wo_ref, wr_ref, br_ref, x1_ref, h2_ref, eid_ref, gw_ref, *, n_first):
    pa = _dot(_pick_rows(n_first, oap_ref, oas_ref).astype(BF16), wpa_ref[...])
    pb = _dot(_pick_rows(n_first, obp_ref, obs_ref), wpb_ref[...])
    mix = jax.nn.sigmoid(ga_ref[...]) * pa + jax.nn.sigmoid(gb_ref[...]) * pb
    x1 = _pick_rows(n_first, xp_ref, xs_ref) + g1_ref[0] * _dot(mix.astype(BF16), wo_ref[...])
    x1_ref[...] = x1
    h2 = x1 * lax.rsqrt(jnp.mean(x1 * x1, axis=-1, keepdims=True) + NORM_EPS) * nf_ref[...]
    h2 = h2 * (1.0 + sc_ref[0]) + sh_ref[0]
    h2_ref[...] = h2

    logits = _dot3(h2, wr_ref[...]) + br_ref[...]
    lane = lax.broadcasted_iota(I32, logits.shape, 1)
    lanef = lane.astype(F32)
    is_g = lane < N_GROUPS
    gl = jnp.where(is_g, logits, -jnp.inf)
    gmax = jnp.max(gl, axis=1, keepdims=True)
    g_sel = jnp.min(jnp.where(gl == gmax, lanef, float(LANES)), axis=1, keepdims=True)
    p_group = 1.0 / jnp.sum(jnp.exp(gl - gmax), axis=1, keepdims=True)
    e_lo = N_GROUPS + g_sel * EXPERTS_PER_GROUP
    in_grp = (lanef >= e_lo) & (lanef < e_lo + EXPERTS_PER_GROUP)
    el = jnp.where(in_grp, logits, -jnp.inf)
    m1 = jnp.max(el, axis=1, keepdims=True)
    i1 = jnp.min(jnp.where(el == m1, lanef, float(LANES)), axis=1, keepdims=True)
    el2 = jnp.where(lanef == i1, -jnp.inf, el)
    m2 = jnp.max(el2, axis=1, keepdims=True)
    i2 = jnp.min(jnp.where(el2 == m2, lanef, float(LANES)), axis=1, keepdims=True)
    e2 = jnp.exp(m2 - m1)
    w1 = p_group / (1.0 + e2)
    w2 = p_group * e2 / (1.0 + e2)
    eid = jnp.where(lane == 0, i1 - N_GROUPS, jnp.where(lane == 1, i2 - N_GROUPS, 0.0))
    eid_ref[...] = eid.astype(I32)
    gw_ref[...] = jnp.where(lane == 0, w1, jnp.where(lane == 1, w2, 0.0))


def _merge(xp, xs, oa_p, oa_s, ob_p, ob_s, zg, mod_tiles, norm_ffn, wpa, wpb, wo, w_router, b_router,
           tiles_per_seq):
    d = xp.shape[1]
    n_first = xp.shape[0] // ROW_TILE
    m = xp.shape[0] + xs.shape[0]
    n_mod = mod_tiles.shape[1]
    mod_map = lambda i: (jnp.minimum(i // tiles_per_seq, n_mod - 1), 0, 0)
    const = lambda i: (0, 0)
    row = lambda i: (i, 0)
    mod = pl.BlockSpec((1, ROW_TILE, d), mod_map)
    wspec = lambda w: pl.BlockSpec(w.shape, const, pipeline_mode=pl.Buffered(1))
    return pl.pallas_call(
        functools.partial(_merge_kernel, n_first=n_first),
        grid=(m // ROW_TILE,),
        in_specs=_two_group_specs(d, n_first) + _two_group_specs(A_WIDTH, n_first)
                 + _two_group_specs(B_WIDTH, n_first) + [
                  pl.BlockSpec((ROW_TILE, d), lambda i: (i, 0)),
                  pl.BlockSpec((ROW_TILE, d), lambda i: (i, 1)),
                  mod, mod, mod,
                  pl.BlockSpec((1, d), const),
                  wspec(wpa), wspec(wpb), wspec(wo), wspec(w_router),
                  pl.BlockSpec((1, LANES), const)],
        out_specs=[pl.BlockSpec((ROW_TILE, d), row),
                   pl.BlockSpec((ROW_TILE, d), row),
                   pl.BlockSpec((ROW_TILE, LANES), row),
                   pl.BlockSpec((ROW_TILE, LANES), row)],
        out_shape=[jax.ShapeDtypeStruct((m, d), F32),
                   jax.ShapeDtypeStruct((m, d), F32),
                   jax.ShapeDtypeStruct((m, LANES), I32),
                   jax.ShapeDtypeStruct((m, LANES), F32)],
        compiler_params=_params("parallel"),
        name="merge_router",
    )(xp, xs, oa_p, oa_s, ob_p, ob_s, zg, zg, mod_tiles[2], mod_tiles[3], mod_tiles[4],
      norm_ffn.reshape(1, d), wpa, wpb, wo, w_router, b_router)


def _moe_kernel(be_ref, nu_ref, x_ref, wg_ref, wu_ref, wd_ref, y_ref, wg_s, wu_s, wd_s):
    i = pl.program_id(0)
    prev = be_ref[jnp.maximum(i - 1, 0)]
    fresh = (i == 0) | (be_ref[i] != prev)

    @pl.when(fresh)
    def _():
        wg_s[...] = wg_ref[...].astype(BF16)
        wu_s[...] = wu_ref[...].astype(BF16)
        wd_s[...] = wd_ref[...].astype(BF16)

    @pl.when(i < nu_ref[0])
    def _():
        xb = x_ref[...].astype(BF16)
        hid = _silu(_dot(xb, wg_s[...])) * _dot(xb, wu_s[...])
        y_ref[...] = _dot(hid.astype(BF16), wd_s[...])

    @pl.when(i >= nu_ref[0])
    def _():
        y_ref[...] = jnp.zeros(y_ref.shape, F32)


def _moe(xs, block_expert, n_used, w_gate, w_up, w_down, layer):
    n_rows, d = xs.shape
    de = w_gate.shape[-1]
    return pl.pallas_call(
        _moe_kernel,
        grid_spec=pltpu.PrefetchScalarGridSpec(
            num_scalar_prefetch=2,
            grid=(n_rows // MOE_TILE,),
            in_specs=[pl.BlockSpec((MOE_TILE, d), lambda i, be, nu: (i, 0)),
                      pl.BlockSpec((None, None, d, de), lambda i, be, nu: (layer, be[i], 0, 0)),
                      pl.BlockSpec((None, None, d, de), lambda i, be, nu: (layer, be[i], 0, 0)),
                      pl.BlockSpec((None, None, de, d), lambda i, be, nu: (layer, be[i], 0, 0))],
            out_specs=pl.BlockSpec((MOE_TILE, d), lambda i, be, nu: (i, 0)),
            scratch_shapes=[pltpu.VMEM((d, de), BF16), pltpu.VMEM((d, de), BF16), pltpu.VMEM((de, d), BF16)]),
        out_shape=jax.ShapeDtypeStruct((n_rows, d), F32),
        compiler_params=_params("arbitrary"),
        name="moe_experts",
    )(block_expert, n_used, xs, w_gate, w_up, w_down)


def _final_kernel(x1_ref, y0_ref, y1_ref, gw_ref, g2_ref, nf_ref, op_ref, os_ref, *, last, n_first):
    gw = gw_ref[...]
    f = gw[:, 0:1] * y0_ref[...] + gw[:, 1:2] * y1_ref[...]
    x2 = x1_ref[...] + g2_ref[0] * f
    if last:
        x2 = x2 * lax.rsqrt(jnp.mean(x2 * x2, axis=-1, keepdims=True) + NORM_EPS) * nf_ref[...]
    i = pl.program_id(0)

    @pl.when(i < n_first)
    def _():
        op_ref[...] = x2

    @pl.when(i >= n_first)
    def _():
        os_ref[...] = x2


def _final(x1, y0, y1, gw, mod_tiles, norm_final, tiles_per_seq, last, n_first):
    m, d = x1.shape
    n_mod = mod_tiles.shape[1]
    mod_map = lambda i: (jnp.minimum(i // tiles_per_seq, n_mod - 1), 0, 0)
    row = lambda i: (i, 0)
    rs = pl.BlockSpec((ROW_TILE, d), row)
    return pl.pallas_call(
        functools.partial(_final_kernel, last=last, n_first=n_first),
        grid=(m // ROW_TILE,),
        in_specs=[rs, rs, rs, pl.BlockSpec((ROW_TILE, LANES), row),
                  pl.BlockSpec((1, ROW_TILE, d), mod_map),
                  pl.BlockSpec((1, d), lambda i: (0, 0))],
        out_specs=_two_group_specs(d, n_first),
        out_shape=[jax.ShapeDtypeStruct((n_first * ROW_TILE, d), F32),
                   jax.ShapeDtypeStruct((m - n_first * ROW_TILE, d), F32)],
        compiler_params=_params("arbitrary"),
        name="combine_final",
    )(x1, y0, y1, gw, mod_tiles[5], norm_final.reshape(1, d))


def _dispatch(eid):
    m = eid.shape[0]
    n_assign = m * EXPERT_TOPK
    flat_e = eid.reshape(-1)
    flat_tok = jnp.repeat(jnp.arange(m, dtype=I32), EXPERT_TOPK)
    onehot = (flat_e[:, None] == jnp.arange(N_EXPERTS, dtype=I32)[None, :]).astype(I32)
    csum = jnp.cumsum(onehot, axis=0)
    rank = jnp.take_along_axis(csum, flat_e[:, None], axis=1)[:, 0] - 1
    counts = csum[-1]
    padded = (counts + MOE_TILE - 1) // MOE_TILE * MOE_TILE
    pad_end = jnp.cumsum(padded)
    pad_start = pad_end - padded
    dest = (pad_start[flat_e] + rank).astype(I32)
    n_blocks = -(-(n_assign + N_EXPERTS * (MOE_TILE - 1)) // MOE_TILE)
    row_tok = jnp.zeros((n_blocks * MOE_TILE,), I32).at[dest].set(flat_tok)
    blk_start = jnp.arange(n_blocks, dtype=I32) * MOE_TILE
    block_expert = jnp.minimum(jnp.sum((pad_end[None, :] <= blk_start[:, None]).astype(I32), axis=1),
                               N_EXPERTS - 1).astype(I32)
    n_used = (pad_end[-1:] // MOE_TILE).astype(I32)
    return dest.reshape(m, EXPERT_TOPK), row_tok, block_expert, n_used


def _mod_tiles(mod, n_prompt, reps):
    d = mod.shape[-1]
    mp = jnp.broadcast_to(mod[:n_prompt, :, None, :], (n_prompt, 6, ROW_TILE, d))
    ms = jnp.repeat(mod[n_prompt:], reps, axis=0).transpose(1, 0, 2)[None]
    return jnp.concatenate([mp, ms], axis=0).transpose(1, 0, 2, 3)


def kernel(x_prompt, x_sample, c_prompt, c_sample, cache_k, cache_v, state_hgrn, page_table,
           w_ada, b_ada, norm_mix, norm_ffn, w_in, hgrn_lb_logits, hgrn_norm,
           w_proj_a, w_proj_b, w_out, w_group, b_group, w_expert_router, b_expert_router,
           w_gate, w_up, w_down, norm_final):
    bp, tp, d = x_prompt.shape
    bs, ts, _ = x_sample.shape
    mp_rows, ms_rows = bp * tp, bs * ts
    assert ms_rows == ROW_TILE and tp % ROW_TILE == 0
    m = mp_rows + ms_rows
    tiles_per_seq = tp // ROW_TILE
    n_full = PAST_LEN // MOBA_BLOCK
    assert PAST_LEN % MOBA_BLOCK == 0 and n_full >= MOBA_TOPK

    xp, xs = x_prompt.reshape(mp_rows, d), x_sample.reshape(ms_rows, d)
    c_all = jnp.concatenate([c_prompt, c_sample, jnp.zeros((-(bp + bs) % 8, d), F32)], axis=0)
    lb_all = jnp.cumsum(jax.nn.softmax(hgrn_lb_logits.astype(F32), axis=0), axis=0)
    cache_kt = cache_k.transpose(0, 1, 2, 4, 3)
    cache_vt = cache_v.transpose(0, 1, 2, 4, 3)
    kp_l, vp_l, sp_l, ks_l, vs_l, ss_l = [], [], [], [], [], []

    for l in range(DEPTH):
        mod = _modulation(c_all, w_ada[l], b_ada[l])[:bp + bs].reshape(bp + bs, 6, d)
        mt = _mod_tiles(mod, bp, ts)
        za, zb, zg = _inproj(xp, xs, mt, norm_mix[l], w_in[l], tiles_per_seq)

        oa_p, st_p = _hgrn(za, lb_all[l], hgrn_norm[l], 0, bp, tp, ROW_TILE, HGRN_CHUNK)
        oa_s, st_s = _hgrn(za, lb_all[l], hgrn_norm[l], mp_rows, bs, ts, ts, ts, s0=state_hgrn[l])

        q_p, k_p, v_p = _rope_split(zb, 0, bp, tp, 512, jnp.arange(tp))
        q_s, k_s, v_s = _rope_split(zb, mp_rows, bs, ts, ts, PAST_LEN + jnp.arange(ts))
        ob_p = _moba_prompt(q_p, k_p, v_p)
        sel = _select_blocks(q_s, cache_kt, page_table[:, :n_full * PAGES_PER_BLOCK], l)[..., :MOBA_TOPK]
        ob_s = _moba_sample(q_s, k_s, v_s, cache_kt, cache_vt, sel, page_table, l)
        ob_s = ob_s.transpose(0, 2, 1, 3).reshape(ms_rows, B_WIDTH).astype(BF16)

        w_router = jnp.concatenate(
            [w_group[l], w_expert_router[l].transpose(1, 0, 2).reshape(d, N_EXPERTS),
             jnp.zeros((d, LANES - N_GROUPS - N_EXPERTS), F32)], axis=1)
        b_router = jnp.concatenate(
            [b_group[l], b_expert_router[l].reshape(-1),
             jnp.zeros((LANES - N_GROUPS - N_EXPERTS,), F32)]).reshape(1, LANES)
        x1, h2, eid, gw = _merge(xp, xs, oa_p, oa_s, ob_p, ob_s, zg, mt, norm_ffn[l], w_proj_a[l].astype(BF16),
                                 w_proj_b[l].astype(BF16), w_out[l].astype(BF16), w_router, b_router,
                                 tiles_per_seq)

        dest, row_tok, block_expert, n_used = _dispatch(eid[:, :EXPERT_TOPK])
        ys = _moe(h2[row_tok], block_expert, n_used, w_gate, w_up, w_down, l)
        xp, xs = _final(x1, ys[dest[:, 0]], ys[dest[:, 1]], gw, mt, norm_final, tiles_per_seq,
                        l == DEPTH - 1, mp_rows // ROW_TILE)

        kp_l.append(k_p); vp_l.append(v_p); sp_l.append(st_p)
        ks_l.append(k_s); vs_l.append(v_s); ss_l.append(st_s)

    y_prompt = xp.reshape(bp, tp, d)
    y_sample = xs.reshape(bs, ts, d)
    return (y_prompt, y_sample, jnp.stack(kp_l), jnp.stack(vp_l), jnp.stack(sp_l),
            jnp.stack(ks_l), jnp.stack(vs_l), jnp.stack(ss_l))
```

```python
import functools

import numpy as np
import jax
import jax.numpy as jnp
from jax import lax
from jax.experimental import pallas as pl
from jax.experimental.pallas import tpu as pltpu

F32 = jnp.float32
BF16 = jnp.bfloat16
I32 = jnp.int32

D_MODEL = 1024
DEPTH = 1
PAST_LEN = 16384
PAGE_SIZE = 128
A_HEADS = 4
A_KDIM = 128
A_VDIM = 128
A_WIDTH = A_HEADS * A_KDIM
B_HEADS = 8
B_HEAD_DIM = 64
B_WIDTH = B_HEADS * B_HEAD_DIM
MOBA_BLOCK = 256
MOBA_TOPK = 3
ROT_DIM = B_HEAD_DIM // 4
ROPE_THETA = 500000.0
N_GROUPS = 4
EXPERTS_PER_GROUP = 8
N_EXPERTS = N_GROUPS * EXPERTS_PER_GROUP
EXPERT_TOPK = 2
D_EXPERT = D_MODEL // 2
NORM_EPS = 1e-6

PAGES_PER_BLOCK = MOBA_BLOCK // PAGE_SIZE
ROW_TILE = 256
HGRN_CHUNK = 16
MOE_TILE = 256
LANES = 128
NEG_BIG = -1e30
LOG2_E = 1.4426950408889634
VMEM_LIMIT_BYTES = 52 * 1024 * 1024


def _params(*sem):
    return pltpu.CompilerParams(dimension_semantics=sem, vmem_limit_bytes=VMEM_LIMIT_BYTES)


def _dot(a, b):
    return jnp.dot(a, b, preferred_element_type=F32)


def _dot_nt(a, b):
    return lax.dot_general(a, b, (((1,), (1,)), ((), ())), preferred_element_type=F32)


def _dot_tn(a, b):
    return lax.dot_general(a, b, (((0,), (0,)), ((), ())), preferred_element_type=F32)


def _split(a):
    hi = a.astype(BF16)
    return hi, (a - hi.astype(F32)).astype(BF16)


def _dot3(a, b, dot=_dot):
    ah, al = _split(a)
    bh, bl = _split(b)
    return dot(ah, bh) + (dot(ah, bl) + dot(al, bh))


def _silu(x):
    return x * jax.nn.sigmoid(x)


def _mod_kernel(c_ref, w_ref, b_ref, o_ref):
    o_ref[...] = _dot3(_silu(c_ref[...]), w_ref[...]) + b_ref[...]


def _modulation(c_all, w, b):
    n = c_all.shape[0]
    d, dout = w.shape
    return pl.pallas_call(
        _mod_kernel,
        grid=(dout // d,),
        in_specs=[pl.BlockSpec((n, d), lambda j: (0, 0)),
                  pl.BlockSpec((d, d), lambda j: (0, j)),
                  pl.BlockSpec((1, d), lambda j: (0, j))],
        out_specs=pl.BlockSpec((n, d), lambda j: (0, j)),
        out_shape=jax.ShapeDtypeStruct((n, dout), F32),
        compiler_params=_params("parallel"),
        name="modulation",
    )(c_all, w, b.reshape(1, dout))


def _pick_rows(n_first, first_ref, second_ref):
    return jnp.where(pl.program_id(0) < n_first, first_ref[...], second_ref[...])


def _two_group_specs(width, n_first):
    return [pl.BlockSpec((ROW_TILE, width), lambda i: (jnp.minimum(i, n_first - 1), 0)),
            pl.BlockSpec((ROW_TILE, width), lambda i: (0, 0))]


def _inproj_kernel(xp_ref, xs_ref, sh_ref, sc_ref, g_ref, w_ref, wlo_ref, za_ref, zb_ref, zg_ref, *, n_first):
    x = _pick_rows(n_first, xp_ref, xs_ref)
    h = x * lax.rsqrt(jnp.mean(x * x, axis=-1, keepdims=True) + NORM_EPS) * g_ref[...]
    h = h * (1.0 + sc_ref[0]) + sh_ref[0]
    hh, hl = _split(h)
    wa = 4 * A_WIDTH
    for c in range(0, wa, 512):
        za_ref[:, c:c + 512] = _dot(hh, w_ref[:, c:c + 512])
    for c in range(0, 2 * B_WIDTH, 512):
        wc = w_ref[:, wa + c:wa + c + 512]
        zb_ref[:, c:c + 512] = _dot(hh, wc) + (_dot(hh, wlo_ref[:, c:c + 512]) + _dot(hl, wc))
    c = 2 * B_WIDTH
    zb_ref[:, c:c + 512] = _dot(hh, w_ref[:, wa + c:wa + c + 512])
    wg = wa + 3 * B_WIDTH
    for c in range(0, 2 * D_MODEL, 512):
        zg_ref[:, c:c + 512] = _dot(hh, w_ref[:, wg + c:wg + c + 512])


def _inproj(xp, xs, mod_tiles, norm_g, w_in, tiles_per_seq):
    d = xp.shape[1]
    n_first = xp.shape[0] // ROW_TILE
    m = xp.shape[0] + xs.shape[0]
    n_mod = mod_tiles.shape[1]
    w_hi = w_in.astype(BF16)
    wa = 4 * A_WIDTH
    wqk = w_in[:, wa:wa + 2 * B_WIDTH]
    w_lo = (wqk - wqk.astype(BF16).astype(F32)).astype(BF16)
    mod_map = lambda i: (jnp.minimum(i // tiles_per_seq, n_mod - 1), 0, 0)
    const = lambda i: (0, 0)
    row = lambda i: (i, 0)
    return pl.pallas_call(
        functools.partial(_inproj_kernel, n_first=n_first),
        grid=(m // ROW_TILE,),
        in_specs=_two_group_specs(d, n_first) + [
                  pl.BlockSpec((1, ROW_TILE, d), mod_map),
                  pl.BlockSpec((1, ROW_TILE, d), mod_map),
                  pl.BlockSpec((1, d), const),
                  pl.BlockSpec(w_hi.shape, const, pipeline_mode=pl.Buffered(1)),
                  pl.BlockSpec(w_lo.shape, const, pipeline_mode=pl.Buffered(1))],
        out_specs=[pl.BlockSpec((ROW_TILE, wa), row),
                   pl.BlockSpec((ROW_TILE, 3 * B_WIDTH), row),
                   pl.BlockSpec((ROW_TILE, 2 * D_MODEL), row)],
        out_shape=[jax.ShapeDtypeStruct((m, wa), F32),
                   jax.ShapeDtypeStruct((m, 3 * B_WIDTH), F32),
                   jax.ShapeDtypeStruct((m, 2 * D_MODEL), F32)],
        compiler_params=_params("parallel"),
        name="inproj",
    )(xp, xs, mod_tiles[0], mod_tiles[1], norm_g.reshape(1, d), w_hi, w_lo)


def _rope_kernel(q_ref, k_ref, v_ref, c_ref, s1_ref, s2_ref, qo_ref, ko_ref, vo_ref):
    cos, s1, s2 = c_ref[...], s1_ref[...], s2_ref[...]
    half = ROT_DIM // 2

    def rope(x):
        up = pltpu.roll(x, B_WIDTH - half, axis=1)
        dn = pltpu.roll(x, half, axis=1)
        return x * cos + up * s1 + dn * s2

    q = rope(q_ref[...]) * (B_HEAD_DIM ** -0.5)
    k = rope(k_ref[...])
    v = v_ref[...]
    for h in range(B_HEADS):
        ls = slice(h * B_HEAD_DIM, (h + 1) * B_HEAD_DIM)
        qo_ref[0, h] = q[:, ls]
        ko_ref[0, h] = k[:, ls]
        vo_ref[0, h] = v[:, ls]


def _rope_tables(pos):
    half = ROT_DIM // 2
    inv = jnp.power(ROPE_THETA, -(jnp.arange(half, dtype=F32) * 2.0 / ROT_DIM))
    ang = pos.astype(F32)[:, None] * inv[None, :]
    cos, sin = jnp.cos(ang), jnp.sin(ang)
    t = pos.shape[0]
    rest = B_HEAD_DIM - ROT_DIM
    c = jnp.concatenate([cos, cos, jnp.ones((t, rest), F32)], axis=-1)
    s1 = jnp.concatenate([-sin, jnp.zeros((t, half + rest), F32)], axis=-1)
    s2 = jnp.concatenate([jnp.zeros((t, half), F32), sin, jnp.zeros((t, rest), F32)], axis=-1)
    return [jnp.tile(a, (1, B_HEADS)) for a in (c, s1, s2)]


def _rope_split(zb, row0, n, t, tile, pos):
    tabs = _rope_tables(pos)
    tps = t // tile
    blk0 = row0 // tile
    zmap = lambda c: (lambda b, s: (blk0 + b * tps + s, c))
    tmap = lambda b, s: (s, 0)
    omap = lambda b, s: (b, 0, s, 0)
    oshape = jax.ShapeDtypeStruct((n, B_HEADS, t, B_HEAD_DIM), F32)
    ospec = pl.BlockSpec((1, B_HEADS, tile, B_HEAD_DIM), omap)
    return pl.pallas_call(
        _rope_kernel,
        grid=(n, tps),
        in_specs=[pl.BlockSpec((tile, B_WIDTH), zmap(0)),
                  pl.BlockSpec((tile, B_WIDTH), zmap(1)),
                  pl.BlockSpec((tile, B_WIDTH), zmap(2)),
                  pl.BlockSpec((tile, B_WIDTH), tmap),
                  pl.BlockSpec((tile, B_WIDTH), tmap),
                  pl.BlockSpec((tile, B_WIDTH), tmap)],
        out_specs=[ospec, ospec, ospec],
        out_shape=[oshape, oshape, oshape],
        compiler_params=_params("parallel", "parallel"),
        name="rope_split",
    )(zb, zb, zb, *tabs)


def _hgrn_kernel(*refs, chunk, has_s0):
    aq_ref, af_ref, ai_ref, ag_ref, lb_ref, gain_ref = refs[:6]
    rest = refs[6:]
    s0_ref = None
    if has_s0:
        s0_ref, rest = rest[0], rest[1:]
    o_ref, so_ref, st_ref, q_s, b_s, k_s = rest
    t = pl.program_id(1)
    tb = aq_ref.shape[0]

    @pl.when(t == 0)
    def _():
        for h in range(A_HEADS):
            if has_s0:
                st_ref[h] = s0_ref[0, h].T
            else:
                st_ref[h] = jnp.zeros((A_VDIM, A_KDIM), F32)

    lb = lb_ref[...]
    f = lb + (1.0 - lb) * jax.nn.sigmoid(af_ref[...])
    logf = jnp.log(f)
    q_s[...] = _silu(aq_ref[...])
    k_s[...] = 1.0 - f
    row = lax.broadcasted_iota(I32, logf.shape, 0) & (chunk - 1)
    b = logf
    sh = 1
    while sh < chunk:
        b = b + jnp.where(row >= sh, pltpu.roll(b, sh, axis=0), 0.0)
        sh *= 2
    b_s[...] = b
    rowc = lax.broadcasted_iota(I32, (chunk, A_KDIM), 0)

    def one_chunk(ci, carry):
        r0 = pl.multiple_of(ci * chunk, chunk)
        rs = pl.ds(r0, chunk)
        for h in range(A_HEADS):
            ls = slice(h * A_KDIM, (h + 1) * A_KDIM)
            qc, bc, kc, vc = q_s[rs, ls], b_s[rs, ls], k_s[rs, ls], ai_ref[rs, ls]
            st = st_ref[h]
            bl = bc[chunk - 1:chunk, :]
            o = _dot_nt((qc * jnp.exp(bc)).astype(BF16), st.astype(BF16))
            grp = 8
            parts = [o[g * grp:(g + 1) * grp] for g in range(chunk // grp)]
            for s in range(chunk):
                for g in range(s // grp, chunk // grp):
                    rows = slice(g * grp, (g + 1) * grp)
                    diff = bc[rows] - bc[s:s + 1, :]
                    if g == s // grp:
                        diff = jnp.where(rowc[rows] >= s, diff, -jnp.inf)
                    e = jnp.exp(diff)
                    r = jnp.sum(qc[rows] * e * kc[s:s + 1, :], axis=1, keepdims=True)
                    parts[g] = parts[g] + r * vc[s:s + 1, :]
            o = jnp.concatenate(parts, axis=0) if len(parts) > 1 else parts[0]
            kp = kc * jnp.exp(bl - bc)
            st_ref[h] = st * jnp.exp(bl) + _dot_tn(vc.astype(BF16), kp.astype(BF16))
            o_ref[rs, ls] = o
        return carry

    n_chunks = tb // chunk
    lax.fori_loop(0, n_chunks, one_chunk, 0, unroll=4 if n_chunks % 4 == 0 else 1)

    gain = gain_ref[...]
    for h in range(A_HEADS):
        ls = slice(h * A_VDIM, (h + 1) * A_VDIM)
        oh = o_ref[:, ls]
        y = oh * lax.rsqrt(jnp.mean(oh * oh, axis=-1, keepdims=True) + NORM_EPS) * gain
        o_ref[:, ls] = y * _silu(ag_ref[:, ls])

    @pl.when(t == pl.num_programs(1) - 1)
    def _():
        for h in range(A_HEADS):
            so_ref[0, h] = st_ref[h].T


def _hgrn(za, lb, gain, row0, n, t, tile, chunk, s0=None):
    tps = t // tile
    blk0 = row0 // tile
    zmap = lambda c: (lambda b, s: (blk0 + b * tps + s, c))
    const = lambda b, s: (0, 0)
    in_specs = [pl.BlockSpec((tile, A_WIDTH), zmap(c)) for c in range(4)]
    in_specs += [pl.BlockSpec((1, A_WIDTH), const), pl.BlockSpec((1, A_VDIM), const)]
    args = [za, za, za, za, lb.reshape(1, A_WIDTH), gain.reshape(1, A_VDIM)]
    if s0 is not None:
        in_specs.append(pl.BlockSpec((1, A_HEADS, A_KDIM, A_VDIM), lambda b, s: (b, 0, 0, 0)))
        args.append(s0)
    return pl.pallas_call(
        functools.partial(_hgrn_kernel, chunk=chunk, has_s0=s0 is not None),
        grid=(n, tps),
        in_specs=in_specs,
        out_specs=[pl.BlockSpec((tile, A_WIDTH), lambda b, s: (b * tps + s, 0)),
                   pl.BlockSpec((1, A_HEADS, A_KDIM, A_VDIM), lambda b, s: (b, 0, 0, 0))],
        out_shape=[jax.ShapeDtypeStruct((n * t, A_WIDTH), F32),
                   jax.ShapeDtypeStruct((n, A_HEADS, A_KDIM, A_VDIM), F32)],
        scratch_shapes=[pltpu.VMEM((A_HEADS, A_VDIM, A_KDIM), F32),
                        pltpu.VMEM((tile, A_WIDTH), F32),
                        pltpu.VMEM((tile, A_WIDTH), F32),
                        pltpu.VMEM((tile, A_WIDTH), F32)],
        compiler_params=_params("parallel", "arbitrary"),
        name="hgrn2",
    )(*args)


MOBA_CHUNK_BLOCKS = 4
MOBA_HEADS_PER_STEP = 2


def _moba_prompt_kernel(q_ref, k_ref, v_ref, o_ref, km_ref, ka_ref, vt_ref, eye_ref):
    j = pl.program_id(2)
    blk = MOBA_BLOCK
    hp, dh = q_ref.shape[1], q_ref.shape[3]
    n_chunks, kc = ka_ref.shape[0], ka_ref.shape[1]
    n_blk = n_chunks * MOBA_CHUNK_BLOCKS
    eye = (lax.broadcasted_iota(I32, (dh, dh), 0) == lax.broadcasted_iota(I32, (dh, dh), 1)).astype(BF16)

    @pl.when(j == 0)
    def _():
        eye_ref[...] = (lax.broadcasted_iota(I32, (blk, blk), 0)
                        == lax.broadcasted_iota(I32, (blk, blk), 1)).astype(BF16)
        for c in range(n_chunks):
            rows = slice(c * kc, (c + 1) * kc)
            key_blk = lax.broadcasted_iota(I32, (kc, n_blk), 0) // blk + c * MOBA_CHUNK_BLOCKS
            onehot = key_blk == lax.broadcasted_iota(I32, (kc, n_blk), 1)
            ka_ref[c, :, hp * dh:] = jnp.where(onehot, 1.0, 0.0).astype(BF16)
            for hh in range(hp):
                ka_ref[c, :, hh * dh:(hh + 1) * dh] = k_ref[0, hh, rows, :].astype(BF16)
                vt_ref[hh, c] = _dot_nt(eye, v_ref[0, hh, rows, :].astype(BF16)).astype(BF16)
        for hh in range(hp):
            for i in range(n_blk):
                km_ref[hh, i:i + 1, :] = jnp.mean(k_ref[0, hh, i * blk:(i + 1) * blk, :], axis=0, keepdims=True)

    bidx = lax.broadcasted_iota(I32, (n_blk, blk), 0)
    valid = bidx < j
    q_rows, biases = [], []
    for hh in range(hp):
        q = q_ref[0, hh]
        sc = jnp.where(valid, _dot3(km_ref[hh], q, _dot_nt), -jnp.inf)
        rank = jnp.zeros((n_blk, blk), I32)
        for i in range(n_blk - 1):
            row = sc[i:i + 1, :]
            rank = rank + jnp.where(row > sc, 1, jnp.where(row == sc, jnp.where(bidx > i, 1, 0), 0))
        biases.append(jnp.where(valid, jnp.where(rank < MOBA_TOPK, 0.0, NEG_BIG),
                                jnp.where(bidx == j, 0.0, NEG_BIG)))
        q_t = _dot_nt(eye, (q * LOG2_E).astype(BF16))
        zero = jnp.zeros((dh, blk), F32)
        q_rows.append(jnp.concatenate([q_t if g == hh else zero for g in range(hp)], axis=1))
    qa = jnp.concatenate(q_rows + [jnp.concatenate(biases, axis=1)], axis=0).astype(BF16)

    def scores(c):
        return jnp.concatenate([_dot(ka_ref[c, :kc // 2], qa), _dot(ka_ref[c, kc // 2:], qa)], axis=0)

    def weighted_values(c, p):
        pb = p.astype(BF16)
        return jnp.concatenate([_dot(vt_ref[hh, c], pb[:, hh * blk:(hh + 1) * blk]) for hh in range(hp)], axis=1)

    cj = j // MOBA_CHUNK_BLOCKS
    key_pos = lax.broadcasted_iota(I32, (kc, hp * blk), 0) + cj * kc
    q_pos = (lax.broadcasted_iota(I32, (kc, hp * blk), 1) & (blk - 1)) + j * blk
    s = jnp.where(key_pos <= q_pos, scores(cj), NEG_BIG)
    m = jnp.max(s, axis=0, keepdims=True)
    p = jnp.exp2(s - m)
    init = (m, jnp.sum(p, axis=0, keepdims=True), weighted_values(cj, p))

    def past_chunk(c, carry):
        m, l, acc = carry
        s = scores(c)
        m_new = jnp.maximum(m, jnp.max(s, axis=0, keepdims=True))
        p = jnp.exp2(s - m_new)
        alpha = jnp.exp2(m - m_new)
        return m_new, alpha * l + jnp.sum(p, axis=0, keepdims=True), alpha * acc + weighted_values(c, p)

    _, l, acc = lax.fori_loop(0, cj, past_chunk, init)
    o_t = (acc / l).astype(BF16)
    o_ref[...] = jnp.concatenate([_dot_nt(eye_ref[...], o_t[:, hh * blk:(hh + 1) * blk]) for hh in range(hp)],
                                 axis=1).astype(BF16)


def _moba_prompt(q, k, v):
    b, h, s, dh = q.shape
    hp = MOBA_HEADS_PER_STEP
    n_blk = s // MOBA_BLOCK
    assert n_blk % MOBA_CHUNK_BLOCKS == 0 and h % hp == 0 and hp * dh == LANES
    n_chunks = n_blk // MOBA_CHUNK_BLOCKS
    kc = MOBA_CHUNK_BLOCKS * MOBA_BLOCK
    full = pl.BlockSpec((1, hp, s, dh), lambda bi, hi, j: (bi, hi, 0, 0))
    return pl.pallas_call(
        _moba_prompt_kernel,
        grid=(b, h // hp, n_blk),
        in_specs=[pl.BlockSpec((1, hp, MOBA_BLOCK, dh), lambda bi, hi, j: (bi, hi, j, 0)), full, full],
        out_specs=pl.BlockSpec((MOBA_BLOCK, hp * dh), lambda bi, hi, j: (bi * n_blk + j, hi)),
        out_shape=jax.ShapeDtypeStruct((b * s, h * dh), BF16),
        scratch_shapes=[pltpu.VMEM((hp, n_blk, dh), F32),
                        pltpu.VMEM((n_chunks, kc, hp * dh + n_blk), BF16),
                        pltpu.VMEM((hp, n_chunks, dh, kc), BF16),
                        pltpu.VMEM((MOBA_BLOCK, MOBA_BLOCK), BF16)],
        compiler_params=_params("parallel", "parallel", "arbitrary"),
        name="moba_prompt",
    )(q, k, v)


SELECT_PAGES_PER_STEP = 16


def _select_kernel(pt_ref, q_ref, ck_ref, o_ref, buf, pm_ref, sem, *, layer, pages_per_seq):
    n_grp = pl.num_programs(1)
    g = pl.program_id(1)
    step = pl.program_id(0) * n_grp + g
    n_steps = pl.num_programs(0) * n_grp
    slot = step % 2
    pps, n_heads, dh, ps = buf.shape[1], buf.shape[2], buf.shape[3], buf.shape[4]
    t = q_ref.shape[2]

    def page_copy(st, sl, i, pg):
        return pltpu.make_async_copy(ck_ref.at[layer, pg], buf.at[sl, i], sem.at[sl])

    def fetch(st, sl):
        base = (st // n_grp) * pages_per_seq + (st % n_grp) * pps
        for i in range(pps):
            page_copy(st, sl, i, pt_ref[base + i]).start()

    @pl.when(step == 0)
    def _():
        fetch(step, slot)

    @pl.when(step + 1 < n_steps)
    def _():
        fetch(step + 1, 1 - slot)

    for i in range(pps):
        page_copy(step, slot, i, 0).wait()

    diag = lax.broadcasted_iota(I32, (dh, ps), 0) == lax.broadcasted_iota(I32, (dh, ps), 1)
    x = buf[slot]
    r = jnp.sum(x, axis=-1, keepdims=True) * (1.0 / ps)
    pm = jnp.sum(jnp.where(diag, r, 0.0), axis=2)
    pm_ref[pl.ds(pl.multiple_of(g * pps * n_heads, pps * n_heads), pps * n_heads), :] = (
        pm.reshape(pps * n_heads, ps))

    @pl.when(g == n_grp - 1)
    def _():
        n_blk = pages_per_seq // PAGES_PER_BLOCK
        lane = lax.broadcasted_iota(I32, (t, n_blk), 1).astype(F32)
        for h in range(n_heads):
            bm = pm_ref[pl.ds(h, n_blk, stride=PAGES_PER_BLOCK * n_heads), :]
            for pi in range(1, PAGES_PER_BLOCK):
                bm = bm + pm_ref[pl.ds(pi * n_heads + h, n_blk, stride=PAGES_PER_BLOCK * n_heads), :]
            bm = bm * (1.0 / PAGES_PER_BLOCK)
            sc = _dot3(q_ref[0, h], bm[:, :dh], _dot_nt)
            res = jnp.zeros((t, n_blk), F32)
            for k in range(MOBA_TOPK):
                m = jnp.max(sc, axis=1, keepdims=True)
                idx = jnp.min(jnp.where(sc == m, lane, float(n_blk)), axis=1, keepdims=True)
                res = jnp.where(lane == k, idx, res)
                sc = jnp.where(lane == idx, -jnp.inf, sc)
            o_ref[0, h] = res.astype(I32)


def _select_blocks(q, cache_kt, page_table, layer):
    n, h, t, dh = q.shape
    ps = cache_kt.shape[-1]
    pages_per_seq = page_table.shape[1]
    pps = SELECT_PAGES_PER_STEP
    assert pages_per_seq % pps == 0 and pages_per_seq % PAGES_PER_BLOCK == 0 and dh <= ps
    n_blk = pages_per_seq // PAGES_PER_BLOCK
    return pl.pallas_call(
        functools.partial(_select_kernel, layer=layer, pages_per_seq=pages_per_seq),
        grid_spec=pltpu.PrefetchScalarGridSpec(
            num_scalar_prefetch=1,
            grid=(n, pages_per_seq // pps),
            in_specs=[pl.BlockSpec((1, h, t, dh), lambda i, g, pt: (i, 0, 0, 0)),
                      pl.BlockSpec(memory_space=pl.ANY)],
            out_specs=pl.BlockSpec((1, h, t, n_blk), lambda i, g, pt: (i, 0, 0, 0)),
            scratch_shapes=[pltpu.VMEM((2, pps, h, dh, ps), F32),
                            pltpu.VMEM((pages_per_seq * h, ps), F32),
                            pltpu.SemaphoreType.DMA((2,))]),
        out_shape=jax.ShapeDtypeStruct((n, h, t, n_blk), I32),
        compiler_params=_params("arbitrary", "arbitrary"),
        name="moba_select",
    )(page_table.reshape(-1), q, cache_kt)


def _moba_sample_kernel(sel_ref, pt_ref, q_ref, kn_ref, vn_ref, ck_ref, cv_ref, o_ref, kbuf, vbuf, sem, *,
                        layer, pages_per_seq):
    n_heads = pl.num_programs(1)
    step = pl.program_id(0) * n_heads + pl.program_id(1)
    n_steps = pl.num_programs(0) * n_heads
    t = q_ref.shape[2]
    n_pages = kbuf.shape[1]
    pages_per_q = n_pages // t
    slot = step % 2

    def page_copies(st, sl, i, pg):
        hh = st % n_heads
        return (pltpu.make_async_copy(ck_ref.at[layer, pg, hh], kbuf.at[sl, i], sem.at[0, sl]),
                pltpu.make_async_copy(cv_ref.at[layer, pg, hh], vbuf.at[sl, i], sem.at[1, sl]))

    def fetch(st, sl):
        pt_base = (st // n_heads) * pages_per_seq
        for b in range(n_pages // PAGES_PER_BLOCK):
            blk = sel_ref[st * (n_pages // PAGES_PER_BLOCK) + b]
            for pi in range(PAGES_PER_BLOCK):
                pg = pt_ref[pt_base + blk * PAGES_PER_BLOCK + pi]
                for cp in page_copies(st, sl, b * PAGES_PER_BLOCK + pi, pg):
                    cp.start()

    @pl.when(step == 0)
    def _():
        fetch(step, slot)

    @pl.when(step + 1 < n_steps)
    def _():
        fetch(step + 1, 1 - slot)

    for i in range(n_pages):
        for cp in page_copies(step, slot, i, 0):
            cp.wait()

    q = q_ref[0, 0]
    qb = q.astype(BF16)
    s = jnp.concatenate([_dot(qb, kbuf[slot, i].astype(BF16)) for i in range(n_pages)], axis=1)
    per_q = pages_per_q * PAGE_SIZE
    col = lax.broadcasted_iota(I32, s.shape, 1)
    lo = lax.broadcasted_iota(I32, s.shape, 0) * per_q
    own = (col >= lo) & (col < lo + per_q)
    sn = _dot_nt(q, kn_ref[0, 0])
    causal = lax.broadcasted_iota(I32, (t, t), 1) <= lax.broadcasted_iota(I32, (t, t), 0)
    m = jnp.maximum(jnp.max(jnp.where(own, s, NEG_BIG), axis=1, keepdims=True),
                    jnp.max(jnp.where(causal, sn, NEG_BIG), axis=1, keepdims=True))
    p = jnp.where(own, jnp.exp(s - m), 0.0).astype(BF16)
    pn = jnp.where(causal, jnp.exp(sn - m), 0.0)
    l = jnp.sum(p.astype(F32), axis=1, keepdims=True) + jnp.sum(pn, axis=1, keepdims=True)
    o = _dot(pn, vn_ref[0, 0])
    for i in range(n_pages):
        o = o + _dot_nt(p[:, i * PAGE_SIZE:(i + 1) * PAGE_SIZE], vbuf[slot, i].astype(BF16))
    o_ref[0, 0] = o / l


def _moba_sample(q, k_new, v_new, cache_kt, cache_vt, sel, page_table, layer):
    n, h, t, dh = q.shape
    n_pages = t * MOBA_TOPK * PAGES_PER_BLOCK
    blk = pl.BlockSpec((1, 1, t, dh), lambda i, j, sl, pt: (i, j, 0, 0))
    anyspec = pl.BlockSpec(memory_space=pl.ANY)
    return pl.pallas_call(
        functools.partial(_moba_sample_kernel, layer=layer, pages_per_seq=page_table.shape[1]),
        grid_spec=pltpu.PrefetchScalarGridSpec(
            num_scalar_prefetch=2,
            grid=(n, h),
            in_specs=[blk, blk, blk, anyspec, anyspec],
            out_specs=blk,
            scratch_shapes=[pltpu.VMEM((2, n_pages, dh, PAGE_SIZE), F32),
                            pltpu.VMEM((2, n_pages, dh, PAGE_SIZE), F32),
                            pltpu.SemaphoreType.DMA((2, 2))]),
        out_shape=jax.ShapeDtypeStruct((n, h, t, dh), F32),
        compiler_params=_params("arbitrary", "arbitrary"),
        name="moba_sample",
    )(sel.reshape(-1), page_table.reshape(-1), q, k_new, v_new, cache_kt, cache_vt)


def _merge_kernel(xp_ref, xs_ref, oap_ref, oas_ref, obp_ref, obs_ref, ga_ref, gb_ref, g1_ref, sh_ref, sc_ref,
                  nf_ref, wpa_ref, wpb_ref, wo_ref, wr_ref, br_ref, x1_ref, h2_ref, eid_ref, gw_ref, *, n_first):
    pa = _dot(_pick_rows(n_first, oap_ref, oas_ref).astype(BF16), wpa_ref[...])
    pb = _dot(_pick_rows(n_first, obp_ref, obs_ref), wpb_ref[...])
    mix = jax.nn.sigmoid(ga_ref[...]) * pa + jax.nn.sigmoid(gb_ref[...]) * pb
    x1 = _pick_rows(n_first, xp_ref, xs_ref) + g1_ref[0] * _dot(mix.astype(BF16), wo_ref[...])
    x1_ref[...] = x1
    h2 = x1 * lax.rsqrt(jnp.mean(x1 * x1, axis=-1, keepdims=True) + NORM_EPS) * nf_ref[...]
    h2 = h2 * (1.0 + sc_ref[0]) + sh_ref[0]
    for s in range(h2_ref.shape[1]):
        h2_ref[:, s, :] = h2[:, s * LANES:(s + 1) * LANES]

    logits = _dot3(h2, wr_ref[...]) + br_ref[...]
    lane = lax.broadcasted_iota(I32, logits.shape, 1)
    lanef = lane.astype(F32)
    is_g = lane < N_GROUPS
    gl = jnp.where(is_g, logits, -jnp.inf)
    gmax = jnp.max(gl, axis=1, keepdims=True)
    g_sel = jnp.min(jnp.where(gl == gmax, lanef, float(LANES)), axis=1, keepdims=True)
    p_group = 1.0 / jnp.sum(jnp.exp(gl - gmax), axis=1, keepdims=True)
    e_lo = N_GROUPS + g_sel * EXPERTS_PER_GROUP
    in_grp = (lanef >= e_lo) & (lanef < e_lo + EXPERTS_PER_GROUP)
    el = jnp.where(in_grp, logits, -jnp.inf)
    m1 = jnp.max(el, axis=1, keepdims=True)
    i1 = jnp.min(jnp.where(el == m1, lanef, float(LANES)), axis=1, keepdims=True)
    el2 = jnp.where(lanef == i1, -jnp.inf, el)
    m2 = jnp.max(el2, axis=1, keepdims=True)
    i2 = jnp.min(jnp.where(el2 == m2, lanef, float(LANES)), axis=1, keepdims=True)
    e2 = jnp.exp(m2 - m1)
    w1 = p_group / (1.0 + e2)
    w2 = p_group * e2 / (1.0 + e2)
    eid = jnp.where(lane == 0, i1 - N_GROUPS, jnp.where(lane == 1, i2 - N_GROUPS, 0.0))
    eid_ref[...] = eid.astype(I32)
    gw_ref[...] = jnp.where(lane == 0, w1, jnp.where(lane == 1, w2, 0.0))


def _merge(xp, xs, oa_p, oa_s, ob_p, ob_s, zg, mod_tiles, norm_ffn, wpa, wpb, wo, w_router, b_router,
           tiles_per_seq):
    d = xp.shape[1]
    n_first = xp.shape[0] // ROW_TILE
    m = xp.shape[0] + xs.shape[0]
    n_mod = mod_tiles.shape[1]
    mod_map = lambda i: (jnp.minimum(i // tiles_per_seq, n_mod - 1), 0, 0)
    const = lambda i: (0, 0)
    row = lambda i: (i, 0)
    mod = pl.BlockSpec((1, ROW_TILE, d), mod_map)
    wspec = lambda w: pl.BlockSpec(w.shape, const, pipeline_mode=pl.Buffered(1))
    return pl.pallas_call(
        functools.partial(_merge_kernel, n_first=n_first),
        grid=(m // ROW_TILE,),
        in_specs=_two_group_specs(d, n_first) + _two_group_specs(A_WIDTH, n_first)
                 + _two_group_specs(B_WIDTH, n_first) + [
                  pl.BlockSpec((ROW_TILE, d), lambda i: (i, 0)),
                  pl.BlockSpec((ROW_TILE, d), lambda i: (i, 1)),
                  mod, mod, mod,
                  pl.BlockSpec((1, d), const),
                  wspec(wpa), wspec(wpb), wspec(wo), wspec(w_router),
                  pl.BlockSpec((1, LANES), const)],
        out_specs=[pl.BlockSpec((ROW_TILE, d), row),
                   pl.BlockSpec((ROW_TILE, d // LANES, LANES), lambda i: (i, 0, 0)),
                   pl.BlockSpec((ROW_TILE, LANES), row),
                   pl.BlockSpec((ROW_TILE, LANES), row)],
        out_shape=[jax.ShapeDtypeStruct((m, d), F32),
                   jax.ShapeDtypeStruct((m, d // LANES, LANES), F32),
                   jax.ShapeDtypeStruct((m, LANES), I32),
                   jax.ShapeDtypeStruct((m, LANES), F32)],
        compiler_params=_params("parallel"),
        name="merge_router",
    )(xp, xs, oa_p, oa_s, ob_p, ob_s, zg, zg, mod_tiles[2], mod_tiles[3], mod_tiles[4],
      norm_ffn.reshape(1, d), wpa, wpb, wo, w_router, b_router)


MOE_RING = 3


def _moe_kernel(be_ref, nu_ref, tok_ref, h_ref, wg_ref, wu_ref, wd_ref, y_ref, xbuf, sem, wg_s, wu_s, wd_s):
    i = pl.program_id(0)
    n_used = nu_ref[0]
    n_sub = h_ref.shape[1]
    ahead = MOE_RING - 1

    def row_copy(blk, r):
        return pltpu.make_async_copy(h_ref.at[tok_ref[blk * MOE_TILE + r]],
                                     xbuf.at[blk % MOE_RING, pl.ds(r * n_sub, n_sub), :],
                                     sem.at[blk % MOE_RING])

    def fetch(blk):
        for r in range(MOE_TILE):
            row_copy(blk, r).start(priority=r % 2)

    for b in range(ahead):
        @pl.when((i == 0) & (b < n_used))
        def _():
            fetch(b)

    @pl.when(i + ahead < n_used)
    def _():
        fetch(i + ahead)

    prev = be_ref[jnp.maximum(i - 1, 0)]
    fresh = (i == 0) | (be_ref[i] != prev)

    @pl.when(fresh)
    def _():
        wg_s[...] = wg_ref[...].astype(BF16)
        wu_s[...] = wu_ref[...].astype(BF16)
        wd_s[...] = wd_ref[...].astype(BF16)

    @pl.when(i < n_used)
    def _():
        for r in range(MOE_TILE):
            row_copy(i, r).wait()
        slot = i % MOE_RING
        xb = jnp.concatenate([xbuf[slot, pl.ds(s, MOE_TILE, stride=n_sub), :].astype(BF16)
                              for s in range(n_sub)], axis=1)
        hid = _silu(_dot(xb, wg_s[...])) * _dot(xb, wu_s[...])
        y_ref[...] = _dot(hid.astype(BF16), wd_s[...])

    @pl.when(i >= n_used)
    def _():
        y_ref[...] = jnp.zeros(y_ref.shape, F32)


def _moe(h, row_tok, block_expert, n_used, w_gate, w_up, w_down, layer):
    n_sub, lanes = h.shape[1], h.shape[2]
    d = n_sub * lanes
    n_rows = row_tok.shape[0]
    de = w_gate.shape[-1]
    return pl.pallas_call(
        _moe_kernel,
        grid_spec=pltpu.PrefetchScalarGridSpec(
            num_scalar_prefetch=3,
            grid=(n_rows // MOE_TILE,),
            in_specs=[pl.BlockSpec(memory_space=pl.ANY),
                      pl.BlockSpec((None, None, d, de), lambda i, be, nu, tk: (layer, be[i], 0, 0)),
                      pl.BlockSpec((None, None, d, de), lambda i, be, nu, tk: (layer, be[i], 0, 0)),
                      pl.BlockSpec((None, None, de, d), lambda i, be, nu, tk: (layer, be[i], 0, 0))],
            out_specs=pl.BlockSpec((MOE_TILE, d), lambda i, be, nu, tk: (i, 0)),
            scratch_shapes=[pltpu.VMEM((MOE_RING, MOE_TILE * n_sub, lanes), F32),
                            pltpu.SemaphoreType.DMA((MOE_RING,)),
                            pltpu.VMEM((d, de), BF16), pltpu.VMEM((d, de), BF16), pltpu.VMEM((de, d), BF16)]),
        out_shape=jax.ShapeDtypeStruct((n_rows, d), F32),
        compiler_params=_params("arbitrary"),
        name="moe_experts",
    )(block_expert, n_used, row_tok, h, w_gate, w_up, w_down)


def _final_kernel(x1_ref, y0_ref, y1_ref, gw_ref, g2_ref, nf_ref, op_ref, os_ref, *, last, n_first):
    gw = gw_ref[...]
    f = gw[:, 0:1] * y0_ref[...] + gw[:, 1:2] * y1_ref[...]
    x2 = x1_ref[...] + g2_ref[0] * f
    if last:
        x2 = x2 * lax.rsqrt(jnp.mean(x2 * x2, axis=-1, keepdims=True) + NORM_EPS) * nf_ref[...]
    i = pl.program_id(0)

    @pl.when(i < n_first)
    def _():
        op_ref[...] = x2

    @pl.when(i >= n_first)
    def _():
        os_ref[...] = x2


def _final(x1, y0, y1, gw, mod_tiles, norm_final, tiles_per_seq, last, n_first):
    m, d = x1.shape
    n_mod = mod_tiles.shape[1]
    mod_map = lambda i: (jnp.minimum(i // tiles_per_seq, n_mod - 1), 0, 0)
    row = lambda i: (i, 0)
    rs = pl.BlockSpec((ROW_TILE, d), row)
    return pl.pallas_call(
        functools.partial(_final_kernel, last=last, n_first=n_first),
        grid=(m // ROW_TILE,),
        in_specs=[rs, rs, rs, pl.BlockSpec((ROW_TILE, LANES), row),
                  pl.BlockSpec((1, ROW_TILE, d), mod_map),
                  pl.BlockSpec((1, d), lambda i: (0, 0))],
        out_specs=_two_group_specs(d, n_first),
        out_shape=[jax.ShapeDtypeStruct((n_first * ROW_TILE, d), F32),
                   jax.ShapeDtypeStruct((m - n_first * ROW_TILE, d), F32)],
        compiler_params=_params("arbitrary"),
        name="combine_final",
    )(x1, y0, y1, gw, mod_tiles[5], norm_final.reshape(1, d))


def _dispatch(eid):
    m = eid.shape[0]
    n_assign = m * EXPERT_TOPK
    flat_e = eid.reshape(-1)
    flat_tok = jnp.repeat(jnp.arange(m, dtype=I32), EXPERT_TOPK)
    onehot = (flat_e[:, None] == jnp.arange(N_EXPERTS, dtype=I32)[None, :]).astype(I32)
    csum = jnp.cumsum(onehot, axis=0)
    rank = jnp.take_along_axis(csum, flat_e[:, None], axis=1)[:, 0] - 1
    counts = csum[-1]
    padded = (counts + MOE_TILE - 1) // MOE_TILE * MOE_TILE
    pad_end = jnp.cumsum(padded)
    pad_start = pad_end - padded
    dest = (pad_start[flat_e] + rank).astype(I32)
    n_blocks = -(-(n_assign + N_EXPERTS * (MOE_TILE - 1)) // MOE_TILE)
    row_tok = jnp.zeros((n_blocks * MOE_TILE,), I32).at[dest].set(flat_tok)
    blk_start = jnp.arange(n_blocks, dtype=I32) * MOE_TILE
    block_expert = jnp.minimum(jnp.sum((pad_end[None, :] <= blk_start[:, None]).astype(I32), axis=1),
                               N_EXPERTS - 1).astype(I32)
    n_used = (pad_end[-1:] // MOE_TILE).astype(I32)
    return dest.reshape(m, EXPERT_TOPK), row_tok, block_expert, n_used


def _mod_tiles(mod, n_prompt, reps):
    d = mod.shape[-1]
    mp = jnp.broadcast_to(mod[:n_prompt, :, None, :], (n_prompt, 6, ROW_TILE, d))
    ms = jnp.repeat(mod[n_prompt:], reps, axis=0).transpose(1, 0, 2)[None]
    return jnp.concatenate([mp, ms], axis=0).transpose(1, 0, 2, 3)


def kernel(x_prompt, x_sample, c_prompt, c_sample, cache_k, cache_v, state_hgrn, page_table,
           w_ada, b_ada, norm_mix, norm_ffn, w_in, hgrn_lb_logits, hgrn_norm,
           w_proj_a, w_proj_b, w_out, w_group, b_group, w_expert_router, b_expert_router,
           w_gate, w_up, w_down, norm_final):
    bp, tp, d = x_prompt.shape
    bs, ts, _ = x_sample.shape
    mp_rows, ms_rows = bp * tp, bs * ts
    assert ms_rows == ROW_TILE and tp % ROW_TILE == 0
    m = mp_rows + ms_rows
    tiles_per_seq = tp // ROW_TILE
    n_full = PAST_LEN // MOBA_BLOCK
    assert PAST_LEN % MOBA_BLOCK == 0 and n_full >= MOBA_TOPK

    xp, xs = x_prompt.reshape(mp_rows, d), x_sample.reshape(ms_rows, d)
    c_all = jnp.concatenate([c_prompt, c_sample, jnp.zeros((-(bp + bs) % 8, d), F32)], axis=0)
    lb_all = jnp.cumsum(jax.nn.softmax(hgrn_lb_logits.astype(F32), axis=0), axis=0)
    cache_kt = cache_k.transpose(0, 1, 2, 4, 3)
    cache_vt = cache_v.transpose(0, 1, 2, 4, 3)
    kp_l, vp_l, sp_l, ks_l, vs_l, ss_l = [], [], [], [], [], []

    for l in range(DEPTH):
        mod = _modulation(c_all, w_ada[l], b_ada[l])[:bp + bs].reshape(bp + bs, 6, d)
        mt = _mod_tiles(mod, bp, ts)
        za, zb, zg = _inproj(xp, xs, mt, norm_mix[l], w_in[l], tiles_per_seq)

        oa_p, st_p = _hgrn(za, lb_all[l], hgrn_norm[l], 0, bp, tp, ROW_TILE, HGRN_CHUNK)
        oa_s, st_s = _hgrn(za, lb_all[l], hgrn_norm[l], mp_rows, bs, ts, ts, ts, s0=state_hgrn[l])

        q_p, k_p, v_p = _rope_split(zb, 0, bp, tp, 512, jnp.arange(tp))
        q_s, k_s, v_s = _rope_split(zb, mp_rows, bs, ts, ts, PAST_LEN + jnp.arange(ts))
        ob_p = _moba_prompt(q_p, k_p, v_p)
        sel = _select_blocks(q_s, cache_kt, page_table[:, :n_full * PAGES_PER_BLOCK], l)[..., :MOBA_TOPK]
        ob_s = _moba_sample(q_s, k_s, v_s, cache_kt, cache_vt, sel, page_table, l)
        ob_s = ob_s.transpose(0, 2, 1, 3).reshape(ms_rows, B_WIDTH).astype(BF16)

        w_router = jnp.concatenate(
            [w_group[l], w_expert_router[l].transpose(1, 0, 2).reshape(d, N_EXPERTS),
             jnp.zeros((d, LANES - N_GROUPS - N_EXPERTS), F32)], axis=1)
        b_router = jnp.concatenate(
            [b_group[l], b_expert_router[l].reshape(-1),
             jnp.zeros((LANES - N_GROUPS - N_EXPERTS,), F32)]).reshape(1, LANES)
        x1, h2, eid, gw = _merge(xp, xs, oa_p, oa_s, ob_p, ob_s, zg, mt, norm_ffn[l], w_proj_a[l].astype(BF16),
                                 w_proj_b[l].astype(BF16), w_out[l].astype(BF16), w_router, b_router,
                                 tiles_per_seq)

        dest, row_tok, block_expert, n_used = _dispatch(eid[:, :EXPERT_TOPK])
        ys = _moe(h2, row_tok, block_expert, n_used, w_gate, w_up, w_down, l)
        xp, xs = _final(x1, ys[dest[:, 0]], ys[dest[:, 1]], gw, mt, norm_final, tiles_per_seq,
                        l == DEPTH - 1, mp_rows // ROW_TILE)

        kp_l.append(k_p); vp_l.append(v_p); sp_l.append(st_p)
        ks_l.append(k_s); vs_l.append(v_s); ss_l.append(st_s)

    y_prompt = xp.reshape(bp, tp, d)
    y_sample = xs.reshape(bs, ts, d)
    return (y_prompt, y_sample, jnp.stack(kp_l), jnp.stack(vp_l), jnp.stack(sp_l),
            jnp.stack(ks_l), jnp.stack(vs_l), jnp.stack(ss_l))
```

```python
import functools

import numpy as np
import jax
import jax.numpy as jnp
from jax import lax
from jax.experimental import pallas as pl
from jax.experimental.pallas import tpu as pltpu

F32 = jnp.float32
BF16 = jnp.bfloat16
I32 = jnp.int32

D_MODEL = 1024
DEPTH = 1
PAST_LEN = 16384
PAGE_SIZE = 128
A_HEADS = 4
A_KDIM = 128
A_VDIM = 128
A_WIDTH = A_HEADS * A_KDIM
B_HEADS = 8
B_HEAD_DIM = 64
B_WIDTH = B_HEADS * B_HEAD_DIM
MOBA_BLOCK = 256
MOBA_TOPK = 3
ROT_DIM = B_HEAD_DIM // 4
ROPE_THETA = 500000.0
N_GROUPS = 4
EXPERTS_PER_GROUP = 8
N_EXPERTS = N_GROUPS * EXPERTS_PER_GROUP
EXPERT_TOPK = 2
D_EXPERT = D_MODEL // 2
NORM_EPS = 1e-6

PAGES_PER_BLOCK = MOBA_BLOCK // PAGE_SIZE
ROW_TILE = 256
HGRN_CHUNK = 16
MOE_TILE = 256
LANES = 128
NEG_BIG = -1e30
LOG2_E = 1.4426950408889634
VMEM_LIMIT_BYTES = 52 * 1024 * 1024


def _params(*sem):
    return pltpu.CompilerParams(dimension_semantics=sem, vmem_limit_bytes=VMEM_LIMIT_BYTES)


def _dot(a, b):
    return jnp.dot(a, b, preferred_element_type=F32)


def _dot_nt(a, b):
    return lax.dot_general(a, b, (((1,), (1,)), ((), ())), preferred_element_type=F32)


def _dot_tn(a, b):
    return lax.dot_general(a, b, (((0,), (0,)), ((), ())), preferred_element_type=F32)


def _split(a):
    hi = a.astype(BF16)
    return hi, (a - hi.astype(F32)).astype(BF16)


def _dot3(a, b, dot=_dot):
    ah, al = _split(a)
    bh, bl = _split(b)
    return dot(ah, bh) + (dot(ah, bl) + dot(al, bh))


def _silu(x):
    return x * jax.nn.sigmoid(x)


def _mod_kernel(c_ref, w_ref, b_ref, o_ref):
    o_ref[...] = _dot3(_silu(c_ref[...]), w_ref[...]) + b_ref[...]


def _modulation(c_all, w, b):
    n = c_all.shape[0]
    d, dout = w.shape
    return pl.pallas_call(
        _mod_kernel,
        grid=(dout // d,),
        in_specs=[pl.BlockSpec((n, d), lambda j: (0, 0)),
                  pl.BlockSpec((d, d), lambda j: (0, j)),
                  pl.BlockSpec((1, d), lambda j: (0, j))],
        out_specs=pl.BlockSpec((n, d), lambda j: (0, j)),
        out_shape=jax.ShapeDtypeStruct((n, dout), F32),
        compiler_params=_params("parallel"),
        name="modulation",
    )(c_all, w, b.reshape(1, dout))


def _pick_rows(n_first, first_ref, second_ref):
    return jnp.where(pl.program_id(0) < n_first, first_ref[...], second_ref[...])


def _two_group_specs(width, n_first):
    return [pl.BlockSpec((ROW_TILE, width), lambda i: (jnp.minimum(i, n_first - 1), 0)),
            pl.BlockSpec((ROW_TILE, width), lambda i: (0, 0))]


def _mod_specs(mods, k, tiles_per_seq):
    mod_p, mod_s = mods
    n_seq, d = mod_p.shape[1], mod_p.shape[3]
    return [pl.BlockSpec((None, None, 1, d), lambda i: (k, jnp.minimum(i // tiles_per_seq, n_seq - 1), 0, 0)),
            pl.BlockSpec((None, ROW_TILE, d), lambda i: (k, 0, 0))]


def _inproj_kernel(xp_ref, xs_ref, shp_ref, shs_ref, scp_ref, scs_ref, g_ref, w_ref, wlo_ref,
                   za_ref, zb_ref, zg_ref, *, n_first):
    x = _pick_rows(n_first, xp_ref, xs_ref)
    h = x * lax.rsqrt(jnp.mean(x * x, axis=-1, keepdims=True) + NORM_EPS) * g_ref[...]
    h = h * (1.0 + _pick_rows(n_first, scp_ref, scs_ref)) + _pick_rows(n_first, shp_ref, shs_ref)
    hh, hl = _split(h)
    wa = 4 * A_WIDTH
    for c in range(0, wa, 512):
        za_ref[:, c:c + 512] = _dot(hh, w_ref[:, c:c + 512])
    for c in range(0, 2 * B_WIDTH, 512):
        wc = w_ref[:, wa + c:wa + c + 512]
        zb_ref[:, c:c + 512] = _dot(hh, wc) + (_dot(hh, wlo_ref[:, c:c + 512]) + _dot(hl, wc))
    c = 2 * B_WIDTH
    zb_ref[:, c:c + 512] = _dot(hh, w_ref[:, wa + c:wa + c + 512])
    wg = wa + 3 * B_WIDTH
    for c in range(0, 2 * D_MODEL, 512):
        zg_ref[:, c:c + 512] = _dot(hh, w_ref[:, wg + c:wg + c + 512])


def _inproj(xp, xs, mods, norm_g, w_in, tiles_per_seq):
    d = xp.shape[1]
    n_first = xp.shape[0] // ROW_TILE
    m = xp.shape[0] + xs.shape[0]
    w_hi = w_in.astype(BF16)
    wa = 4 * A_WIDTH
    wqk = w_in[:, wa:wa + 2 * B_WIDTH]
    w_lo = (wqk - wqk.astype(BF16).astype(F32)).astype(BF16)
    const = lambda i: (0, 0)
    row = lambda i: (i, 0)
    return pl.pallas_call(
        functools.partial(_inproj_kernel, n_first=n_first),
        grid=(m // ROW_TILE,),
        in_specs=_two_group_specs(d, n_first) + _mod_specs(mods, 0, tiles_per_seq)
                 + _mod_specs(mods, 1, tiles_per_seq) + [
                  pl.BlockSpec((1, d), const),
                  pl.BlockSpec(w_hi.shape, const, pipeline_mode=pl.Buffered(1)),
                  pl.BlockSpec(w_lo.shape, const, pipeline_mode=pl.Buffered(1))],
        out_specs=[pl.BlockSpec((ROW_TILE, wa), row),
                   pl.BlockSpec((ROW_TILE, 3 * B_WIDTH), row),
                   pl.BlockSpec((ROW_TILE, 2 * D_MODEL), row)],
        out_shape=[jax.ShapeDtypeStruct((m, wa), F32),
                   jax.ShapeDtypeStruct((m, 3 * B_WIDTH), F32),
                   jax.ShapeDtypeStruct((m, 2 * D_MODEL), F32)],
        compiler_params=_params("parallel"),
        name="inproj",
    )(xp, xs, *mods, *mods, norm_g.reshape(1, d), w_hi, w_lo)


def _rope_kernel(q_ref, k_ref, v_ref, c_ref, s1_ref, s2_ref, qo_ref, ko_ref, vo_ref):
    cos, s1, s2 = c_ref[...], s1_ref[...], s2_ref[...]
    half = ROT_DIM // 2

    def rope(x):
        up = pltpu.roll(x, B_WIDTH - half, axis=1)
        dn = pltpu.roll(x, half, axis=1)
        return x * cos + up * s1 + dn * s2

    q = rope(q_ref[...]) * (B_HEAD_DIM ** -0.5)
    k = rope(k_ref[...])
    v = v_ref[...]
    for h in range(B_HEADS):
        ls = slice(h * B_HEAD_DIM, (h + 1) * B_HEAD_DIM)
        qo_ref[0, h] = q[:, ls]
        ko_ref[0, h] = k[:, ls]
        vo_ref[0, h] = v[:, ls]


def _rope_tables(pos):
    half = ROT_DIM // 2
    inv = jnp.power(ROPE_THETA, -(jnp.arange(half, dtype=F32) * 2.0 / ROT_DIM))
    ang = pos.astype(F32)[:, None] * inv[None, :]
    cos, sin = jnp.cos(ang), jnp.sin(ang)
    t = pos.shape[0]
    rest = B_HEAD_DIM - ROT_DIM
    c = jnp.concatenate([cos, cos, jnp.ones((t, rest), F32)], axis=-1)
    s1 = jnp.concatenate([-sin, jnp.zeros((t, half + rest), F32)], axis=-1)
    s2 = jnp.concatenate([jnp.zeros((t, half), F32), sin, jnp.zeros((t, rest), F32)], axis=-1)
    return [jnp.tile(a, (1, B_HEADS)) for a in (c, s1, s2)]


def _rope_split(zb, row0, n, t, tile, pos):
    tabs = _rope_tables(pos)
    tps = t // tile
    blk0 = row0 // tile
    zmap = lambda c: (lambda b, s: (blk0 + b * tps + s, c))
    tmap = lambda b, s: (s, 0)
    omap = lambda b, s: (b, 0, s, 0)
    oshape = jax.ShapeDtypeStruct((n, B_HEADS, t, B_HEAD_DIM), F32)
    ospec = pl.BlockSpec((1, B_HEADS, tile, B_HEAD_DIM), omap)
    return pl.pallas_call(
        _rope_kernel,
        grid=(n, tps),
        in_specs=[pl.BlockSpec((tile, B_WIDTH), zmap(0)),
                  pl.BlockSpec((tile, B_WIDTH), zmap(1)),
                  pl.BlockSpec((tile, B_WIDTH), zmap(2)),
                  pl.BlockSpec((tile, B_WIDTH), tmap),
                  pl.BlockSpec((tile, B_WIDTH), tmap),
                  pl.BlockSpec((tile, B_WIDTH), tmap)],
        out_specs=[ospec, ospec, ospec],
        out_shape=[oshape, oshape, oshape],
        compiler_params=_params("parallel", "parallel"),
        name="rope_split",
    )(zb, zb, zb, *tabs)


def _hgrn_kernel(*refs, chunk, has_s0):
    aq_ref, af_ref, ai_ref, ag_ref, lb_ref, gain_ref = refs[:6]
    rest = refs[6:]
    s0_ref = None
    if has_s0:
        s0_ref, rest = rest[0], rest[1:]
    o_ref, so_ref, st_ref, q_s, b_s, k_s = rest
    t = pl.program_id(1)
    tb = aq_ref.shape[0]

    @pl.when(t == 0)
    def _():
        for h in range(A_HEADS):
            if has_s0:
                st_ref[h] = s0_ref[0, h].T
            else:
                st_ref[h] = jnp.zeros((A_VDIM, A_KDIM), F32)

    lb = lb_ref[...]
    f = lb + (1.0 - lb) * jax.nn.sigmoid(af_ref[...])
    logf = jnp.log(f)
    q_s[...] = _silu(aq_ref[...])
    k_s[...] = 1.0 - f
    row = lax.broadcasted_iota(I32, logf.shape, 0) & (chunk - 1)
    b = logf
    sh = 1
    while sh < chunk:
        b = b + jnp.where(row >= sh, pltpu.roll(b, sh, axis=0), 0.0)
        sh *= 2
    b_s[...] = b
    rowc = lax.broadcasted_iota(I32, (chunk, A_KDIM), 0)

    def one_chunk(ci, carry):
        r0 = pl.multiple_of(ci * chunk, chunk)
        rs = pl.ds(r0, chunk)
        for h in range(A_HEADS):
            ls = slice(h * A_KDIM, (h + 1) * A_KDIM)
            qc, bc, kc, vc = q_s[rs, ls], b_s[rs, ls], k_s[rs, ls], ai_ref[rs, ls]
            st = st_ref[h]
            bl = bc[chunk - 1:chunk, :]
            o = _dot_nt((qc * jnp.exp(bc)).astype(BF16), st.astype(BF16))
            grp = 8
            parts = [o[g * grp:(g + 1) * grp] for g in range(chunk // grp)]
            for s in range(chunk):
                for g in range(s // grp, chunk // grp):
                    rows = slice(g * grp, (g + 1) * grp)
                    diff = bc[rows] - bc[s:s + 1, :]
                    if g == s // grp:
                        diff = jnp.where(rowc[rows] >= s, diff, -jnp.inf)
                    e = jnp.exp(diff)
                    r = jnp.sum(qc[rows] * e * kc[s:s + 1, :], axis=1, keepdims=True)
                    parts[g] = parts[g] + r * vc[s:s + 1, :]
            o = jnp.concatenate(parts, axis=0) if len(parts) > 1 else parts[0]
            kp = kc * jnp.exp(bl - bc)
            st_ref[h] = st * jnp.exp(bl) + _dot_tn(vc.astype(BF16), kp.astype(BF16))
            o_ref[rs, ls] = o
        return carry

    n_chunks = tb // chunk
    lax.fori_loop(0, n_chunks, one_chunk, 0, unroll=4 if n_chunks % 4 == 0 else 1)

    gain = gain_ref[...]
    for h in range(A_HEADS):
        ls = slice(h * A_VDIM, (h + 1) * A_VDIM)
        oh = o_ref[:, ls]
        y = oh * lax.rsqrt(jnp.mean(oh * oh, axis=-1, keepdims=True) + NORM_EPS) * gain
        o_ref[:, ls] = y * _silu(ag_ref[:, ls])

    @pl.when(t == pl.num_programs(1) - 1)
    def _():
        for h in range(A_HEADS):
            so_ref[0, h] = st_ref[h].T


def _hgrn(za, lb, gain, row0, n, t, tile, chunk, s0=None):
    tps = t // tile
    blk0 = row0 // tile
    zmap = lambda c: (lambda b, s: (blk0 + b * tps + s, c))
    const = lambda b, s: (0, 0)
    in_specs = [pl.BlockSpec((tile, A_WIDTH), zmap(c)) for c in range(4)]
    in_specs += [pl.BlockSpec((1, A_WIDTH), const), pl.BlockSpec((1, A_VDIM), const)]
    args = [za, za, za, za, lb.reshape(1, A_WIDTH), gain.reshape(1, A_VDIM)]
    if s0 is not None:
        in_specs.append(pl.BlockSpec((1, A_HEADS, A_KDIM, A_VDIM), lambda b, s: (b, 0, 0, 0)))
        args.append(s0)
    return pl.pallas_call(
        functools.partial(_hgrn_kernel, chunk=chunk, has_s0=s0 is not None),
        grid=(n, tps),
        in_specs=in_specs,
        out_specs=[pl.BlockSpec((tile, A_WIDTH), lambda b, s: (b * tps + s, 0)),
                   pl.BlockSpec((1, A_HEADS, A_KDIM, A_VDIM), lambda b, s: (b, 0, 0, 0))],
        out_shape=[jax.ShapeDtypeStruct((n * t, A_WIDTH), F32),
                   jax.ShapeDtypeStruct((n, A_HEADS, A_KDIM, A_VDIM), F32)],
        scratch_shapes=[pltpu.VMEM((A_HEADS, A_VDIM, A_KDIM), F32),
                        pltpu.VMEM((tile, A_WIDTH), F32),
                        pltpu.VMEM((tile, A_WIDTH), F32),
                        pltpu.VMEM((tile, A_WIDTH), F32)],
        compiler_params=_params("parallel", "arbitrary"),
        name="hgrn2",
    )(*args)


MOBA_CHUNK_BLOCKS = 4
MOBA_HEADS_PER_STEP = 2


def _moba_prompt_kernel(q_ref, k_ref, v_ref, o_ref, km_ref, ka_ref, vt_ref, eye_ref):
    j = pl.program_id(2)
    blk = MOBA_BLOCK
    hp, dh = q_ref.shape[1], q_ref.shape[3]
    n_chunks, kc = ka_ref.shape[0], ka_ref.shape[1]
    n_blk = n_chunks * MOBA_CHUNK_BLOCKS
    eye = (lax.broadcasted_iota(I32, (dh, dh), 0) == lax.broadcasted_iota(I32, (dh, dh), 1)).astype(BF16)

    @pl.when(j == 0)
    def _():
        eye_ref[...] = (lax.broadcasted_iota(I32, (blk, blk), 0)
                        == lax.broadcasted_iota(I32, (blk, blk), 1)).astype(BF16)
        for c in range(n_chunks):
            rows = slice(c * kc, (c + 1) * kc)
            key_blk = lax.broadcasted_iota(I32, (kc, n_blk), 0) // blk + c * MOBA_CHUNK_BLOCKS
            onehot = key_blk == lax.broadcasted_iota(I32, (kc, n_blk), 1)
            ka_ref[c, :, hp * dh:] = jnp.where(onehot, 1.0, 0.0).astype(BF16)
            for hh in range(hp):
                ka_ref[c, :, hh * dh:(hh + 1) * dh] = k_ref[0, hh, rows, :].astype(BF16)
                vt_ref[hh, c] = _dot_nt(eye, v_ref[0, hh, rows, :].astype(BF16)).astype(BF16)
        for hh in range(hp):
            for i in range(n_blk):
                km_ref[hh, i:i + 1, :] = jnp.mean(k_ref[0, hh, i * blk:(i + 1) * blk, :], axis=0, keepdims=True)

    bidx = lax.broadcasted_iota(I32, (n_blk, blk), 0)
    valid = bidx < j
    q_rows, biases = [], []
    for hh in range(hp):
        q = q_ref[0, hh]
        sc = jnp.where(valid, _dot3(km_ref[hh], q, _dot_nt), -jnp.inf)
        rank = jnp.zeros((n_blk, blk), I32)
        for i in range(n_blk - 1):
            row = sc[i:i + 1, :]
            rank = rank + jnp.where(row > sc, 1, jnp.where(row == sc, jnp.where(bidx > i, 1, 0), 0))
        biases.append(jnp.where(valid, jnp.where(rank < MOBA_TOPK, 0.0, NEG_BIG),
                                jnp.where(bidx == j, 0.0, NEG_BIG)))
        q_t = _dot_nt(eye, (q * LOG2_E).astype(BF16))
        zero = jnp.zeros((dh, blk), F32)
        q_rows.append(jnp.concatenate([q_t if g == hh else zero for g in range(hp)], axis=1))
    qa = jnp.concatenate(q_rows + [jnp.concatenate(biases, axis=1)], axis=0).astype(BF16)

    def scores(c):
        return jnp.concatenate([_dot(ka_ref[c, :kc // 2], qa), _dot(ka_ref[c, kc // 2:], qa)], axis=0)

    def weighted_values(c, p):
        pb = p.astype(BF16)
        return jnp.concatenate([_dot(vt_ref[hh, c], pb[:, hh * blk:(hh + 1) * blk]) for hh in range(hp)], axis=1)

    cj = j // MOBA_CHUNK_BLOCKS
    key_pos = lax.broadcasted_iota(I32, (kc, hp * blk), 0) + cj * kc
    q_pos = (lax.broadcasted_iota(I32, (kc, hp * blk), 1) & (blk - 1)) + j * blk
    s = jnp.where(key_pos <= q_pos, scores(cj), NEG_BIG)
    m = jnp.max(s, axis=0, keepdims=True)
    p = jnp.exp2(s - m)
    init = (m, jnp.sum(p, axis=0, keepdims=True), weighted_values(cj, p))

    def past_chunk(c, carry):
        m, l, acc = carry
        s = scores(c)
        m_new = jnp.maximum(m, jnp.max(s, axis=0, keepdims=True))
        p = jnp.exp2(s - m_new)
        alpha = jnp.exp2(m - m_new)
        return m_new, alpha * l + jnp.sum(p, axis=0, keepdims=True), alpha * acc + weighted_values(c, p)

    _, l, acc = lax.fori_loop(0, cj, past_chunk, init)
    o_t = (acc / l).astype(BF16)
    o_ref[...] = jnp.concatenate([_dot_nt(eye_ref[...], o_t[:, hh * blk:(hh + 1) * blk]) for hh in range(hp)],
                                 axis=1).astype(BF16)


def _moba_prompt(q, k, v):
    b, h, s, dh = q.shape
    hp = MOBA_HEADS_PER_STEP
    n_blk = s // MOBA_BLOCK
    assert n_blk % MOBA_CHUNK_BLOCKS == 0 and h % hp == 0 and hp * dh == LANES
    n_chunks = n_blk // MOBA_CHUNK_BLOCKS
    kc = MOBA_CHUNK_BLOCKS * MOBA_BLOCK
    full = pl.BlockSpec((1, hp, s, dh), lambda bi, hi, j: (bi, hi, 0, 0))
    return pl.pallas_call(
        _moba_prompt_kernel,
        grid=(b, h // hp, n_blk),
        in_specs=[pl.BlockSpec((1, hp, MOBA_BLOCK, dh), lambda bi, hi, j: (bi, hi, j, 0)), full, full],
        out_specs=pl.BlockSpec((MOBA_BLOCK, hp * dh), lambda bi, hi, j: (bi * n_blk + j, hi)),
        out_shape=jax.ShapeDtypeStruct((b * s, h * dh), BF16),
        scratch_shapes=[pltpu.VMEM((hp, n_blk, dh), F32),
                        pltpu.VMEM((n_chunks, kc, hp * dh + n_blk), BF16),
                        pltpu.VMEM((hp, n_chunks, dh, kc), BF16),
                        pltpu.VMEM((MOBA_BLOCK, MOBA_BLOCK), BF16)],
        compiler_params=_params("parallel", "parallel", "arbitrary"),
        name="moba_prompt",
    )(q, k, v)


SELECT_PAGES_PER_STEP = 16


def _select_kernel(pt_ref, q_ref, ck_ref, o_ref, buf, pm_ref, sem, *, layer, pages_per_seq):
    n_grp = pl.num_programs(1)
    g = pl.program_id(1)
    step = pl.program_id(0) * n_grp + g
    n_steps = pl.num_programs(0) * n_grp
    slot = step % 2
    pps, n_heads, dh, ps = buf.shape[1], buf.shape[2], buf.shape[3], buf.shape[4]
    t = q_ref.shape[2]

    def page_copy(st, sl, i, pg):
        return pltpu.make_async_copy(ck_ref.at[layer, pg], buf.at[sl, i], sem.at[sl])

    def fetch(st, sl):
        base = (st // n_grp) * pages_per_seq + (st % n_grp) * pps
        for i in range(pps):
            page_copy(st, sl, i, pt_ref[base + i]).start()

    @pl.when(step == 0)
    def _():
        fetch(step, slot)

    @pl.when(step + 1 < n_steps)
    def _():
        fetch(step + 1, 1 - slot)

    for i in range(pps):
        page_copy(step, slot, i, 0).wait()

    bps = pps // PAGES_PER_BLOCK
    diag = lax.broadcasted_iota(I32, (dh, ps), 0) == lax.broadcasted_iota(I32, (dh, ps), 1)
    blocks = []
    for b in range(bps):
        xb = buf[slot, b * PAGES_PER_BLOCK]
        for pi in range(1, PAGES_PER_BLOCK):
            xb = xb + buf[slot, b * PAGES_PER_BLOCK + pi]
        blocks.append(xb)
    x = jnp.stack(blocks, axis=0)
    r = jnp.sum(x, axis=-1, keepdims=True) * (1.0 / (ps * PAGES_PER_BLOCK))
    bm_step = jnp.sum(jnp.where(diag, r, 0.0), axis=2)
    pm_ref[pl.ds(pl.multiple_of(g * bps * n_heads, bps * n_heads), bps * n_heads), :] = (
        bm_step.reshape(bps * n_heads, ps))

    @pl.when(g == n_grp - 1)
    def _():
        n_blk = pages_per_seq // PAGES_PER_BLOCK
        lane = lax.broadcasted_iota(I32, (t, n_blk), 1).astype(F32)
        for h in range(n_heads):
            bm = pm_ref[pl.ds(h, n_blk, stride=n_heads), :]
            sc = _dot3(q_ref[0, h], bm[:, :dh], _dot_nt)
            res = jnp.zeros((t, n_blk), F32)
            for k in range(MOBA_TOPK):
                m = jnp.max(sc, axis=1, keepdims=True)
                idx = jnp.min(jnp.where(sc == m, lane, float(n_blk)), axis=1, keepdims=True)
                res = jnp.where(lane == k, idx, res)
                sc = jnp.where(lane == idx, -jnp.inf, sc)
            o_ref[0, h] = res.astype(I32)


def _select_blocks(q, cache_kt, page_table, layer):
    n, h, t, dh = q.shape
    ps = cache_kt.shape[-1]
    pages_per_seq = page_table.shape[1]
    pps = SELECT_PAGES_PER_STEP
    assert pages_per_seq % pps == 0 and pages_per_seq % PAGES_PER_BLOCK == 0 and dh <= ps
    n_blk = pages_per_seq // PAGES_PER_BLOCK
    return pl.pallas_call(
        functools.partial(_select_kernel, layer=layer, pages_per_seq=pages_per_seq),
        grid_spec=pltpu.PrefetchScalarGridSpec(
            num_scalar_prefetch=1,
            grid=(n, pages_per_seq // pps),
            in_specs=[pl.BlockSpec((1, h, t, dh), lambda i, g, pt: (i, 0, 0, 0)),
                      pl.BlockSpec(memory_space=pl.ANY)],
            out_specs=pl.BlockSpec((1, h, t, n_blk), lambda i, g, pt: (i, 0, 0, 0)),
            scratch_shapes=[pltpu.VMEM((2, pps, h, dh, ps), F32),
                            pltpu.VMEM((n_blk * h, ps), F32),
                            pltpu.SemaphoreType.DMA((2,))]),
        out_shape=jax.ShapeDtypeStruct((n, h, t, n_blk), I32),
        compiler_params=_params("arbitrary", "arbitrary"),
        name="moba_select",
    )(page_table.reshape(-1), q, cache_kt)


def _moba_sample_kernel(sel_ref, pt_ref, q_ref, kn_ref, vn_ref, ck_ref, cv_ref, o_ref, kbuf, vbuf, sem, *,
                        layer, pages_per_seq):
    n_heads = pl.num_programs(1)
    step = pl.program_id(0) * n_heads + pl.program_id(1)
    n_steps = pl.num_programs(0) * n_heads
    t = q_ref.shape[2]
    n_pages = kbuf.shape[1]
    pages_per_q = n_pages // t
    slot = step % 2

    def page_copies(st, sl, i, pg):
        hh = st % n_heads
        return (pltpu.make_async_copy(ck_ref.at[layer, pg, hh], kbuf.at[sl, i], sem.at[0, sl]),
                pltpu.make_async_copy(cv_ref.at[layer, pg, hh], vbuf.at[sl, i], sem.at[1, sl]))

    def fetch(st, sl):
        pt_base = (st // n_heads) * pages_per_seq
        for b in range(n_pages // PAGES_PER_BLOCK):
            blk = sel_ref[st * (n_pages // PAGES_PER_BLOCK) + b]
            for pi in range(PAGES_PER_BLOCK):
                pg = pt_ref[pt_base + blk * PAGES_PER_BLOCK + pi]
                for thread, cp in enumerate(page_copies(st, sl, b * PAGES_PER_BLOCK + pi, pg)):
                    cp.start(priority=thread)

    @pl.when(step == 0)
    def _():
        fetch(step, slot)

    @pl.when(step + 1 < n_steps)
    def _():
        fetch(step + 1, 1 - slot)

    for i in range(n_pages):
        for cp in page_copies(step, slot, i, 0):
            cp.wait()

    q = q_ref[0, 0]
    qb = q.astype(BF16)
    def pages_of(buf_ref, tq):
        return jnp.concatenate([buf_ref[slot, tq * pages_per_q + i].astype(BF16) for i in range(pages_per_q)],
                               axis=1)

    s = jnp.concatenate([_dot(qb, pages_of(kbuf, tq)) for tq in range(t)], axis=1)
    per_q = pages_per_q * PAGE_SIZE
    col = lax.broadcasted_iota(I32, s.shape, 1)
    lo = lax.broadcasted_iota(I32, s.shape, 0) * per_q
    own = (col >= lo) & (col < lo + per_q)
    sn = _dot_nt(q, kn_ref[0, 0])
    causal = lax.broadcasted_iota(I32, (t, t), 1) <= lax.broadcasted_iota(I32, (t, t), 0)
    m = jnp.maximum(jnp.max(jnp.where(own, s, NEG_BIG), axis=1, keepdims=True),
                    jnp.max(jnp.where(causal, sn, NEG_BIG), axis=1, keepdims=True))
    p = jnp.where(own, jnp.exp(s - m), 0.0).astype(BF16)
    pn = jnp.where(causal, jnp.exp(sn - m), 0.0)
    l = jnp.sum(p.astype(F32), axis=1, keepdims=True) + jnp.sum(pn, axis=1, keepdims=True)
    o = _dot(pn, vn_ref[0, 0])
    for tq in range(t):
        o = o + _dot_nt(p[:, tq * per_q:(tq + 1) * per_q], pages_of(vbuf, tq))
    o_ref[0, 0] = o / l


def _moba_sample(q, k_new, v_new, cache_kt, cache_vt, sel, page_table, layer):
    n, h, t, dh = q.shape
    n_pages = t * MOBA_TOPK * PAGES_PER_BLOCK
    blk = pl.BlockSpec((1, 1, t, dh), lambda i, j, sl, pt: (i, j, 0, 0))
    anyspec = pl.BlockSpec(memory_space=pl.ANY)
    return pl.pallas_call(
        functools.partial(_moba_sample_kernel, layer=layer, pages_per_seq=page_table.shape[1]),
        grid_spec=pltpu.PrefetchScalarGridSpec(
            num_scalar_prefetch=2,
            grid=(n, h),
            in_specs=[blk, blk, blk, anyspec, anyspec],
            out_specs=blk,
            scratch_shapes=[pltpu.VMEM((2, n_pages, dh, PAGE_SIZE), F32),
                            pltpu.VMEM((2, n_pages, dh, PAGE_SIZE), F32),
                            pltpu.SemaphoreType.DMA((2, 2))]),
        out_shape=jax.ShapeDtypeStruct((n, h, t, dh), F32),
        compiler_params=_params("arbitrary", "arbitrary"),
        name="moba_sample",
    )(sel.reshape(-1), page_table.reshape(-1), q, k_new, v_new, cache_kt, cache_vt)


def _merge_kernel(xp_ref, xs_ref, oap_ref, oas_ref, obp_ref, obs_ref, ga_ref, gb_ref, g1p_ref, g1s_ref,
                  shp_ref, shs_ref, scp_ref, scs_ref, nf_ref, wpa_ref, wpb_ref, wo_ref, wr_ref, br_ref,
                  x1_ref, h2_ref, eid_ref, gw_ref, *, n_first):
    pa = _dot(_pick_rows(n_first, oap_ref, oas_ref).astype(BF16), wpa_ref[...])
    pb = _dot(_pick_rows(n_first, obp_ref, obs_ref), wpb_ref[...])
    mix = jax.nn.sigmoid(ga_ref[...]) * pa + jax.nn.sigmoid(gb_ref[...]) * pb
    x1 = (_pick_rows(n_first, xp_ref, xs_ref)
          + _pick_rows(n_first, g1p_ref, g1s_ref) * _dot(mix.astype(BF16), wo_ref[...]))
    x1_ref[...] = x1
    h2 = x1 * lax.rsqrt(jnp.mean(x1 * x1, axis=-1, keepdims=True) + NORM_EPS) * nf_ref[...]
    h2 = h2 * (1.0 + _pick_rows(n_first, scp_ref, scs_ref)) + _pick_rows(n_first, shp_ref, shs_ref)
    for s in range(h2_ref.shape[1]):
        h2_ref[:, s, :] = h2[:, s * LANES:(s + 1) * LANES]

    logits = _dot3(h2, wr_ref[...]) + br_ref[...]
    lane = lax.broadcasted_iota(I32, logits.shape, 1)
    lanef = lane.astype(F32)
    is_g = lane < N_GROUPS
    gl = jnp.where(is_g, logits, -jnp.inf)
    gmax = jnp.max(gl, axis=1, keepdims=True)
    g_sel = jnp.min(jnp.where(gl == gmax, lanef, float(LANES)), axis=1, keepdims=True)
    p_group = 1.0 / jnp.sum(jnp.exp(gl - gmax), axis=1, keepdims=True)
    e_lo = N_GROUPS + g_sel * EXPERTS_PER_GROUP
    in_grp = (lanef >= e_lo) & (lanef < e_lo + EXPERTS_PER_GROUP)
    el = jnp.where(in_grp, logits, -jnp.inf)
    m1 = jnp.max(el, axis=1, keepdims=True)
    i1 = jnp.min(jnp.where(el == m1, lanef, float(LANES)), axis=1, keepdims=True)
    el2 = jnp.where(lanef == i1, -jnp.inf, el)
    m2 = jnp.max(el2, axis=1, keepdims=True)
    i2 = jnp.min(jnp.where(el2 == m2, lanef, float(LANES)), axis=1, keepdims=True)
    e2 = jnp.exp(m2 - m1)
    w1 = p_group / (1.0 + e2)
    w2 = p_group * e2 / (1.0 + e2)
    eid = jnp.where(lane == 0, i1 - N_GROUPS, jnp.where(lane == 1, i2 - N_GROUPS, 0.0))
    eid_ref[...] = eid.astype(I32)
    gw_ref[...] = jnp.where(lane == 0, w1, jnp.where(lane == 1, w2, 0.0))


def _merge(xp, xs, oa_p, oa_s, ob_p, ob_s, zg, mods, norm_ffn, wpa, wpb, wo, w_router, b_router,
           tiles_per_seq):
    d = xp.shape[1]
    n_first = xp.shape[0] // ROW_TILE
    m = xp.shape[0] + xs.shape[0]
    const = lambda i: (0, 0)
    row = lambda i: (i, 0)
    wspec = lambda w: pl.BlockSpec(w.shape, const, pipeline_mode=pl.Buffered(1))
    return pl.pallas_call(
        functools.partial(_merge_kernel, n_first=n_first),
        grid=(m // ROW_TILE,),
        in_specs=_two_group_specs(d, n_first) + _two_group_specs(A_WIDTH, n_first)
                 + _two_group_specs(B_WIDTH, n_first) + [
                  pl.BlockSpec((ROW_TILE, d), lambda i: (i, 0)),
                  pl.BlockSpec((ROW_TILE, d), lambda i: (i, 1))]
                 + _mod_specs(mods, 2, tiles_per_seq) + _mod_specs(mods, 3, tiles_per_seq)
                 + _mod_specs(mods, 4, tiles_per_seq) + [
                  pl.BlockSpec((1, d), const),
                  wspec(wpa), wspec(wpb), wspec(wo), wspec(w_router),
                  pl.BlockSpec((1, LANES), const)],
        out_specs=[pl.BlockSpec((ROW_TILE, d), row),
                   pl.BlockSpec((ROW_TILE, d // LANES, LANES), lambda i: (i, 0, 0)),
                   pl.BlockSpec((ROW_TILE, LANES), row),
                   pl.BlockSpec((ROW_TILE, LANES), row)],
        out_shape=[jax.ShapeDtypeStruct((m, d), F32),
                   jax.ShapeDtypeStruct((m, d // LANES, LANES), F32),
                   jax.ShapeDtypeStruct((m, LANES), I32),
                   jax.ShapeDtypeStruct((m, LANES), F32)],
        compiler_params=_params("parallel"),
        name="merge_router",
    )(xp, xs, oa_p, oa_s, ob_p, ob_s, zg, zg, *mods, *mods, *mods,
      norm_ffn.reshape(1, d), wpa, wpb, wo, w_router, b_router)


MOE_RING = 3


def _moe_kernel(be_ref, nu_ref, tok_ref, h_ref, wg_ref, wu_ref, wd_ref, y_ref, xbuf, sem, wg_s, wu_s, wd_s):
    i = pl.program_id(0)
    n_used = nu_ref[0]
    n_sub = h_ref.shape[1]
    ahead = MOE_RING - 1

    def row_copy(blk, r):
        return pltpu.make_async_copy(h_ref.at[tok_ref[blk * MOE_TILE + r]],
                                     xbuf.at[blk % MOE_RING, pl.ds(r * n_sub, n_sub), :],
                                     sem.at[blk % MOE_RING])

    def fetch(blk):
        for r in range(MOE_TILE):
            row_copy(blk, r).start(priority=r % 2)

    for b in range(ahead):
        @pl.when((i == 0) & (b < n_used))
        def _():
            fetch(b)

    @pl.when(i + ahead < n_used)
    def _():
        fetch(i + ahead)

    prev = be_ref[jnp.maximum(i - 1, 0)]
    fresh = (i == 0) | (be_ref[i] != prev)

    @pl.when(fresh)
    def _():
        wg_s[...] = wg_ref[...].astype(BF16)
        wu_s[...] = wu_ref[...].astype(BF16)
        wd_s[...] = wd_ref[...].astype(BF16)

    @pl.when(i < n_used)
    def _():
        for r in range(MOE_TILE):
            row_copy(i, r).wait()
        slot = i % MOE_RING
        xb = jnp.concatenate([xbuf[slot, pl.ds(s, MOE_TILE, stride=n_sub), :].astype(BF16)
                              for s in range(n_sub)], axis=1)
        hid = _silu(_dot(xb, wg_s[...])) * _dot(xb, wu_s[...])
        y_ref[...] = _dot(hid.astype(BF16), wd_s[...])

    @pl.when(i >= n_used)
    def _():
        y_ref[...] = jnp.zeros(y_ref.shape, F32)


def _moe(h, row_tok, block_expert, n_used, w_gate, w_up, w_down, layer):
    n_sub, lanes = h.shape[1], h.shape[2]
    d = n_sub * lanes
    n_rows = row_tok.shape[0]
    de = w_gate.shape[-1]
    return pl.pallas_call(
        _moe_kernel,
        grid_spec=pltpu.PrefetchScalarGridSpec(
            num_scalar_prefetch=3,
            grid=(n_rows // MOE_TILE,),
            in_specs=[pl.BlockSpec(memory_space=pl.ANY),
                      pl.BlockSpec((None, None, d, de), lambda i, be, nu, tk: (layer, be[i], 0, 0)),
                      pl.BlockSpec((None, None, d, de), lambda i, be, nu, tk: (layer, be[i], 0, 0)),
                      pl.BlockSpec((None, None, de, d), lambda i, be, nu, tk: (layer, be[i], 0, 0))],
            out_specs=pl.BlockSpec((MOE_TILE, d), lambda i, be, nu, tk: (i, 0)),
            scratch_shapes=[pltpu.VMEM((MOE_RING, MOE_TILE * n_sub, lanes), F32),
                            pltpu.SemaphoreType.DMA((MOE_RING,)),
                            pltpu.VMEM((d, de), BF16), pltpu.VMEM((d, de), BF16), pltpu.VMEM((de, d), BF16)]),
        out_shape=jax.ShapeDtypeStruct((n_rows, d), F32),
        compiler_params=_params("arbitrary"),
        name="moe_experts",
    )(block_expert, n_used, row_tok, h, w_gate, w_up, w_down)


def _final_kernel(x1_ref, y0_ref, y1_ref, gw_ref, g2p_ref, g2s_ref, nf_ref, op_ref, os_ref, *, last, n_first):
    gw = gw_ref[...]
    f = gw[:, 0:1] * y0_ref[...] + gw[:, 1:2] * y1_ref[...]
    x2 = x1_ref[...] + _pick_rows(n_first, g2p_ref, g2s_ref) * f
    if last:
        x2 = x2 * lax.rsqrt(jnp.mean(x2 * x2, axis=-1, keepdims=True) + NORM_EPS) * nf_ref[...]
    i = pl.program_id(0)

    @pl.when(i < n_first)
    def _():
        op_ref[...] = x2

    @pl.when(i >= n_first)
    def _():
        os_ref[...] = x2


def _final(x1, y0, y1, gw, mods, norm_final, tiles_per_seq, last, n_first):
    m, d = x1.shape
    row = lambda i: (i, 0)
    rs = pl.BlockSpec((ROW_TILE, d), row)
    return pl.pallas_call(
        functools.partial(_final_kernel, last=last, n_first=n_first),
        grid=(m // ROW_TILE,),
        in_specs=[rs, rs, rs, pl.BlockSpec((ROW_TILE, LANES), row)] + _mod_specs(mods, 5, tiles_per_seq) + [
                  pl.BlockSpec((1, d), lambda i: (0, 0))],
        out_specs=_two_group_specs(d, n_first),
        out_shape=[jax.ShapeDtypeStruct((n_first * ROW_TILE, d), F32),
                   jax.ShapeDtypeStruct((m - n_first * ROW_TILE, d), F32)],
        compiler_params=_params("arbitrary"),
        name="combine_final",
    )(x1, y0, y1, gw, *mods, norm_final.reshape(1, d))


def _dispatch(eid):
    m = eid.shape[0]
    n_assign = m * EXPERT_TOPK
    flat_e = eid.reshape(-1)
    flat_tok = jnp.repeat(jnp.arange(m, dtype=I32), EXPERT_TOPK)
    onehot = (flat_e[:, None] == jnp.arange(N_EXPERTS, dtype=I32)[None, :]).astype(I32)
    csum = jnp.cumsum(onehot, axis=0)
    rank = jnp.take_along_axis(csum, flat_e[:, None], axis=1)[:, 0] - 1
    counts = csum[-1]
    padded = (counts + MOE_TILE - 1) // MOE_TILE * MOE_TILE
    pad_end = jnp.cumsum(padded)
    pad_start = pad_end - padded
    dest = (pad_start[flat_e] + rank).astype(I32)
    n_blocks = -(-(n_assign + N_EXPERTS * (MOE_TILE - 1)) // MOE_TILE)
    row_tok = jnp.zeros((n_blocks * MOE_TILE,), I32).at[dest].set(flat_tok)
    blk_start = jnp.arange(n_blocks, dtype=I32) * MOE_TILE
    block_expert = jnp.minimum(jnp.sum((pad_end[None, :] <= blk_start[:, None]).astype(I32), axis=1),
                               N_EXPERTS - 1).astype(I32)
    n_used = (pad_end[-1:] // MOE_TILE).astype(I32)
    return dest.reshape(m, EXPERT_TOPK), row_tok, block_expert, n_used


def _mod_tiles(mod, n_prompt, reps):
    mod_p = mod[:n_prompt].transpose(1, 0, 2)[:, :, None, :]
    mod_s = jnp.repeat(mod[n_prompt:], reps, axis=0).transpose(1, 0, 2)
    return mod_p, mod_s


def kernel(x_prompt, x_sample, c_prompt, c_sample, cache_k, cache_v, state_hgrn, page_table,
           w_ada, b_ada, norm_mix, norm_ffn, w_in, hgrn_lb_logits, hgrn_norm,
           w_proj_a, w_proj_b, w_out, w_group, b_group, w_expert_router, b_expert_router,
           w_gate, w_up, w_down, norm_final):
    bp, tp, d = x_prompt.shape
    bs, ts, _ = x_sample.shape
    mp_rows, ms_rows = bp * tp, bs * ts
    assert ms_rows == ROW_TILE and tp % ROW_TILE == 0
    m = mp_rows + ms_rows
    tiles_per_seq = tp // ROW_TILE
    n_full = PAST_LEN // MOBA_BLOCK
    assert PAST_LEN % MOBA_BLOCK == 0 and n_full >= MOBA_TOPK

    xp, xs = x_prompt.reshape(mp_rows, d), x_sample.reshape(ms_rows, d)
    c_all = jnp.concatenate([c_prompt, c_sample, jnp.zeros((-(bp + bs) % 8, d), F32)], axis=0)
    lb_all = jnp.cumsum(jax.nn.softmax(hgrn_lb_logits.astype(F32), axis=0), axis=0)
    cache_kt = cache_k.transpose(0, 1, 2, 4, 3)
    cache_vt = cache_v.transpose(0, 1, 2, 4, 3)
    kp_l, vp_l, sp_l, ks_l, vs_l, ss_l = [], [], [], [], [], []

    for l in range(DEPTH):
        mod = _modulation(c_all, w_ada[l], b_ada[l])[:bp + bs].reshape(bp + bs, 6, d)
        mt = _mod_tiles(mod, bp, ts)
        za, zb, zg = _inproj(xp, xs, mt, norm_mix[l], w_in[l], tiles_per_seq)

        oa_p, st_p = _hgrn(za, lb_all[l], hgrn_norm[l], 0, bp, tp, ROW_TILE, HGRN_CHUNK)
        oa_s, st_s = _hgrn(za, lb_all[l], hgrn_norm[l], mp_rows, bs, ts, ts, ts, s0=state_hgrn[l])

        q_p, k_p, v_p = _rope_split(zb, 0, bp, tp, 512, jnp.arange(tp))
        q_s, k_s, v_s = _rope_split(zb, mp_rows, bs, ts, ts, PAST_LEN + jnp.arange(ts))
        ob_p = _moba_prompt(q_p, k_p, v_p)
        sel = _select_blocks(q_s, cache_kt, page_table[:, :n_full * PAGES_PER_BLOCK], l)[..., :MOBA_TOPK]
        ob_s = _moba_sample(q_s, k_s, v_s, cache_kt, cache_vt, sel, page_table, l)
        ob_s = ob_s.transpose(0, 2, 1, 3).reshape(ms_rows, B_WIDTH).astype(BF16)

        w_router = jnp.concatenate(
            [w_group[l], w_expert_router[l].transpose(1, 0, 2).reshape(d, N_EXPERTS),
             jnp.zeros((d, LANES - N_GROUPS - N_EXPERTS), F32)], axis=1)
        b_router = jnp.concatenate(
            [b_group[l], b_expert_router[l].reshape(-1),
             jnp.zeros((LANES - N_GROUPS - N_EXPERTS,), F32)]).reshape(1, LANES)
        x1, h2, eid, gw = _merge(xp, xs, oa_p, oa_s, ob_p, ob_s, zg, mt, norm_ffn[l], w_proj_a[l].astype(BF16),
                                 w_proj_b[l].astype(BF16), w_out[l].astype(BF16), w_router, b_router,
                                 tiles_per_seq)

        dest, row_tok, block_expert, n_used = _dispatch(eid[:, :EXPERT_TOPK])
        ys = _moe(h2, row_tok, block_expert, n_used, w_gate, w_up, w_down, l)
        xp, xs = _final(x1, ys[dest[:, 0]], ys[dest[:, 1]], gw, mt, norm_final, tiles_per_seq,
                        l == DEPTH - 1, mp_rows // ROW_TILE)

        kp_l.append(k_p); vp_l.append(v_p); sp_l.append(st_p)
        ks_l.append(k_s); vs_l.append(v_s); ss_l.append(st_s)

    y_prompt = xp.reshape(bp, tp, d)
    y_sample = xs.reshape(bs, ts, d)
    return (y_prompt, y_sample, jnp.stack(kp_l), jnp.stack(vp_l), jnp.stack(sp_l),
            jnp.stack(ks_l), jnp.stack(vs_l), jnp.stack(ss_l))
```

```python
import functools

import numpy as np
import jax
import jax.numpy as jnp
from jax import lax
from jax.experimental import pallas as pl
from jax.experimental.pallas import tpu as pltpu

F32 = jnp.float32
BF16 = jnp.bfloat16
I32 = jnp.int32

D_MODEL = 1024
DEPTH = 1
PAST_LEN = 16384
PAGE_SIZE = 128
A_HEADS = 4
A_KDIM = 128
A_VDIM = 128
A_WIDTH = A_HEADS * A_KDIM
B_HEADS = 8
B_HEAD_DIM = 64
B_WIDTH = B_HEADS * B_HEAD_DIM
MOBA_BLOCK = 256
MOBA_TOPK = 3
ROT_DIM = B_HEAD_DIM // 4
ROPE_THETA = 500000.0
N_GROUPS = 4
EXPERTS_PER_GROUP = 8
N_EXPERTS = N_GROUPS * EXPERTS_PER_GROUP
EXPERT_TOPK = 2
D_EXPERT = D_MODEL // 2
NORM_EPS = 1e-6

PAGES_PER_BLOCK = MOBA_BLOCK // PAGE_SIZE
ROW_TILE = 256
HGRN_CHUNK = 16
MOE_TILE = 256
LANES = 128
NEG_BIG = -1e30
LOG2_E = 1.4426950408889634
VMEM_LIMIT_BYTES = 52 * 1024 * 1024


def _params(*sem):
    return pltpu.CompilerParams(dimension_semantics=sem, vmem_limit_bytes=VMEM_LIMIT_BYTES)


def _dot(a, b):
    return jnp.dot(a, b, preferred_element_type=F32)


def _dot_nt(a, b):
    return lax.dot_general(a, b, (((1,), (1,)), ((), ())), preferred_element_type=F32)


def _dot_tn(a, b):
    return lax.dot_general(a, b, (((0,), (0,)), ((), ())), preferred_element_type=F32)


def _split(a):
    hi = a.astype(BF16)
    return hi, (a - hi.astype(F32)).astype(BF16)


def _dot3(a, b, dot=_dot):
    ah, al = _split(a)
    bh, bl = _split(b)
    return dot(ah, bh) + (dot(ah, bl) + dot(al, bh))


def _silu(x):
    return x * jax.nn.sigmoid(x)


def _mod_kernel(c_ref, w_ref, b_ref, o_ref):
    o_ref[...] = _dot3(_silu(c_ref[...]), w_ref[...]) + b_ref[...]


def _modulation(c_all, w, b):
    n = c_all.shape[0]
    d, dout = w.shape
    return pl.pallas_call(
        _mod_kernel,
        grid=(dout // d,),
        in_specs=[pl.BlockSpec((n, d), lambda j: (0, 0)),
                  pl.BlockSpec((d, d), lambda j: (0, j)),
                  pl.BlockSpec((1, d), lambda j: (0, j))],
        out_specs=pl.BlockSpec((n, d), lambda j: (0, j)),
        out_shape=jax.ShapeDtypeStruct((n, dout), F32),
        compiler_params=_params("parallel"),
        name="modulation",
    )(c_all, w, b.reshape(1, dout))


def _pick_rows(n_first, first_ref, second_ref):
    return jnp.where(pl.program_id(0) < n_first, first_ref[...], second_ref[...])


def _two_group_specs(width, n_first):
    return [pl.BlockSpec((ROW_TILE, width), lambda i: (jnp.minimum(i, n_first - 1), 0)),
            pl.BlockSpec((ROW_TILE, width), lambda i: (0, 0))]


def _mod_specs(mods, k, tiles_per_seq):
    mod_p, mod_s = mods
    n_seq, d = mod_p.shape[1], mod_p.shape[3]
    return [pl.BlockSpec((None, None, 1, d), lambda i: (k, jnp.minimum(i // tiles_per_seq, n_seq - 1), 0, 0)),
            pl.BlockSpec((None, ROW_TILE, d), lambda i: (k, 0, 0))]


def _inproj_kernel(xp_ref, xs_ref, shp_ref, shs_ref, scp_ref, scs_ref, g_ref, w_ref,
                   za_ref, zb_ref, zg_ref, *, n_first):
    x = _pick_rows(n_first, xp_ref, xs_ref)
    h = x * lax.rsqrt(jnp.mean(x * x, axis=-1, keepdims=True) + NORM_EPS) * g_ref[...]
    h = h * (1.0 + _pick_rows(n_first, scp_ref, scs_ref)) + _pick_rows(n_first, shp_ref, shs_ref)
    hh = h.astype(BF16)
    wa = 4 * A_WIDTH
    for c in range(0, wa, 512):
        za_ref[:, c:c + 512] = _dot(hh, w_ref[:, c:c + 512])
    for c in range(0, 3 * B_WIDTH, 512):
        zb_ref[:, c:c + 512] = _dot(hh, w_ref[:, wa + c:wa + c + 512])
    wg = wa + 3 * B_WIDTH
    for c in range(0, 2 * D_MODEL, 512):
        zg_ref[:, c:c + 512] = _dot(hh, w_ref[:, wg + c:wg + c + 512])


def _inproj(xp, xs, mods, norm_g, w_in, tiles_per_seq):
    d = xp.shape[1]
    n_first = xp.shape[0] // ROW_TILE
    m = xp.shape[0] + xs.shape[0]
    w_hi = w_in.astype(BF16)
    wa = 4 * A_WIDTH
    const = lambda i: (0, 0)
    row = lambda i: (i, 0)
    return pl.pallas_call(
        functools.partial(_inproj_kernel, n_first=n_first),
        grid=(m // ROW_TILE,),
        in_specs=_two_group_specs(d, n_first) + _mod_specs(mods, 0, tiles_per_seq)
                 + _mod_specs(mods, 1, tiles_per_seq) + [
                  pl.BlockSpec((1, d), const),
                  pl.BlockSpec(w_hi.shape, const, pipeline_mode=pl.Buffered(1))],
        out_specs=[pl.BlockSpec((ROW_TILE, wa), row),
                   pl.BlockSpec((ROW_TILE, 3 * B_WIDTH), row),
                   pl.BlockSpec((ROW_TILE, 2 * D_MODEL), row)],
        out_shape=[jax.ShapeDtypeStruct((m, wa), F32),
                   jax.ShapeDtypeStruct((m, 3 * B_WIDTH), F32),
                   jax.ShapeDtypeStruct((m, 2 * D_MODEL), F32)],
        compiler_params=_params("parallel"),
        name="inproj",
    )(xp, xs, *mods, *mods, norm_g.reshape(1, d), w_hi)


def _rope_kernel(q_ref, k_ref, v_ref, c_ref, s1_ref, s2_ref, qo_ref, ko_ref, vo_ref):
    cos, s1, s2 = c_ref[...], s1_ref[...], s2_ref[...]
    half = ROT_DIM // 2

    def rope(x):
        up = pltpu.roll(x, B_WIDTH - half, axis=1)
        dn = pltpu.roll(x, half, axis=1)
        return x * cos + up * s1 + dn * s2

    q = rope(q_ref[...]) * (B_HEAD_DIM ** -0.5)
    k = rope(k_ref[...])
    v = v_ref[...]
    for h in range(B_HEADS):
        ls = slice(h * B_HEAD_DIM, (h + 1) * B_HEAD_DIM)
        qo_ref[0, h] = q[:, ls]
        ko_ref[0, h] = k[:, ls]
        vo_ref[0, h] = v[:, ls]


def _rope_tables(pos):
    half = ROT_DIM // 2
    inv = jnp.power(ROPE_THETA, -(jnp.arange(half, dtype=F32) * 2.0 / ROT_DIM))
    ang = pos.astype(F32)[:, None] * inv[None, :]
    cos, sin = jnp.cos(ang), jnp.sin(ang)
    t = pos.shape[0]
    rest = B_HEAD_DIM - ROT_DIM
    c = jnp.concatenate([cos, cos, jnp.ones((t, rest), F32)], axis=-1)
    s1 = jnp.concatenate([-sin, jnp.zeros((t, half + rest), F32)], axis=-1)
    s2 = jnp.concatenate([jnp.zeros((t, half), F32), sin, jnp.zeros((t, rest), F32)], axis=-1)
    return [jnp.tile(a, (1, B_HEADS)) for a in (c, s1, s2)]


def _rope_split(zb, row0, n, t, tile, pos):
    tabs = _rope_tables(pos)
    tps = t // tile
    blk0 = row0 // tile
    zmap = lambda c: (lambda b, s: (blk0 + b * tps + s, c))
    tmap = lambda b, s: (s, 0)
    omap = lambda b, s: (b, 0, s, 0)
    oshape = jax.ShapeDtypeStruct((n, B_HEADS, t, B_HEAD_DIM), F32)
    ospec = pl.BlockSpec((1, B_HEADS, tile, B_HEAD_DIM), omap)
    return pl.pallas_call(
        _rope_kernel,
        grid=(n, tps),
        in_specs=[pl.BlockSpec((tile, B_WIDTH), zmap(0)),
                  pl.BlockSpec((tile, B_WIDTH), zmap(1)),
                  pl.BlockSpec((tile, B_WIDTH), zmap(2)),
                  pl.BlockSpec((tile, B_WIDTH), tmap),
                  pl.BlockSpec((tile, B_WIDTH), tmap),
                  pl.BlockSpec((tile, B_WIDTH), tmap)],
        out_specs=[ospec, ospec, ospec],
        out_shape=[oshape, oshape, oshape],
        compiler_params=_params("parallel", "parallel"),
        name="rope_split",
    )(zb, zb, zb, *tabs)


def _hgrn_kernel(*refs, chunk, has_s0):
    aq_ref, af_ref, ai_ref, ag_ref, lb_ref, gain_ref = refs[:6]
    rest = refs[6:]
    s0_ref = None
    if has_s0:
        s0_ref, rest = rest[0], rest[1:]
    o_ref, so_ref, st_ref, q_s, b_s, k_s = rest
    t = pl.program_id(1)
    tb = aq_ref.shape[0]

    @pl.when(t == 0)
    def _():
        for h in range(A_HEADS):
            if has_s0:
                st_ref[h] = s0_ref[0, h].T
            else:
                st_ref[h] = jnp.zeros((A_VDIM, A_KDIM), F32)

    lb = lb_ref[...]
    f = lb + (1.0 - lb) * jax.nn.sigmoid(af_ref[...])
    logf = jnp.log(f)
    q_s[...] = _silu(aq_ref[...])
    k_s[...] = 1.0 - f
    row = lax.broadcasted_iota(I32, logf.shape, 0) & (chunk - 1)
    b = logf
    sh = 1
    while sh < chunk:
        b = b + jnp.where(row >= sh, pltpu.roll(b, sh, axis=0), 0.0)
        sh *= 2
    b_s[...] = b
    rowc = lax.broadcasted_iota(I32, (chunk, A_KDIM), 0)

    def one_chunk(ci, carry):
        r0 = pl.multiple_of(ci * chunk, chunk)
        rs = pl.ds(r0, chunk)
        for h in range(A_HEADS):
            ls = slice(h * A_KDIM, (h + 1) * A_KDIM)
            qc, bc, kc, vc = q_s[rs, ls], b_s[rs, ls], k_s[rs, ls], ai_ref[rs, ls]
            st = st_ref[h]
            bl = bc[chunk - 1:chunk, :]
            o = _dot_nt((qc * jnp.exp(bc)).astype(BF16), st.astype(BF16))
            grp = 8
            parts = [o[g * grp:(g + 1) * grp] for g in range(chunk // grp)]
            for s in range(chunk):
                for g in range(s // grp, chunk // grp):
                    rows = slice(g * grp, (g + 1) * grp)
                    diff = bc[rows] - bc[s:s + 1, :]
                    if g == s // grp:
                        diff = jnp.where(rowc[rows] >= s, diff, -jnp.inf)
                    e = jnp.exp(diff)
                    r = jnp.sum(qc[rows] * e * kc[s:s + 1, :], axis=1, keepdims=True)
                    parts[g] = parts[g] + r * vc[s:s + 1, :]
            o = jnp.concatenate(parts, axis=0) if len(parts) > 1 else parts[0]
            kp = kc * jnp.exp(bl - bc)
            st_ref[h] = st * jnp.exp(bl) + _dot_tn(vc.astype(BF16), kp.astype(BF16))
            o_ref[rs, ls] = o
        return carry

    n_chunks = tb // chunk
    lax.fori_loop(0, n_chunks, one_chunk, 0, unroll=4 if n_chunks % 4 == 0 else 1)

    gain = gain_ref[...]
    for h in range(A_HEADS):
        ls = slice(h * A_VDIM, (h + 1) * A_VDIM)
        oh = o_ref[:, ls]
        y = oh * lax.rsqrt(jnp.mean(oh * oh, axis=-1, keepdims=True) + NORM_EPS) * gain
        o_ref[:, ls] = y * _silu(ag_ref[:, ls])

    @pl.when(t == pl.num_programs(1) - 1)
    def _():
        for h in range(A_HEADS):
            so_ref[0, h] = st_ref[h].T


def _hgrn(za, lb, gain, row0, n, t, tile, chunk, s0=None):
    tps = t // tile
    blk0 = row0 // tile
    zmap = lambda c: (lambda b, s: (blk0 + b * tps + s, c))
    const = lambda b, s: (0, 0)
    in_specs = [pl.BlockSpec((tile, A_WIDTH), zmap(c)) for c in range(4)]
    in_specs += [pl.BlockSpec((1, A_WIDTH), const), pl.BlockSpec((1, A_VDIM), const)]
    args = [za, za, za, za, lb.reshape(1, A_WIDTH), gain.reshape(1, A_VDIM)]
    if s0 is not None:
        in_specs.append(pl.BlockSpec((1, A_HEADS, A_KDIM, A_VDIM), lambda b, s: (b, 0, 0, 0)))
        args.append(s0)
    return pl.pallas_call(
        functools.partial(_hgrn_kernel, chunk=chunk, has_s0=s0 is not None),
        grid=(n, tps),
        in_specs=in_specs,
        out_specs=[pl.BlockSpec((tile, A_WIDTH), lambda b, s: (b * tps + s, 0)),
                   pl.BlockSpec((1, A_HEADS, A_KDIM, A_VDIM), lambda b, s: (b, 0, 0, 0))],
        out_shape=[jax.ShapeDtypeStruct((n * t, A_WIDTH), F32),
                   jax.ShapeDtypeStruct((n, A_HEADS, A_KDIM, A_VDIM), F32)],
        scratch_shapes=[pltpu.VMEM((A_HEADS, A_VDIM, A_KDIM), F32),
                        pltpu.VMEM((tile, A_WIDTH), F32),
                        pltpu.VMEM((tile, A_WIDTH), F32),
                        pltpu.VMEM((tile, A_WIDTH), F32)],
        compiler_params=_params("parallel", "arbitrary"),
        name="hgrn2",
    )(*args)


MOBA_CHUNK_BLOCKS = 4
MOBA_HEADS_PER_STEP = 2
MOBA_QUERY_BLOCKS = 2


def _moba_prompt_kernel(q_ref, k_ref, v_ref, o_ref, km_ref, ka_ref, vt_ref, eye_ref):
    j = pl.program_id(2)
    blk = MOBA_BLOCK
    hp, nq, dh = q_ref.shape[1], q_ref.shape[2], q_ref.shape[3]
    n_chunks, kc = ka_ref.shape[0], ka_ref.shape[1]
    n_blk = n_chunks * MOBA_CHUNK_BLOCKS
    eye = (lax.broadcasted_iota(I32, (dh, dh), 0) == lax.broadcasted_iota(I32, (dh, dh), 1)).astype(BF16)

    @pl.when(j == 0)
    def _():
        eye_ref[...] = (lax.broadcasted_iota(I32, (blk, blk), 0)
                        == lax.broadcasted_iota(I32, (blk, blk), 1)).astype(BF16)
        for c in range(n_chunks):
            rows = slice(c * kc, (c + 1) * kc)
            key_blk = lax.broadcasted_iota(I32, (kc, n_blk), 0) // blk + c * MOBA_CHUNK_BLOCKS
            onehot = key_blk == lax.broadcasted_iota(I32, (kc, n_blk), 1)
            ka_ref[c, :, hp * dh:] = jnp.where(onehot, 1.0, 0.0).astype(BF16)
            for hh in range(hp):
                ka_ref[c, :, hh * dh:(hh + 1) * dh] = k_ref[0, hh, rows, :].astype(BF16)
                vt_ref[hh, c] = _dot_nt(eye, v_ref[0, hh, rows, :].astype(BF16)).astype(BF16)
        for hh in range(hp):
            for i in range(n_blk):
                km_ref[hh, i:i + 1, :] = jnp.mean(k_ref[0, hh, i * blk:(i + 1) * blk, :], axis=0, keepdims=True)

    bidx = lax.broadcasted_iota(I32, (n_blk, nq), 0)
    own = j * (nq // blk) + lax.broadcasted_iota(I32, (n_blk, nq), 1) // blk
    valid = bidx < own
    q_rows, biases = [], []
    for hh in range(hp):
        q = q_ref[0, hh]
        sc = jnp.where(valid, _dot3(km_ref[hh], q, _dot_nt), -jnp.inf)
        rank = jnp.zeros((n_blk, nq), I32)
        for i in range(n_blk - 1):
            row = sc[i:i + 1, :]
            rank = rank + jnp.where(row > sc, 1, jnp.where(row == sc, jnp.where(bidx > i, 1, 0), 0))
        biases.append(jnp.where(valid, jnp.where(rank < MOBA_TOPK, 0.0, NEG_BIG),
                                jnp.where(bidx == own, 0.0, NEG_BIG)))
        q_t = _dot_nt(eye, (q * LOG2_E).astype(BF16))
        zero = jnp.zeros((dh, nq), F32)
        q_rows.append(jnp.concatenate([q_t if g == hh else zero for g in range(hp)], axis=1))
    qa = jnp.concatenate(q_rows + [jnp.concatenate(biases, axis=1)], axis=0).astype(BF16)

    def scores(c):
        return jnp.concatenate([_dot(ka_ref[c, :kc // 2], qa), _dot(ka_ref[c, kc // 2:], qa)], axis=0)

    def weighted_values(c, p):
        pb = p.astype(BF16)
        return jnp.concatenate([_dot(vt_ref[hh, c], pb[:, hh * nq:(hh + 1) * nq]) for hh in range(hp)], axis=1)

    cj = (j * nq) // kc
    key_pos = lax.broadcasted_iota(I32, (kc, hp * nq), 0) + cj * kc
    q_pos = (lax.broadcasted_iota(I32, (kc, hp * nq), 1) & (nq - 1)) + j * nq
    s = jnp.where(key_pos <= q_pos, scores(cj), NEG_BIG)
    m = jnp.max(s, axis=0, keepdims=True)
    p = jnp.exp2(s - m)
    init = (m, jnp.sum(p, axis=0, keepdims=True), weighted_values(cj, p))

    def past_chunk(c, carry):
        m, l, acc = carry
        s = scores(c)
        m_new = jnp.maximum(m, jnp.max(s, axis=0, keepdims=True))
        p = jnp.exp2(s - m_new)
        alpha = jnp.exp2(m - m_new)
        return m_new, alpha * l + jnp.sum(p, axis=0, keepdims=True), alpha * acc + weighted_values(c, p)

    _, l, acc = lax.fori_loop(0, cj, past_chunk, init)
    o_t = (acc / l).astype(BF16)
    o_ref[...] = jnp.concatenate(
        [jnp.concatenate([_dot_nt(eye_ref[...], o_t[:, hh * nq + u * blk:hh * nq + (u + 1) * blk])
                          for u in range(nq // blk)], axis=0) for hh in range(hp)], axis=1).astype(BF16)


def _moba_prompt(q, k, v):
    b, h, s, dh = q.shape
    hp = MOBA_HEADS_PER_STEP
    n_blk = s // MOBA_BLOCK
    nq = MOBA_QUERY_BLOCKS * MOBA_BLOCK
    assert n_blk % MOBA_CHUNK_BLOCKS == 0 and h % hp == 0 and hp * dh == LANES
    assert MOBA_CHUNK_BLOCKS % MOBA_QUERY_BLOCKS == 0
    n_chunks = n_blk // MOBA_CHUNK_BLOCKS
    n_tiles = s // nq
    kc = MOBA_CHUNK_BLOCKS * MOBA_BLOCK
    full = pl.BlockSpec((1, hp, s, dh), lambda bi, hi, j: (bi, hi, 0, 0))
    return pl.pallas_call(
        _moba_prompt_kernel,
        grid=(b, h // hp, n_tiles),
        in_specs=[pl.BlockSpec((1, hp, nq, dh), lambda bi, hi, j: (bi, hi, j, 0)), full, full],
        out_specs=pl.BlockSpec((nq, hp * dh), lambda bi, hi, j: (bi * n_tiles + j, hi)),
        out_shape=jax.ShapeDtypeStruct((b * s, h * dh), BF16),
        scratch_shapes=[pltpu.VMEM((hp, n_blk, dh), F32),
                        pltpu.VMEM((n_chunks, kc, hp * dh + n_blk), BF16),
                        pltpu.VMEM((hp, n_chunks, dh, kc), BF16),
                        pltpu.VMEM((MOBA_BLOCK, MOBA_BLOCK), BF16)],
        compiler_params=_params("parallel", "parallel", "arbitrary"),
        name="moba_prompt",
    )(q, k, v)


SELECT_PAGES_PER_STEP = 16


def _select_kernel(pt_ref, q_ref, ck_ref, o_ref, buf, pm_ref, sem, *, layer, pages_per_seq):
    n_grp = pl.num_programs(1)
    g = pl.program_id(1)
    step = pl.program_id(0) * n_grp + g
    n_steps = pl.num_programs(0) * n_grp
    slot = step % 2
    pps, n_heads, dh, ps = buf.shape[1], buf.shape[2], buf.shape[3], buf.shape[4]
    t = q_ref.shape[2]

    def page_copy(st, sl, i, pg):
        return pltpu.make_async_copy(ck_ref.at[layer, pg], buf.at[sl, i], sem.at[sl])

    def fetch(st, sl):
        base = (st // n_grp) * pages_per_seq + (st % n_grp) * pps
        for i in range(pps):
            page_copy(st, sl, i, pt_ref[base + i]).start(priority=i % 2)

    @pl.when(step == 0)
    def _():
        fetch(step, slot)

    @pl.when(step + 1 < n_steps)
    def _():
        fetch(step + 1, 1 - slot)

    for i in range(pps):
        page_copy(step, slot, i, 0).wait()

    bps = pps // PAGES_PER_BLOCK
    diag = lax.broadcasted_iota(I32, (dh, ps), 0) == lax.broadcasted_iota(I32, (dh, ps), 1)
    blocks = []
    for b in range(bps):
        xb = buf[slot, b * PAGES_PER_BLOCK]
        for pi in range(1, PAGES_PER_BLOCK):
            xb = xb + buf[slot, b * PAGES_PER_BLOCK + pi]
        blocks.append(xb)
    x = jnp.stack(blocks, axis=0)
    r = jnp.sum(x, axis=-1, keepdims=True) * (1.0 / (ps * PAGES_PER_BLOCK))
    bm_step = jnp.sum(jnp.where(diag, r, 0.0), axis=2)
    pm_ref[pl.ds(pl.multiple_of(g * bps * n_heads, bps * n_heads), bps * n_heads), :] = (
        bm_step.reshape(bps * n_heads, ps))

    @pl.when(g == n_grp - 1)
    def _():
        n_blk = pages_per_seq // PAGES_PER_BLOCK
        lane = lax.broadcasted_iota(I32, (t, n_blk), 1).astype(F32)
        for h in range(n_heads):
            bm = pm_ref[pl.ds(h, n_blk, stride=n_heads), :]
            sc = _dot3(q_ref[0, h], bm[:, :dh], _dot_nt)
            res = jnp.zeros((t, n_blk), F32)
            for k in range(MOBA_TOPK):
                m = jnp.max(sc, axis=1, keepdims=True)
                idx = jnp.min(jnp.where(sc == m, lane, float(n_blk)), axis=1, keepdims=True)
                res = jnp.where(lane == k, idx, res)
                sc = jnp.where(lane == idx, -jnp.inf, sc)
            o_ref[0, h] = res.astype(I32)


def _select_blocks(q, cache_kt, page_table, layer):
    n, h, t, dh = q.shape
    ps = cache_kt.shape[-1]
    pages_per_seq = page_table.shape[1]
    pps = SELECT_PAGES_PER_STEP
    assert pages_per_seq % pps == 0 and pages_per_seq % PAGES_PER_BLOCK == 0 and dh <= ps
    n_blk = pages_per_seq // PAGES_PER_BLOCK
    return pl.pallas_call(
        functools.partial(_select_kernel, layer=layer, pages_per_seq=pages_per_seq),
        grid_spec=pltpu.PrefetchScalarGridSpec(
            num_scalar_prefetch=1,
            grid=(n, pages_per_seq // pps),
            in_specs=[pl.BlockSpec((1, h, t, dh), lambda i, g, pt: (i, 0, 0, 0)),
                      pl.BlockSpec(memory_space=pl.ANY)],
            out_specs=pl.BlockSpec((1, h, t, n_blk), lambda i, g, pt: (i, 0, 0, 0)),
            scratch_shapes=[pltpu.VMEM((2, pps, h, dh, ps), F32),
                            pltpu.VMEM((n_blk * h, ps), F32),
                            pltpu.SemaphoreType.DMA((2,))]),
        out_shape=jax.ShapeDtypeStruct((n, h, t, n_blk), I32),
        compiler_params=_params("arbitrary", "arbitrary"),
        name="moba_select",
    )(page_table.reshape(-1), q, cache_kt)


def _moba_sample_kernel(sel_ref, pt_ref, q_ref, kn_ref, vn_ref, ck_ref, cv_ref, o_ref, kbuf, vbuf, sem, *,
                        layer, pages_per_seq):
    n_heads = pl.num_programs(1)
    step = pl.program_id(0) * n_heads + pl.program_id(1)
    n_steps = pl.num_programs(0) * n_heads
    t = q_ref.shape[2]
    n_pages = kbuf.shape[1]
    pages_per_q = n_pages // t
    slot = step % 2

    def page_copies(st, sl, i, pg):
        hh = st % n_heads
        return (pltpu.make_async_copy(ck_ref.at[layer, pg, hh], kbuf.at[sl, i], sem.at[0, sl]),
                pltpu.make_async_copy(cv_ref.at[layer, pg, hh], vbuf.at[sl, i], sem.at[1, sl]))

    def fetch(st, sl):
        pt_base = (st // n_heads) * pages_per_seq
        for b in range(n_pages // PAGES_PER_BLOCK):
            blk = sel_ref[st * (n_pages // PAGES_PER_BLOCK) + b]
            for pi in range(PAGES_PER_BLOCK):
                pg = pt_ref[pt_base + blk * PAGES_PER_BLOCK + pi]
                for thread, cp in enumerate(page_copies(st, sl, b * PAGES_PER_BLOCK + pi, pg)):
                    cp.start(priority=thread)

    @pl.when(step == 0)
    def _():
        fetch(step, slot)

    @pl.when(step + 1 < n_steps)
    def _():
        fetch(step + 1, 1 - slot)

    for i in range(n_pages):
        for cp in page_copies(step, slot, i, 0):
            cp.wait()

    q = q_ref[0, 0]
    qb = q.astype(BF16)
    def pages_of(buf_ref, tq):
        return jnp.concatenate([buf_ref[slot, tq * pages_per_q + i].astype(BF16) for i in range(pages_per_q)],
                               axis=1)

    s = jnp.concatenate([_dot(qb, pages_of(kbuf, tq)) for tq in range(t)], axis=1)
    per_q = pages_per_q * PAGE_SIZE
    col = lax.broadcasted_iota(I32, s.shape, 1)
    lo = lax.broadcasted_iota(I32, s.shape, 0) * per_q
    own = (col >= lo) & (col < lo + per_q)
    sn = _dot_nt(q, kn_ref[0, 0])
    causal = lax.broadcasted_iota(I32, (t, t), 1) <= lax.broadcasted_iota(I32, (t, t), 0)
    m = jnp.maximum(jnp.max(jnp.where(own, s, NEG_BIG), axis=1, keepdims=True),
                    jnp.max(jnp.where(causal, sn, NEG_BIG), axis=1, keepdims=True))
    p = jnp.where(own, jnp.exp(s - m), 0.0).astype(BF16)
    pn = jnp.where(causal, jnp.exp(sn - m), 0.0)
    l = jnp.sum(p.astype(F32), axis=1, keepdims=True) + jnp.sum(pn, axis=1, keepdims=True)
    o = _dot(pn, vn_ref[0, 0])
    for tq in range(t):
        o = o + _dot_nt(p[:, tq * per_q:(tq + 1) * per_q], pages_of(vbuf, tq))
    o_ref[0, 0] = o / l


def _moba_sample(q, k_new, v_new, cache_kt, cache_vt, sel, page_table, layer):
    n, h, t, dh = q.shape
    n_pages = t * MOBA_TOPK * PAGES_PER_BLOCK
    blk = pl.BlockSpec((1, 1, t, dh), lambda i, j, sl, pt: (i, j, 0, 0))
    anyspec = pl.BlockSpec(memory_space=pl.ANY)
    return pl.pallas_call(
        functools.partial(_moba_sample_kernel, layer=layer, pages_per_seq=page_table.shape[1]),
        grid_spec=pltpu.PrefetchScalarGridSpec(
            num_scalar_prefetch=2,
            grid=(n, h),
            in_specs=[blk, blk, blk, anyspec, anyspec],
            out_specs=blk,
            scratch_shapes=[pltpu.VMEM((2, n_pages, dh, PAGE_SIZE), F32),
                            pltpu.VMEM((2, n_pages, dh, PAGE_SIZE), F32),
                            pltpu.SemaphoreType.DMA((2, 2))]),
        out_shape=jax.ShapeDtypeStruct((n, h, t, dh), F32),
        compiler_params=_params("arbitrary", "arbitrary"),
        name="moba_sample",
    )(sel.reshape(-1), page_table.reshape(-1), q, k_new, v_new, cache_kt, cache_vt)


def _merge_kernel(xp_ref, xs_ref, oap_ref, oas_ref, obp_ref, obs_ref, ga_ref, gb_ref, g1p_ref, g1s_ref,
                  shp_ref, shs_ref, scp_ref, scs_ref, nf_ref, wpa_ref, wpb_ref, wo_ref, wr_ref, br_ref,
                  x1_ref, h2_ref, eid_ref, gw_ref, *, n_first):
    pa = _dot(_pick_rows(n_first, oap_ref, oas_ref).astype(BF16), wpa_ref[...])
    pb = _dot(_pick_rows(n_first, obp_ref, obs_ref), wpb_ref[...])
    mix = jax.nn.sigmoid(ga_ref[...]) * pa + jax.nn.sigmoid(gb_ref[...]) * pb
    x1 = (_pick_rows(n_first, xp_ref, xs_ref)
          + _pick_rows(n_first, g1p_ref, g1s_ref) * _dot(mix.astype(BF16), wo_ref[...]))
    x1_ref[...] = x1
    h2 = x1 * lax.rsqrt(jnp.mean(x1 * x1, axis=-1, keepdims=True) + NORM_EPS) * nf_ref[...]
    h2 = h2 * (1.0 + _pick_rows(n_first, scp_ref, scs_ref)) + _pick_rows(n_first, shp_ref, shs_ref)
    for s in range(h2_ref.shape[1]):
        h2_ref[:, s, :] = h2[:, s * LANES:(s + 1) * LANES]

    logits = _dot3(h2, wr_ref[...]) + br_ref[...]
    lane = lax.broadcasted_iota(I32, logits.shape, 1)
    lanef = lane.astype(F32)
    is_g = lane < N_GROUPS
    gl = jnp.where(is_g, logits, -jnp.inf)
    gmax = jnp.max(gl, axis=1, keepdims=True)
    g_sel = jnp.min(jnp.where(gl == gmax, lanef, float(LANES)), axis=1, keepdims=True)
    p_group = 1.0 / jnp.sum(jnp.exp(gl - gmax), axis=1, keepdims=True)
    e_lo = N_GROUPS + g_sel * EXPERTS_PER_GROUP
    in_grp = (lanef >= e_lo) & (lanef < e_lo + EXPERTS_PER_GROUP)
    el = jnp.where(in_grp, logits, -jnp.inf)
    m1 = jnp.max(el, axis=1, keepdims=True)
    i1 = jnp.min(jnp.where(el == m1, lanef, float(LANES)), axis=1, keepdims=True)
    el2 = jnp.where(lanef == i1, -jnp.inf, el)
    m2 = jnp.max(el2, axis=1, keepdims=True)
    i2 = jnp.min(jnp.where(el2 == m2, lanef, float(LANES)), axis=1, keepdims=True)
    e2 = jnp.exp(m2 - m1)
    w1 = p_group / (1.0 + e2)
    w2 = p_group * e2 / (1.0 + e2)
    eid = jnp.where(lane == 0, i1 - N_GROUPS, jnp.where(lane == 1, i2 - N_GROUPS, 0.0))
    eid_ref[...] = eid.astype(I32)
    gw_ref[...] = jnp.where(lane == 0, w1, jnp.where(lane == 1, w2, 0.0))


def _merge(xp, xs, oa_p, oa_s, ob_p, ob_s, zg, mods, norm_ffn, wpa, wpb, wo, w_router, b_router,
           tiles_per_seq):
    d = xp.shape[1]
    n_first = xp.shape[0] // ROW_TILE
    m = xp.shape[0] + xs.shape[0]
    const = lambda i: (0, 0)
    row = lambda i: (i, 0)
    wspec = lambda w: pl.BlockSpec(w.shape, const, pipeline_mode=pl.Buffered(1))
    return pl.pallas_call(
        functools.partial(_merge_kernel, n_first=n_first),
        grid=(m // ROW_TILE,),
        in_specs=_two_group_specs(d, n_first) + _two_group_specs(A_WIDTH, n_first)
                 + _two_group_specs(B_WIDTH, n_first) + [
                  pl.BlockSpec((ROW_TILE, d), lambda i: (i, 0)),
                  pl.BlockSpec((ROW_TILE, d), lambda i: (i, 1))]
                 + _mod_specs(mods, 2, tiles_per_seq) + _mod_specs(mods, 3, tiles_per_seq)
                 + _mod_specs(mods, 4, tiles_per_seq) + [
                  pl.BlockSpec((1, d), const),
                  wspec(wpa), wspec(wpb), wspec(wo), wspec(w_router),
                  pl.BlockSpec((1, LANES), const)],
        out_specs=[pl.BlockSpec((ROW_TILE, d), row),
                   pl.BlockSpec((ROW_TILE, d // LANES, LANES), lambda i: (i, 0, 0)),
                   pl.BlockSpec((ROW_TILE, LANES), row),
                   pl.BlockSpec((ROW_TILE, LANES), row)],
        out_shape=[jax.ShapeDtypeStruct((m, d), F32),
                   jax.ShapeDtypeStruct((m, d // LANES, LANES), F32),
                   jax.ShapeDtypeStruct((m, LANES), I32),
                   jax.ShapeDtypeStruct((m, LANES), F32)],
        compiler_params=_params("parallel"),
        name="merge_router",
    )(xp, xs, oa_p, oa_s, ob_p, ob_s, zg, zg, *mods, *mods, *mods,
      norm_ffn.reshape(1, d), wpa, wpb, wo, w_router, b_router)


MOE_RING = 3


def _moe_kernel(be_ref, nu_ref, tok_ref, h_ref, wg_ref, wu_ref, wd_ref, y_ref, xbuf, sem, wg_s, wu_s, wd_s):
    i = pl.program_id(0)
    n_used = nu_ref[0]
    n_sub = h_ref.shape[1]
    ahead = MOE_RING - 1

    def row_copy(blk, r):
        return pltpu.make_async_copy(h_ref.at[tok_ref[blk * MOE_TILE + r]],
                                     xbuf.at[blk % MOE_RING, pl.ds(r * n_sub, n_sub), :],
                                     sem.at[blk % MOE_RING])

    def fetch(blk):
        for r in range(MOE_TILE):
            row_copy(blk, r).start(priority=r % 2)

    for b in range(ahead):
        @pl.when((i == 0) & (b < n_used))
        def _():
            fetch(b)

    @pl.when(i + ahead < n_used)
    def _():
        fetch(i + ahead)

    prev = be_ref[jnp.maximum(i - 1, 0)]
    fresh = (i == 0) | (be_ref[i] != prev)

    @pl.when(fresh)
    def _():
        wg_s[...] = wg_ref[...].astype(BF16)
        wu_s[...] = wu_ref[...].astype(BF16)
        wd_s[...] = wd_ref[...].astype(BF16)

    @pl.when(i < n_used)
    def _():
        for r in range(MOE_TILE):
            row_copy(i, r).wait()
        slot = i % MOE_RING
        xb = jnp.concatenate([xbuf[slot, pl.ds(s, MOE_TILE, stride=n_sub), :].astype(BF16)
                              for s in range(n_sub)], axis=1)
        hid = _silu(_dot(xb, wg_s[...])) * _dot(xb, wu_s[...])
        y_ref[...] = _dot(hid.astype(BF16), wd_s[...])

    @pl.when(i >= n_used)
    def _():
        y_ref[...] = jnp.zeros(y_ref.shape, F32)


def _moe(h, row_tok, block_expert, n_used, w_gate, w_up, w_down, layer):
    n_sub, lanes = h.shape[1], h.shape[2]
    d = n_sub * lanes
    n_rows = row_tok.shape[0]
    de = w_gate.shape[-1]
    return pl.pallas_call(
        _moe_kernel,
        grid_spec=pltpu.PrefetchScalarGridSpec(
            num_scalar_prefetch=3,
            grid=(n_rows // MOE_TILE,),
            in_specs=[pl.BlockSpec(memory_space=pl.ANY),
                      pl.BlockSpec((None, None, d, de), lambda i, be, nu, tk: (layer, be[i], 0, 0)),
                      pl.BlockSpec((None, None, d, de), lambda i, be, nu, tk: (layer, be[i], 0, 0)),
                      pl.BlockSpec((None, None, de, d), lambda i, be, nu, tk: (layer, be[i], 0, 0))],
            out_specs=pl.BlockSpec((MOE_TILE, d), lambda i, be, nu, tk: (i, 0)),
            scratch_shapes=[pltpu.VMEM((MOE_RING, MOE_TILE * n_sub, lanes), F32),
                            pltpu.SemaphoreType.DMA((MOE_RING,)),
                            pltpu.VMEM((d, de), BF16), pltpu.VMEM((d, de), BF16), pltpu.VMEM((de, d), BF16)]),
        out_shape=jax.ShapeDtypeStruct((n_rows, d), F32),
        compiler_params=_params("arbitrary"),
        name="moe_experts",
    )(block_expert, n_used, row_tok, h, w_gate, w_up, w_down)


def _final_kernel(x1_ref, y0_ref, y1_ref, gw_ref, g2p_ref, g2s_ref, nf_ref, op_ref, os_ref, *, last, n_first):
    gw = gw_ref[...]
    f = gw[:, 0:1] * y0_ref[...] + gw[:, 1:2] * y1_ref[...]
    x2 = x1_ref[...] + _pick_rows(n_first, g2p_ref, g2s_ref) * f
    if last:
        x2 = x2 * lax.rsqrt(jnp.mean(x2 * x2, axis=-1, keepdims=True) + NORM_EPS) * nf_ref[...]
    i = pl.program_id(0)

    @pl.when(i < n_first)
    def _():
        op_ref[...] = x2

    @pl.when(i >= n_first)
    def _():
        os_ref[...] = x2


def _final(x1, y0, y1, gw, mods, norm_final, tiles_per_seq, last, n_first):
    m, d = x1.shape
    row = lambda i: (i, 0)
    rs = pl.BlockSpec((ROW_TILE, d), row)
    return pl.pallas_call(
        functools.partial(_final_kernel, last=last, n_first=n_first),
        grid=(m // ROW_TILE,),
        in_specs=[rs, rs, rs, pl.BlockSpec((ROW_TILE, LANES), row)] + _mod_specs(mods, 5, tiles_per_seq) + [
                  pl.BlockSpec((1, d), lambda i: (0, 0))],
        out_specs=_two_group_specs(d, n_first),
        out_shape=[jax.ShapeDtypeStruct((n_first * ROW_TILE, d), F32),
                   jax.ShapeDtypeStruct((m - n_first * ROW_TILE, d), F32)],
        compiler_params=_params("arbitrary"),
        name="combine_final",
    )(x1, y0, y1, gw, *mods, norm_final.reshape(1, d))


def _dispatch(eid):
    m = eid.shape[0]
    n_assign = m * EXPERT_TOPK
    flat_e = eid.reshape(-1)
    flat_tok = jnp.repeat(jnp.arange(m, dtype=I32), EXPERT_TOPK)
    onehot = (flat_e[:, None] == jnp.arange(N_EXPERTS, dtype=I32)[None, :]).astype(I32)
    csum = jnp.cumsum(onehot, axis=0)
    rank = jnp.take_along_axis(csum, flat_e[:, None], axis=1)[:, 0] - 1
    counts = csum[-1]
    padded = (counts + MOE_TILE - 1) // MOE_TILE * MOE_TILE
    pad_end = jnp.cumsum(padded)
    pad_start = pad_end - padded
    dest = (pad_start[flat_e] + rank).astype(I32)
    n_blocks = -(-(n_assign + N_EXPERTS * (MOE_TILE - 1)) // MOE_TILE)
    row_tok = jnp.zeros((n_blocks * MOE_TILE,), I32).at[dest].set(flat_tok)
    blk_start = jnp.arange(n_blocks, dtype=I32) * MOE_TILE
    block_expert = jnp.minimum(jnp.sum((pad_end[None, :] <= blk_start[:, None]).astype(I32), axis=1),
                               N_EXPERTS - 1).astype(I32)
    n_used = (pad_end[-1:] // MOE_TILE).astype(I32)
    return dest.reshape(m, EXPERT_TOPK), row_tok, block_expert, n_used


def _mod_tiles(mod, n_prompt, reps):
    mod_p = mod[:n_prompt].transpose(1, 0, 2)[:, :, None, :]
    mod_s = jnp.repeat(mod[n_prompt:], reps, axis=0).transpose(1, 0, 2)
    return mod_p, mod_s


def kernel(x_prompt, x_sample, c_prompt, c_sample, cache_k, cache_v, state_hgrn, page_table,
           w_ada, b_ada, norm_mix, norm_ffn, w_in, hgrn_lb_logits, hgrn_norm,
           w_proj_a, w_proj_b, w_out, w_group, b_group, w_expert_router, b_expert_router,
           w_gate, w_up, w_down, norm_final):
    bp, tp, d = x_prompt.shape
    bs, ts, _ = x_sample.shape
    mp_rows, ms_rows = bp * tp, bs * ts
    assert ms_rows == ROW_TILE and tp % ROW_TILE == 0
    m = mp_rows + ms_rows
    tiles_per_seq = tp // ROW_TILE
    n_full = PAST_LEN // MOBA_BLOCK
    assert PAST_LEN % MOBA_BLOCK == 0 and n_full >= MOBA_TOPK

    xp, xs = x_prompt.reshape(mp_rows, d), x_sample.reshape(ms_rows, d)
    c_all = jnp.concatenate([c_prompt, c_sample, jnp.zeros((-(bp + bs) % 8, d), F32)], axis=0)
    lb_all = jnp.cumsum(jax.nn.softmax(hgrn_lb_logits.astype(F32), axis=0), axis=0)
    cache_kt = cache_k.transpose(0, 1, 2, 4, 3)
    cache_vt = cache_v.transpose(0, 1, 2, 4, 3)
    kp_l, vp_l, sp_l, ks_l, vs_l, ss_l = [], [], [], [], [], []

    for l in range(DEPTH):
        mod = _modulation(c_all, w_ada[l], b_ada[l])[:bp + bs].reshape(bp + bs, 6, d)
        mt = _mod_tiles(mod, bp, ts)
        za, zb, zg = _inproj(xp, xs, mt, norm_mix[l], w_in[l], tiles_per_seq)

        oa_p, st_p = _hgrn(za, lb_all[l], hgrn_norm[l], 0, bp, tp, ROW_TILE, HGRN_CHUNK)
        oa_s, st_s = _hgrn(za, lb_all[l], hgrn_norm[l], mp_rows, bs, ts, ts, ts, s0=state_hgrn[l])

        q_p, k_p, v_p = _rope_split(zb, 0, bp, tp, 512, jnp.arange(tp))
        q_s, k_s, v_s = _rope_split(zb, mp_rows, bs, ts, ts, PAST_LEN + jnp.arange(ts))
        ob_p = _moba_prompt(q_p, k_p, v_p)
        sel = _select_blocks(q_s, cache_kt, page_table[:, :n_full * PAGES_PER_BLOCK], l)[..., :MOBA_TOPK]
        ob_s = _moba_sample(q_s, k_s, v_s, cache_kt, cache_vt, sel, page_table, l)
        ob_s = ob_s.transpose(0, 2, 1, 3).reshape(ms_rows, B_WIDTH).astype(BF16)

        w_router = jnp.concatenate(
            [w_group[l], w_expert_router[l].transpose(1, 0, 2).reshape(d, N_EXPERTS),
             jnp.zeros((d, LANES - N_GROUPS - N_EXPERTS), F32)], axis=1)
        b_router = jnp.concatenate(
            [b_group[l], b_expert_router[l].reshape(-1),
             jnp.zeros((LANES - N_GROUPS - N_EXPERTS,), F32)]).reshape(1, LANES)
        x1, h2, eid, gw = _merge(xp, xs, oa_p, oa_s, ob_p, ob_s, zg, mt, norm_ffn[l], w_proj_a[l].astype(BF16),
                                 w_proj_b[l].astype(BF16), w_out[l].astype(BF16), w_router, b_router,
                                 tiles_per_seq)

        dest, row_tok, block_expert, n_used = _dispatch(eid[:, :EXPERT_TOPK])
        ys = _moe(h2, row_tok, block_expert, n_used, w_gate, w_up, w_down, l)
        xp, xs = _final(x1, ys[dest[:, 0]], ys[dest[:, 1]], gw, mt, norm_final, tiles_per_seq,
                        l == DEPTH - 1, mp_rows // ROW_TILE)

        kp_l.append(k_p); vp_l.append(v_p); sp_l.append(st_p)
        ks_l.append(k_s); vs_l.append(v_s); ss_l.append(st_s)

    y_prompt = xp.reshape(bp, tp, d)
    y_sample = xs.reshape(bs, ts, d)
    return (y_prompt, y_sample, jnp.stack(kp_l), jnp.stack(vp_l), jnp.stack(sp_l),
            jnp.stack(ks_l), jnp.stack(vs_l), jnp.stack(ss_l))
```

```python
import functools

import numpy as np
import jax
import jax.numpy as jnp
from jax import lax
from jax.experimental import pallas as pl
from jax.experimental.pallas import tpu as pltpu

F32 = jnp.float32
BF16 = jnp.bfloat16
I32 = jnp.int32

D_MODEL = 1024
DEPTH = 1
PAST_LEN = 16384
PAGE_SIZE = 128
A_HEADS = 4
A_KDIM = 128
A_VDIM = 128
A_WIDTH = A_HEADS * A_KDIM
B_HEADS = 8
B_HEAD_DIM = 64
B_WIDTH = B_HEADS * B_HEAD_DIM
MOBA_BLOCK = 256
MOBA_TOPK = 3
ROT_DIM = B_HEAD_DIM // 4
ROPE_THETA = 500000.0
N_GROUPS = 4
EXPERTS_PER_GROUP = 8
N_EXPERTS = N_GROUPS * EXPERTS_PER_GROUP
EXPERT_TOPK = 2
D_EXPERT = D_MODEL // 2
NORM_EPS = 1e-6

PAGES_PER_BLOCK = MOBA_BLOCK // PAGE_SIZE
ROW_TILE = 256
HGRN_CHUNK = 16
MOE_TILE = 256
LANES = 128
NEG_BIG = -1e30
LOG2_E = 1.4426950408889634
VMEM_LIMIT_BYTES = 52 * 1024 * 1024


def _params(*sem):
    return pltpu.CompilerParams(dimension_semantics=sem, vmem_limit_bytes=VMEM_LIMIT_BYTES)


def _dot(a, b):
    return jnp.dot(a, b, preferred_element_type=F32)


def _dot_nt(a, b):
    return lax.dot_general(a, b, (((1,), (1,)), ((), ())), preferred_element_type=F32)


def _dot_tn(a, b):
    return lax.dot_general(a, b, (((0,), (0,)), ((), ())), preferred_element_type=F32)


def _split(a):
    hi = a.astype(BF16)
    return hi, (a - hi.astype(F32)).astype(BF16)


def _dot3(a, b, dot=_dot):
    ah, al = _split(a)
    bh, bl = _split(b)
    return dot(ah, bh) + (dot(ah, bl) + dot(al, bh))


def _silu(x):
    return x * jax.nn.sigmoid(x)


def _mod_kernel(c_ref, w_ref, b_ref, o_ref):
    o_ref[...] = _dot3(_silu(c_ref[...]), w_ref[...]) + b_ref[...]


def _modulation(c_all, w, b):
    n = c_all.shape[0]
    d, dout = w.shape
    return pl.pallas_call(
        _mod_kernel,
        grid=(dout // d,),
        in_specs=[pl.BlockSpec((n, d), lambda j: (0, 0)),
                  pl.BlockSpec((d, d), lambda j: (0, j)),
                  pl.BlockSpec((1, d), lambda j: (0, j))],
        out_specs=pl.BlockSpec((n, d), lambda j: (0, j)),
        out_shape=jax.ShapeDtypeStruct((n, dout), F32),
        compiler_params=_params("parallel"),
        name="modulation",
    )(c_all, w, b.reshape(1, dout))


def _pick_rows(n_first, first_ref, second_ref):
    return jnp.where(pl.program_id(0) < n_first, first_ref[...], second_ref[...])


def _two_group_specs(width, n_first):
    return [pl.BlockSpec((ROW_TILE, width), lambda i: (jnp.minimum(i, n_first - 1), 0)),
            pl.BlockSpec((ROW_TILE, width), lambda i: (0, 0))]


def _mod_specs(mods, k, tiles_per_seq):
    mod_p, mod_s = mods
    n_seq, d = mod_p.shape[1], mod_p.shape[3]
    return [pl.BlockSpec((None, None, 1, d), lambda i: (k, jnp.minimum(i // tiles_per_seq, n_seq - 1), 0, 0)),
            pl.BlockSpec((None, ROW_TILE, d), lambda i: (k, 0, 0))]


def _inproj_kernel(xp_ref, xs_ref, shp_ref, shs_ref, scp_ref, scs_ref, g_ref, w_ref,
                   za_ref, zb_ref, zg_ref, *, n_first):
    x = _pick_rows(n_first, xp_ref, xs_ref)
    h = x * lax.rsqrt(jnp.mean(x * x, axis=-1, keepdims=True) + NORM_EPS) * g_ref[...]
    h = h * (1.0 + _pick_rows(n_first, scp_ref, scs_ref)) + _pick_rows(n_first, shp_ref, shs_ref)
    hh = h.astype(BF16)
    wa = 4 * A_WIDTH
    for c in range(0, wa, 512):
        za_ref[:, c:c + 512] = _dot(hh, w_ref[:, c:c + 512])
    for c in range(0, 3 * B_WIDTH, 512):
        zb_ref[:, c:c + 512] = _dot(hh, w_ref[:, wa + c:wa + c + 512])
    wg = wa + 3 * B_WIDTH
    for c in range(0, 2 * D_MODEL, 512):
        zg_ref[:, c:c + 512] = _dot(hh, w_ref[:, wg + c:wg + c + 512])


def _inproj(xp, xs, mods, norm_g, w_in, tiles_per_seq):
    d = xp.shape[1]
    n_first = xp.shape[0] // ROW_TILE
    m = xp.shape[0] + xs.shape[0]
    w_hi = w_in.astype(BF16)
    wa = 4 * A_WIDTH
    const = lambda i: (0, 0)
    row = lambda i: (i, 0)
    return pl.pallas_call(
        functools.partial(_inproj_kernel, n_first=n_first),
        grid=(m // ROW_TILE,),
        in_specs=_two_group_specs(d, n_first) + _mod_specs(mods, 0, tiles_per_seq)
                 + _mod_specs(mods, 1, tiles_per_seq) + [
                  pl.BlockSpec((1, d), const),
                  pl.BlockSpec(w_hi.shape, const, pipeline_mode=pl.Buffered(1))],
        out_specs=[pl.BlockSpec((ROW_TILE, wa), row),
                   pl.BlockSpec((ROW_TILE, 3 * B_WIDTH), row),
                   pl.BlockSpec((ROW_TILE, 2 * D_MODEL), row)],
        out_shape=[jax.ShapeDtypeStruct((m, wa), F32),
                   jax.ShapeDtypeStruct((m, 3 * B_WIDTH), F32),
                   jax.ShapeDtypeStruct((m, 2 * D_MODEL), F32)],
        compiler_params=_params("parallel"),
        name="inproj",
    )(xp, xs, *mods, *mods, norm_g.reshape(1, d), w_hi)


def _rope_kernel(q_ref, k_ref, v_ref, c_ref, s1_ref, s2_ref, qo_ref, ko_ref, vo_ref):
    cos, s1, s2 = c_ref[...], s1_ref[...], s2_ref[...]
    half = ROT_DIM // 2

    def rope(x):
        up = pltpu.roll(x, B_WIDTH - half, axis=1)
        dn = pltpu.roll(x, half, axis=1)
        return x * cos + up * s1 + dn * s2

    q = rope(q_ref[...]) * (B_HEAD_DIM ** -0.5)
    k = rope(k_ref[...])
    v = v_ref[...]
    for h in range(B_HEADS):
        ls = slice(h * B_HEAD_DIM, (h + 1) * B_HEAD_DIM)
        qo_ref[0, h] = q[:, ls]
        ko_ref[0, h] = k[:, ls]
        vo_ref[0, h] = v[:, ls]


def _rope_tables(pos):
    half = ROT_DIM // 2
    inv = jnp.power(ROPE_THETA, -(jnp.arange(half, dtype=F32) * 2.0 / ROT_DIM))
    ang = pos.astype(F32)[:, None] * inv[None, :]
    cos, sin = jnp.cos(ang), jnp.sin(ang)
    t = pos.shape[0]
    rest = B_HEAD_DIM - ROT_DIM
    c = jnp.concatenate([cos, cos, jnp.ones((t, rest), F32)], axis=-1)
    s1 = jnp.concatenate([-sin, jnp.zeros((t, half + rest), F32)], axis=-1)
    s2 = jnp.concatenate([jnp.zeros((t, half), F32), sin, jnp.zeros((t, rest), F32)], axis=-1)
    return [jnp.tile(a, (1, B_HEADS)) for a in (c, s1, s2)]


def _rope_split(zb, row0, n, t, tile, pos):
    tabs = _rope_tables(pos)
    tps = t // tile
    blk0 = row0 // tile
    zmap = lambda c: (lambda b, s: (blk0 + b * tps + s, c))
    tmap = lambda b, s: (s, 0)
    omap = lambda b, s: (b, 0, s, 0)
    oshape = jax.ShapeDtypeStruct((n, B_HEADS, t, B_HEAD_DIM), F32)
    ospec = pl.BlockSpec((1, B_HEADS, tile, B_HEAD_DIM), omap)
    return pl.pallas_call(
        _rope_kernel,
        grid=(n, tps),
        in_specs=[pl.BlockSpec((tile, B_WIDTH), zmap(0)),
                  pl.BlockSpec((tile, B_WIDTH), zmap(1)),
                  pl.BlockSpec((tile, B_WIDTH), zmap(2)),
                  pl.BlockSpec((tile, B_WIDTH), tmap),
                  pl.BlockSpec((tile, B_WIDTH), tmap),
                  pl.BlockSpec((tile, B_WIDTH), tmap)],
        out_specs=[ospec, ospec, ospec],
        out_shape=[oshape, oshape, oshape],
        compiler_params=_params("parallel", "parallel"),
        name="rope_split",
    )(zb, zb, zb, *tabs)


def _hgrn_kernel(*refs, chunk, has_s0):
    aq_ref, af_ref, ai_ref, ag_ref, lb_ref, gain_ref = refs[:6]
    rest = refs[6:]
    s0_ref = None
    if has_s0:
        s0_ref, rest = rest[0], rest[1:]
    o_ref, so_ref, st_ref, q_s, b_s, k_s = rest
    t = pl.program_id(1)
    tb = aq_ref.shape[0]

    @pl.when(t == 0)
    def _():
        for h in range(A_HEADS):
            if has_s0:
                st_ref[h] = s0_ref[0, h].T
            else:
                st_ref[h] = jnp.zeros((A_VDIM, A_KDIM), F32)

    lb = lb_ref[...]
    f = lb + (1.0 - lb) * jax.nn.sigmoid(af_ref[...])
    logf = jnp.log(f)
    q_s[...] = _silu(aq_ref[...])
    k_s[...] = 1.0 - f
    row = lax.broadcasted_iota(I32, logf.shape, 0) & (chunk - 1)
    b = logf
    sh = 1
    while sh < chunk:
        b = b + jnp.where(row >= sh, pltpu.roll(b, sh, axis=0), 0.0)
        sh *= 2
    b_s[...] = b
    rowc = lax.broadcasted_iota(I32, (chunk, A_KDIM), 0)

    def one_chunk(ci, carry):
        r0 = pl.multiple_of(ci * chunk, chunk)
        rs = pl.ds(r0, chunk)
        for h in range(A_HEADS):
            ls = slice(h * A_KDIM, (h + 1) * A_KDIM)
            qc, bc, kc, vc = q_s[rs, ls], b_s[rs, ls], k_s[rs, ls], ai_ref[rs, ls]
            st = st_ref[h]
            bl = bc[chunk - 1:chunk, :]
            o = _dot_nt((qc * jnp.exp(bc)).astype(BF16), st.astype(BF16))
            grp = 8
            parts = [o[g * grp:(g + 1) * grp] for g in range(chunk // grp)]
            for s in range(chunk):
                for g in range(s // grp, chunk // grp):
                    rows = slice(g * grp, (g + 1) * grp)
                    diff = bc[rows] - bc[s:s + 1, :]
                    if g == s // grp:
                        diff = jnp.where(rowc[rows] >= s, diff, -jnp.inf)
                    e = jnp.exp(diff)
                    r = jnp.sum(qc[rows] * e * kc[s:s + 1, :], axis=1, keepdims=True)
                    parts[g] = parts[g] + r * vc[s:s + 1, :]
            o = jnp.concatenate(parts, axis=0) if len(parts) > 1 else parts[0]
            kp = kc * jnp.exp(bl - bc)
            st_ref[h] = st * jnp.exp(bl) + _dot_tn(vc.astype(BF16), kp.astype(BF16))
            o_ref[rs, ls] = o
        return carry

    n_chunks = tb // chunk
    lax.fori_loop(0, n_chunks, one_chunk, 0, unroll=4 if n_chunks % 4 == 0 else 1)

    gain = gain_ref[...]
    for h in range(A_HEADS):
        ls = slice(h * A_VDIM, (h + 1) * A_VDIM)
        oh = o_ref[:, ls]
        y = oh * lax.rsqrt(jnp.mean(oh * oh, axis=-1, keepdims=True) + NORM_EPS) * gain
        o_ref[:, ls] = y * _silu(ag_ref[:, ls])

    @pl.when(t == pl.num_programs(1) - 1)
    def _():
        for h in range(A_HEADS):
            so_ref[0, h] = st_ref[h].T


def _hgrn(za, lb, gain, row0, n, t, tile, chunk, s0=None):
    tps = t // tile
    blk0 = row0 // tile
    zmap = lambda c: (lambda b, s: (blk0 + b * tps + s, c))
    const = lambda b, s: (0, 0)
    in_specs = [pl.BlockSpec((tile, A_WIDTH), zmap(c)) for c in range(4)]
    in_specs += [pl.BlockSpec((1, A_WIDTH), const), pl.BlockSpec((1, A_VDIM), const)]
    args = [za, za, za, za, lb.reshape(1, A_WIDTH), gain.reshape(1, A_VDIM)]
    if s0 is not None:
        in_specs.append(pl.BlockSpec((1, A_HEADS, A_KDIM, A_VDIM), lambda b, s: (b, 0, 0, 0)))
        args.append(s0)
    return pl.pallas_call(
        functools.partial(_hgrn_kernel, chunk=chunk, has_s0=s0 is not None),
        grid=(n, tps),
        in_specs=in_specs,
        out_specs=[pl.BlockSpec((tile, A_WIDTH), lambda b, s: (b * tps + s, 0)),
                   pl.BlockSpec((1, A_HEADS, A_KDIM, A_VDIM), lambda b, s: (b, 0, 0, 0))],
        out_shape=[jax.ShapeDtypeStruct((n * t, A_WIDTH), F32),
                   jax.ShapeDtypeStruct((n, A_HEADS, A_KDIM, A_VDIM), F32)],
        scratch_shapes=[pltpu.VMEM((A_HEADS, A_VDIM, A_KDIM), F32),
                        pltpu.VMEM((tile, A_WIDTH), F32),
                        pltpu.VMEM((tile, A_WIDTH), F32),
                        pltpu.VMEM((tile, A_WIDTH), F32)],
        compiler_params=_params("parallel", "arbitrary"),
        name="hgrn2",
    )(*args)


MOBA_CHUNK_BLOCKS = 4
MOBA_HEADS_PER_STEP = 2
MOBA_QUERY_BLOCKS = 4


def _moba_prompt_kernel(q_ref, k_ref, v_ref, o_ref, km_ref, ka_ref, vt_ref, eye_ref):
    j = pl.program_id(2)
    blk = MOBA_BLOCK
    hp, nq, dh = q_ref.shape[1], q_ref.shape[2], q_ref.shape[3]
    n_chunks, kc = ka_ref.shape[0], ka_ref.shape[1]
    n_blk = n_chunks * MOBA_CHUNK_BLOCKS
    eye = (lax.broadcasted_iota(I32, (dh, dh), 0) == lax.broadcasted_iota(I32, (dh, dh), 1)).astype(BF16)

    @pl.when(j == 0)
    def _():
        eye_ref[...] = (lax.broadcasted_iota(I32, (blk, blk), 0)
                        == lax.broadcasted_iota(I32, (blk, blk), 1)).astype(BF16)
        for c in range(n_chunks):
            rows = slice(c * kc, (c + 1) * kc)
            key_blk = lax.broadcasted_iota(I32, (kc, n_blk), 0) // blk + c * MOBA_CHUNK_BLOCKS
            onehot = key_blk == lax.broadcasted_iota(I32, (kc, n_blk), 1)
            ka_ref[c, :, hp * dh:] = jnp.where(onehot, 1.0, 0.0).astype(BF16)
            for hh in range(hp):
                ka_ref[c, :, hh * dh:(hh + 1) * dh] = k_ref[0, hh, rows, :].astype(BF16)
                vt_ref[hh, c] = _dot_nt(eye, v_ref[0, hh, rows, :].astype(BF16)).astype(BF16)
        for hh in range(hp):
            for i in range(n_blk):
                km_ref[hh, i:i + 1, :] = jnp.mean(k_ref[0, hh, i * blk:(i + 1) * blk, :], axis=0, keepdims=True)

    bidx = lax.broadcasted_iota(I32, (n_blk, nq), 0)
    own = j * (nq // blk) + lax.broadcasted_iota(I32, (n_blk, nq), 1) // blk
    valid = bidx < own
    q_rows, biases = [], []
    for hh in range(hp):
        q = q_ref[0, hh]
        sc = jnp.where(valid, _dot3(km_ref[hh], q, _dot_nt), -jnp.inf)
        rank = jnp.zeros((n_blk, nq), I32)
        for i in range(n_blk - 1):
            row = sc[i:i + 1, :]
            rank = rank + jnp.where(row > sc, 1, jnp.where(row == sc, jnp.where(bidx > i, 1, 0), 0))
        biases.append(jnp.where(valid, jnp.where(rank < MOBA_TOPK, 0.0, NEG_BIG),
                                jnp.where(bidx == own, 0.0, NEG_BIG)))
        q_t = _dot_nt(eye, (q * LOG2_E).astype(BF16))
        zero = jnp.zeros((dh, nq), F32)
        q_rows.append(jnp.concatenate([q_t if g == hh else zero for g in range(hp)], axis=1))
    qa = jnp.concatenate(q_rows + [jnp.concatenate(biases, axis=1)], axis=0).astype(BF16)

    def scores(c):
        return jnp.concatenate([_dot(ka_ref[c, :kc // 2], qa), _dot(ka_ref[c, kc // 2:], qa)], axis=0)

    def weighted_values(c, p):
        pb = p.astype(BF16)
        return jnp.concatenate([_dot(vt_ref[hh, c], pb[:, hh * nq:(hh + 1) * nq]) for hh in range(hp)], axis=1)

    cj = (j * nq) // kc
    key_pos = lax.broadcasted_iota(I32, (kc, hp * nq), 0) + cj * kc
    q_pos = (lax.broadcasted_iota(I32, (kc, hp * nq), 1) & (nq - 1)) + j * nq
    s = jnp.where(key_pos <= q_pos, scores(cj), NEG_BIG)
    m = jnp.max(s, axis=0, keepdims=True)
    p = jnp.exp2(s - m)
    init = (m, jnp.sum(p, axis=0, keepdims=True), weighted_values(cj, p))

    def past_chunk(c, carry):
        m, l, acc = carry
        s = scores(c)
        m_new = jnp.maximum(m, jnp.max(s, axis=0, keepdims=True))
        p = jnp.exp2(s - m_new)
        alpha = jnp.exp2(m - m_new)
        return m_new, alpha * l + jnp.sum(p, axis=0, keepdims=True), alpha * acc + weighted_values(c, p)

    _, l, acc = lax.fori_loop(0, cj, past_chunk, init)
    o_t = (acc / l).astype(BF16)
    o_ref[...] = jnp.concatenate(
        [jnp.concatenate([_dot_nt(eye_ref[...], o_t[:, hh * nq + u * blk:hh * nq + (u + 1) * blk])
                          for u in range(nq // blk)], axis=0) for hh in range(hp)], axis=1).astype(BF16)


def _moba_prompt(q, k, v):
    b, h, s, dh = q.shape
    hp = MOBA_HEADS_PER_STEP
    n_blk = s // MOBA_BLOCK
    nq = MOBA_QUERY_BLOCKS * MOBA_BLOCK
    assert n_blk % MOBA_CHUNK_BLOCKS == 0 and h % hp == 0 and hp * dh == LANES
    assert MOBA_CHUNK_BLOCKS % MOBA_QUERY_BLOCKS == 0
    n_chunks = n_blk // MOBA_CHUNK_BLOCKS
    n_tiles = s // nq
    kc = MOBA_CHUNK_BLOCKS * MOBA_BLOCK
    full = pl.BlockSpec((1, hp, s, dh), lambda bi, hi, j: (bi, hi, 0, 0))
    return pl.pallas_call(
        _moba_prompt_kernel,
        grid=(b, h // hp, n_tiles),
        in_specs=[pl.BlockSpec((1, hp, nq, dh), lambda bi, hi, j: (bi, hi, j, 0)), full, full],
        out_specs=pl.BlockSpec((nq, hp * dh), lambda bi, hi, j: (bi * n_tiles + j, hi)),
        out_shape=jax.ShapeDtypeStruct((b * s, h * dh), BF16),
        scratch_shapes=[pltpu.VMEM((hp, n_blk, dh), F32),
                        pltpu.VMEM((n_chunks, kc, hp * dh + n_blk), BF16),
                        pltpu.VMEM((hp, n_chunks, dh, kc), BF16),
                        pltpu.VMEM((MOBA_BLOCK, MOBA_BLOCK), BF16)],
        compiler_params=_params("parallel", "parallel", "arbitrary"),
        name="moba_prompt",
    )(q, k, v)


SELECT_PAGES_PER_STEP = 16
PAGE_RING = 3


def _select_kernel(pt_ref, q_ref, ck_ref, o_ref, buf, pm_ref, sem, *, layer, pages_per_seq):
    n_grp = pl.num_programs(1)
    g = pl.program_id(1)
    step = pl.program_id(0) * n_grp + g
    n_steps = pl.num_programs(0) * n_grp
    ring = buf.shape[0]
    slot = step % ring
    pps, n_heads, dh, ps = buf.shape[1], buf.shape[2], buf.shape[3], buf.shape[4]
    t = q_ref.shape[2]

    def page_copy(st, sl, i, pg):
        return pltpu.make_async_copy(ck_ref.at[layer, pg], buf.at[sl, i], sem.at[sl])

    def fetch(st):
        base = (st // n_grp) * pages_per_seq + (st % n_grp) * pps
        for i in range(pps):
            page_copy(st, st % ring, i, pt_ref[base + i]).start()

    for ahead in range(ring - 1):
        @pl.when((step == 0) & (ahead < n_steps))
        def _():
            fetch(ahead)

    @pl.when(step + ring - 1 < n_steps)
    def _():
        fetch(step + ring - 1)

    for i in range(pps):
        page_copy(step, slot, i, 0).wait()

    bps = pps // PAGES_PER_BLOCK
    diag = lax.broadcasted_iota(I32, (dh, ps), 0) == lax.broadcasted_iota(I32, (dh, ps), 1)
    blocks = []
    for b in range(bps):
        xb = buf[slot, b * PAGES_PER_BLOCK]
        for pi in range(1, PAGES_PER_BLOCK):
            xb = xb + buf[slot, b * PAGES_PER_BLOCK + pi]
        blocks.append(xb)
    x = jnp.stack(blocks, axis=0)
    r = jnp.sum(x, axis=-1, keepdims=True) * (1.0 / (ps * PAGES_PER_BLOCK))
    bm_step = jnp.sum(jnp.where(diag, r, 0.0), axis=2)
    pm_ref[pl.ds(pl.multiple_of(g * bps * n_heads, bps * n_heads), bps * n_heads), :] = (
        bm_step.reshape(bps * n_heads, ps))

    @pl.when(g == n_grp - 1)
    def _():
        n_blk = pages_per_seq // PAGES_PER_BLOCK
        lane = lax.broadcasted_iota(I32, (t, n_blk), 1).astype(F32)
        for h in range(n_heads):
            bm = pm_ref[pl.ds(h, n_blk, stride=n_heads), :]
            sc = _dot3(q_ref[0, h], bm[:, :dh], _dot_nt)
            res = jnp.zeros((t, n_blk), F32)
            for k in range(MOBA_TOPK):
                m = jnp.max(sc, axis=1, keepdims=True)
                idx = jnp.min(jnp.where(sc == m, lane, float(n_blk)), axis=1, keepdims=True)
                res = jnp.where(lane == k, idx, res)
                sc = jnp.where(lane == idx, -jnp.inf, sc)
            o_ref[0, h] = res.astype(I32)


def _select_blocks(q, cache_kt, page_table, layer):
    n, h, t, dh = q.shape
    ps = cache_kt.shape[-1]
    pages_per_seq = page_table.shape[1]
    pps = SELECT_PAGES_PER_STEP
    assert pages_per_seq % pps == 0 and pages_per_seq % PAGES_PER_BLOCK == 0 and dh <= ps
    n_blk = pages_per_seq // PAGES_PER_BLOCK
    return pl.pallas_call(
        functools.partial(_select_kernel, layer=layer, pages_per_seq=pages_per_seq),
        grid_spec=pltpu.PrefetchScalarGridSpec(
            num_scalar_prefetch=1,
            grid=(n, pages_per_seq // pps),
            in_specs=[pl.BlockSpec((1, h, t, dh), lambda i, g, pt: (i, 0, 0, 0)),
                      pl.BlockSpec(memory_space=pl.ANY)],
            out_specs=pl.BlockSpec((1, h, t, n_blk), lambda i, g, pt: (i, 0, 0, 0)),
            scratch_shapes=[pltpu.VMEM((PAGE_RING, pps, h, dh, ps), F32),
                            pltpu.VMEM((n_blk * h, ps), F32),
                            pltpu.SemaphoreType.DMA((PAGE_RING,))]),
        out_shape=jax.ShapeDtypeStruct((n, h, t, n_blk), I32),
        compiler_params=_params("arbitrary", "arbitrary"),
        name="moba_select",
    )(page_table.reshape(-1), q, cache_kt)


def _moba_sample_kernel(sel_ref, pt_ref, q_ref, kn_ref, vn_ref, ck_ref, cv_ref, o_ref, kbuf, vbuf, sem, *,
                        layer, pages_per_seq):
    n_heads = pl.num_programs(1)
    step = pl.program_id(0) * n_heads + pl.program_id(1)
    n_steps = pl.num_programs(0) * n_heads
    t = q_ref.shape[2]
    ring, n_pages = kbuf.shape[0], kbuf.shape[1]
    pages_per_q = n_pages // t
    slot = step % ring

    def page_copies(st, sl, i, pg):
        hh = st % n_heads
        return (pltpu.make_async_copy(ck_ref.at[layer, pg, hh], kbuf.at[sl, i], sem.at[0, sl]),
                pltpu.make_async_copy(cv_ref.at[layer, pg, hh], vbuf.at[sl, i], sem.at[1, sl]))

    def fetch(st):
        sl = st % ring
        pt_base = (st // n_heads) * pages_per_seq
        for b in range(n_pages // PAGES_PER_BLOCK):
            blk = sel_ref[st * (n_pages // PAGES_PER_BLOCK) + b]
            for pi in range(PAGES_PER_BLOCK):
                pg = pt_ref[pt_base + blk * PAGES_PER_BLOCK + pi]
                for thread, cp in enumerate(page_copies(st, sl, b * PAGES_PER_BLOCK + pi, pg)):
                    cp.start(priority=thread)

    for ahead in range(ring - 1):
        @pl.when((step == 0) & (ahead < n_steps))
        def _():
            fetch(ahead)

    @pl.when(step + ring - 1 < n_steps)
    def _():
        fetch(step + ring - 1)

    for i in range(n_pages):
        for cp in page_copies(step, slot, i, 0):
            cp.wait()

    q = q_ref[0, 0]
    qb = q.astype(BF16)
    def pages_of(buf_ref, tq):
        return jnp.concatenate([buf_ref[slot, tq * pages_per_q + i].astype(BF16) for i in range(pages_per_q)],
                               axis=1)

    s = jnp.concatenate([_dot(qb, pages_of(kbuf, tq)) for tq in range(t)], axis=1)
    per_q = pages_per_q * PAGE_SIZE
    col = lax.broadcasted_iota(I32, s.shape, 1)
    lo = lax.broadcasted_iota(I32, s.shape, 0) * per_q
    own = (col >= lo) & (col < lo + per_q)
    sn = _dot_nt(q, kn_ref[0, 0])
    causal = lax.broadcasted_iota(I32, (t, t), 1) <= lax.broadcasted_iota(I32, (t, t), 0)
    m = jnp.maximum(jnp.max(jnp.where(own, s, NEG_BIG), axis=1, keepdims=True),
                    jnp.max(jnp.where(causal, sn, NEG_BIG), axis=1, keepdims=True))
    p = jnp.where(own, jnp.exp(s - m), 0.0).astype(BF16)
    pn = jnp.where(causal, jnp.exp(sn - m), 0.0)
    l = jnp.sum(p.astype(F32), axis=1, keepdims=True) + jnp.sum(pn, axis=1, keepdims=True)
    o = _dot(pn, vn_ref[0, 0])
    for tq in range(t):
        o = o + _dot_nt(p[:, tq * per_q:(tq + 1) * per_q], pages_of(vbuf, tq))
    o_ref[0, 0] = o / l


def _moba_sample(q, k_new, v_new, cache_kt, cache_vt, sel, page_table, layer):
    n, h, t, dh = q.shape
    n_pages = t * MOBA_TOPK * PAGES_PER_BLOCK
    blk = pl.BlockSpec((1, 1, t, dh), lambda i, j, sl, pt: (i, j, 0, 0))
    anyspec = pl.BlockSpec(memory_space=pl.ANY)
    return pl.pallas_call(
        functools.partial(_moba_sample_kernel, layer=layer, pages_per_seq=page_table.shape[1]),
        grid_spec=pltpu.PrefetchScalarGridSpec(
            num_scalar_prefetch=2,
            grid=(n, h),
            in_specs=[blk, blk, blk, anyspec, anyspec],
            out_specs=blk,
            scratch_shapes=[pltpu.VMEM((PAGE_RING, n_pages, dh, PAGE_SIZE), F32),
                            pltpu.VMEM((PAGE_RING, n_pages, dh, PAGE_SIZE), F32),
                            pltpu.SemaphoreType.DMA((2, PAGE_RING))]),
        out_shape=jax.ShapeDtypeStruct((n, h, t, dh), F32),
        compiler_params=_params("arbitrary", "arbitrary"),
        name="moba_sample",
    )(sel.reshape(-1), page_table.reshape(-1), q, k_new, v_new, cache_kt, cache_vt)


def _merge_kernel(xp_ref, xs_ref, oap_ref, oas_ref, obp_ref, obs_ref, ga_ref, gb_ref, g1p_ref, g1s_ref,
                  shp_ref, shs_ref, scp_ref, scs_ref, nf_ref, wpa_ref, wpb_ref, wo_ref, wr_ref, br_ref,
                  x1_ref, h2_ref, eid_ref, gw_ref, *, n_first):
    pa = _dot(_pick_rows(n_first, oap_ref, oas_ref).astype(BF16), wpa_ref[...])
    pb = _dot(_pick_rows(n_first, obp_ref, obs_ref), wpb_ref[...])
    mix = jax.nn.sigmoid(ga_ref[...]) * pa + jax.nn.sigmoid(gb_ref[...]) * pb
    x1 = (_pick_rows(n_first, xp_ref, xs_ref)
          + _pick_rows(n_first, g1p_ref, g1s_ref) * _dot(mix.astype(BF16), wo_ref[...]))
    x1_ref[...] = x1
    h2 = x1 * lax.rsqrt(jnp.mean(x1 * x1, axis=-1, keepdims=True) + NORM_EPS) * nf_ref[...]
    h2 = h2 * (1.0 + _pick_rows(n_first, scp_ref, scs_ref)) + _pick_rows(n_first, shp_ref, shs_ref)
    hb = h2.astype(BF16).astype(F32)
    half = hb.shape[1] // 2
    words = pltpu.bitcast(hb[:, :half], jnp.uint32) | (pltpu.bitcast(hb[:, half:], jnp.uint32) >> 16)
    for s in range(h2_ref.shape[1]):
        h2_ref[:, s, :] = words[:, s * LANES:(s + 1) * LANES]

    logits = _dot3(h2, wr_ref[...]) + br_ref[...]
    lane = lax.broadcasted_iota(I32, logits.shape, 1)
    lanef = lane.astype(F32)
    is_g = lane < N_GROUPS
    gl = jnp.where(is_g, logits, -jnp.inf)
    gmax = jnp.max(gl, axis=1, keepdims=True)
    g_sel = jnp.min(jnp.where(gl == gmax, lanef, float(LANES)), axis=1, keepdims=True)
    p_group = 1.0 / jnp.sum(jnp.exp(gl - gmax), axis=1, keepdims=True)
    e_lo = N_GROUPS + g_sel * EXPERTS_PER_GROUP
    in_grp = (lanef >= e_lo) & (lanef < e_lo + EXPERTS_PER_GROUP)
    el = jnp.where(in_grp, logits, -jnp.inf)
    m1 = jnp.max(el, axis=1, keepdims=True)
    i1 = jnp.min(jnp.where(el == m1, lanef, float(LANES)), axis=1, keepdims=True)
    el2 = jnp.where(lanef == i1, -jnp.inf, el)
    m2 = jnp.max(el2, axis=1, keepdims=True)
    i2 = jnp.min(jnp.where(el2 == m2, lanef, float(LANES)), axis=1, keepdims=True)
    e2 = jnp.exp(m2 - m1)
    w1 = p_group / (1.0 + e2)
    w2 = p_group * e2 / (1.0 + e2)
    eid = jnp.where(lane == 0, i1 - N_GROUPS, jnp.where(lane == 1, i2 - N_GROUPS, 0.0))
    eid_ref[...] = eid.astype(I32)
    gw_ref[...] = jnp.where(lane == 0, w1, jnp.where(lane == 1, w2, 0.0))


def _merge(xp, xs, oa_p, oa_s, ob_p, ob_s, zg, mods, norm_ffn, wpa, wpb, wo, w_router, b_router,
           tiles_per_seq):
    d = xp.shape[1]
    n_first = xp.shape[0] // ROW_TILE
    m = xp.shape[0] + xs.shape[0]
    const = lambda i: (0, 0)
    row = lambda i: (i, 0)
    wspec = lambda w: pl.BlockSpec(w.shape, const, pipeline_mode=pl.Buffered(1))
    return pl.pallas_call(
        functools.partial(_merge_kernel, n_first=n_first),
        grid=(m // ROW_TILE,),
        in_specs=_two_group_specs(d, n_first) + _two_group_specs(A_WIDTH, n_first)
                 + _two_group_specs(B_WIDTH, n_first) + [
                  pl.BlockSpec((ROW_TILE, d), lambda i: (i, 0)),
                  pl.BlockSpec((ROW_TILE, d), lambda i: (i, 1))]
                 + _mod_specs(mods, 2, tiles_per_seq) + _mod_specs(mods, 3, tiles_per_seq)
                 + _mod_specs(mods, 4, tiles_per_seq) + [
                  pl.BlockSpec((1, d), const),
                  wspec(wpa), wspec(wpb), wspec(wo), wspec(w_router),
                  pl.BlockSpec((1, LANES), const)],
        out_specs=[pl.BlockSpec((ROW_TILE, d), row),
                   pl.BlockSpec((ROW_TILE, d // (2 * LANES), LANES), lambda i: (i, 0, 0)),
                   pl.BlockSpec((ROW_TILE, LANES), row),
                   pl.BlockSpec((ROW_TILE, LANES), row)],
        out_shape=[jax.ShapeDtypeStruct((m, d), F32),
                   jax.ShapeDtypeStruct((m, d // (2 * LANES), LANES), jnp.uint32),
                   jax.ShapeDtypeStruct((m, LANES), I32),
                   jax.ShapeDtypeStruct((m, LANES), F32)],
        compiler_params=_params("parallel"),
        name="merge_router",
    )(xp, xs, oa_p, oa_s, ob_p, ob_s, zg, zg, *mods, *mods, *mods,
      norm_ffn.reshape(1, d), wpa, wpb, wo, w_router, b_router)


MOE_RING = 3


def _moe_kernel(be_ref, nu_ref, tok_ref, h_ref, wg_ref, wu_ref, wd_ref, y_ref, xbuf, sem, wg_s, wu_s, wd_s):
    i = pl.program_id(0)
    n_used = nu_ref[0]
    n_sub = h_ref.shape[1]
    ahead = MOE_RING - 1

    def row_copy(blk, r):
        return pltpu.make_async_copy(h_ref.at[tok_ref[blk * MOE_TILE + r]],
                                     xbuf.at[blk % MOE_RING, pl.ds(r * n_sub, n_sub), :],
                                     sem.at[blk % MOE_RING])

    def fetch(blk):
        for r in range(MOE_TILE):
            row_copy(blk, r).start(priority=r % 2)

    for b in range(ahead):
        @pl.when((i == 0) & (b < n_used))
        def _():
            fetch(b)

    @pl.when(i + ahead < n_used)
    def _():
        fetch(i + ahead)

    prev = be_ref[jnp.maximum(i - 1, 0)]
    fresh = (i == 0) | (be_ref[i] != prev)

    @pl.when(fresh)
    def _():
        wg_s[...] = wg_ref[...].astype(BF16)
        wu_s[...] = wu_ref[...].astype(BF16)
        wd_s[...] = wd_ref[...].astype(BF16)

    @pl.when(i < n_used)
    def _():
        for r in range(MOE_TILE):
            row_copy(i, r).wait()
        slot = i % MOE_RING
        words = [xbuf[slot, pl.ds(s, MOE_TILE, stride=n_sub), :] for s in range(n_sub)]
        xb = jnp.concatenate(
            [pltpu.bitcast(w & jnp.uint32(0xFFFF0000), F32).astype(BF16) for w in words]
            + [pltpu.bitcast(w << 16, F32).astype(BF16) for w in words], axis=1)
        hid = _silu(_dot(xb, wg_s[...])) * _dot(xb, wu_s[...])
        y_ref[...] = _dot(hid.astype(BF16), wd_s[...])

    @pl.when(i >= n_used)
    def _():
        y_ref[...] = jnp.zeros(y_ref.shape, F32)


def _moe(h, row_tok, block_expert, n_used, w_gate, w_up, w_down, layer):
    n_sub, lanes = h.shape[1], h.shape[2]
    d = 2 * n_sub * lanes
    n_rows = row_tok.shape[0]
    de = w_gate.shape[-1]
    return pl.pallas_call(
        _moe_kernel,
        grid_spec=pltpu.PrefetchScalarGridSpec(
            num_scalar_prefetch=3,
            grid=(n_rows // MOE_TILE,),
            in_specs=[pl.BlockSpec(memory_space=pl.ANY),
                      pl.BlockSpec((None, None, d, de), lambda i, be, nu, tk: (layer, be[i], 0, 0)),
                      pl.BlockSpec((None, None, d, de), lambda i, be, nu, tk: (layer, be[i], 0, 0)),
                      pl.BlockSpec((None, None, de, d), lambda i, be, nu, tk: (layer, be[i], 0, 0))],
            out_specs=pl.BlockSpec((MOE_TILE, d), lambda i, be, nu, tk: (i, 0)),
            scratch_shapes=[pltpu.VMEM((MOE_RING, MOE_TILE * n_sub, lanes), jnp.uint32),
                            pltpu.SemaphoreType.DMA((MOE_RING,)),
                            pltpu.VMEM((d, de), BF16), pltpu.VMEM((d, de), BF16), pltpu.VMEM((de, d), BF16)]),
        out_shape=jax.ShapeDtypeStruct((n_rows, d), F32),
        compiler_params=_params("arbitrary"),
        name="moe_experts",
    )(block_expert, n_used, row_tok, h, w_gate, w_up, w_down)


def _final_kernel(x1_ref, y0_ref, y1_ref, gw_ref, g2p_ref, g2s_ref, nf_ref, op_ref, os_ref, *, last, n_first):
    gw = gw_ref[...]
    f = gw[:, 0:1] * y0_ref[...] + gw[:, 1:2] * y1_ref[...]
    x2 = x1_ref[...] + _pick_rows(n_first, g2p_ref, g2s_ref) * f
    if last:
        x2 = x2 * lax.rsqrt(jnp.mean(x2 * x2, axis=-1, keepdims=True) + NORM_EPS) * nf_ref[...]
    i = pl.program_id(0)

    @pl.when(i < n_first)
    def _():
        op_ref[...] = x2

    @pl.when(i >= n_first)
    def _():
        os_ref[...] = x2


def _final(x1, y0, y1, gw, mods, norm_final, tiles_per_seq, last, n_first):
    m, d = x1.shape
    row = lambda i: (i, 0)
    rs = pl.BlockSpec((ROW_TILE, d), row)
    return pl.pallas_call(
        functools.partial(_final_kernel, last=last, n_first=n_first),
        grid=(m // ROW_TILE,),
        in_specs=[rs, rs, rs, pl.BlockSpec((ROW_TILE, LANES), row)] + _mod_specs(mods, 5, tiles_per_seq) + [
                  pl.BlockSpec((1, d), lambda i: (0, 0))],
        out_specs=_two_group_specs(d, n_first),
        out_shape=[jax.ShapeDtypeStruct((n_first * ROW_TILE, d), F32),
                   jax.ShapeDtypeStruct((m - n_first * ROW_TILE, d), F32)],
        compiler_params=_params("arbitrary"),
        name="combine_final",
    )(x1, y0, y1, gw, *mods, norm_final.reshape(1, d))


def _dispatch(eid):
    m = eid.shape[0]
    n_assign = m * EXPERT_TOPK
    flat_e = eid.reshape(-1)
    flat_tok = jnp.repeat(jnp.arange(m, dtype=I32), EXPERT_TOPK)
    onehot = (flat_e[:, None] == jnp.arange(N_EXPERTS, dtype=I32)[None, :]).astype(I32)
    csum = jnp.cumsum(onehot, axis=0)
    rank = jnp.take_along_axis(csum, flat_e[:, None], axis=1)[:, 0] - 1
    counts = csum[-1]
    padded = (counts + MOE_TILE - 1) // MOE_TILE * MOE_TILE
    pad_end = jnp.cumsum(padded)
    pad_start = pad_end - padded
    dest = (pad_start[flat_e] + rank).astype(I32)
    n_blocks = -(-(n_assign + N_EXPERTS * (MOE_TILE - 1)) // MOE_TILE)
    row_tok = jnp.zeros((n_blocks * MOE_TILE,), I32).at[dest].set(flat_tok)
    blk_start = jnp.arange(n_blocks, dtype=I32) * MOE_TILE
    block_expert = jnp.minimum(jnp.sum((pad_end[None, :] <= blk_start[:, None]).astype(I32), axis=1),
                               N_EXPERTS - 1).astype(I32)
    n_used = (pad_end[-1:] // MOE_TILE).astype(I32)
    return dest.reshape(m, EXPERT_TOPK), row_tok, block_expert, n_used


def _mod_tiles(mod, n_prompt, reps):
    mod_p = mod[:n_prompt].transpose(1, 0, 2)[:, :, None, :]
    mod_s = jnp.repeat(mod[n_prompt:], reps, axis=0).transpose(1, 0, 2)
    return mod_p, mod_s


def kernel(x_prompt, x_sample, c_prompt, c_sample, cache_k, cache_v, state_hgrn, page_table,
           w_ada, b_ada, norm_mix, norm_ffn, w_in, hgrn_lb_logits, hgrn_norm,
           w_proj_a, w_proj_b, w_out, w_group, b_group, w_expert_router, b_expert_router,
           w_gate, w_up, w_down, norm_final):
    bp, tp, d = x_prompt.shape
    bs, ts, _ = x_sample.shape
    mp_rows, ms_rows = bp * tp, bs * ts
    assert ms_rows == ROW_TILE and tp % ROW_TILE == 0
    m = mp_rows + ms_rows
    tiles_per_seq = tp // ROW_TILE
    n_full = PAST_LEN // MOBA_BLOCK
    assert PAST_LEN % MOBA_BLOCK == 0 and n_full >= MOBA_TOPK

    xp, xs = x_prompt.reshape(mp_rows, d), x_sample.reshape(ms_rows, d)
    c_all = jnp.concatenate([c_prompt, c_sample, jnp.zeros((-(bp + bs) % 8, d), F32)], axis=0)
    lb_all = jnp.cumsum(jax.nn.softmax(hgrn_lb_logits.astype(F32), axis=0), axis=0)
    cache_kt = cache_k.transpose(0, 1, 2, 4, 3)
    cache_vt = cache_v.transpose(0, 1, 2, 4, 3)
    kp_l, vp_l, sp_l, ks_l, vs_l, ss_l = [], [], [], [], [], []

    for l in range(DEPTH):
        mod = _modulation(c_all, w_ada[l], b_ada[l])[:bp + bs].reshape(bp + bs, 6, d)
        mt = _mod_tiles(mod, bp, ts)
        za, zb, zg = _inproj(xp, xs, mt, norm_mix[l], w_in[l], tiles_per_seq)

        oa_p, st_p = _hgrn(za, lb_all[l], hgrn_norm[l], 0, bp, tp, ROW_TILE, HGRN_CHUNK)
        oa_s, st_s = _hgrn(za, lb_all[l], hgrn_norm[l], mp_rows, bs, ts, ts, ts, s0=state_hgrn[l])

        q_p, k_p, v_p = _rope_split(zb, 0, bp, tp, 512, jnp.arange(tp))
        q_s, k_s, v_s = _rope_split(zb, mp_rows, bs, ts, ts, PAST_LEN + jnp.arange(ts))
        ob_p = _moba_prompt(q_p, k_p, v_p)
        sel = _select_blocks(q_s, cache_kt, page_table[:, :n_full * PAGES_PER_BLOCK], l)[..., :MOBA_TOPK]
        ob_s = _moba_sample(q_s, k_s, v_s, cache_kt, cache_vt, sel, page_table, l)
        ob_s = ob_s.transpose(0, 2, 1, 3).reshape(ms_rows, B_WIDTH).astype(BF16)

        w_router = jnp.concatenate(
            [w_group[l], w_expert_router[l].transpose(1, 0, 2).reshape(d, N_EXPERTS),
             jnp.zeros((d, LANES - N_GROUPS - N_EXPERTS), F32)], axis=1)
        b_router = jnp.concatenate(
            [b_group[l], b_expert_router[l].reshape(-1),
             jnp.zeros((LANES - N_GROUPS - N_EXPERTS,), F32)]).reshape(1, LANES)
        x1, h2, eid, gw = _merge(xp, xs, oa_p, oa_s, ob_p, ob_s, zg, mt, norm_ffn[l], w_proj_a[l].astype(BF16),
                                 w_proj_b[l].astype(BF16), w_out[l].astype(BF16), w_router, b_router,
                                 tiles_per_seq)

        dest, row_tok, block_expert, n_used = _dispatch(eid[:, :EXPERT_TOPK])
        ys = _moe(h2, row_tok, block_expert, n_used, w_gate, w_up, w_down, l)
        xp, xs = _final(x1, ys[dest[:, 0]], ys[dest[:, 1]], gw, mt, norm_final, tiles_per_seq,
                        l == DEPTH - 1, mp_rows // ROW_TILE)

        kp_l.append(k_p); vp_l.append(v_p); sp_l.append(st_p)
        ks_l.append(k_s); vs_l.append(v_s); ss_l.append(st_s)

    y_prompt = xp.reshape(bp, tp, d)
    y_sample = xs.reshape(bs, ts, d)
    return (y_prompt, y_sample, jnp.stack(kp_l), jnp.stack(vp_l), jnp.stack(sp_l),
            jnp.stack(ks_l), jnp.stack(vs_l), jnp.stack(ss_l))
```

```python
import functools

import numpy as np
import jax
import jax.numpy as jnp
from jax import lax
from jax.experimental import pallas as pl
from jax.experimental.pallas import tpu as pltpu

F32 = jnp.float32
BF16 = jnp.bfloat16
I32 = jnp.int32

D_MODEL = 1024
DEPTH = 1
PAST_LEN = 16384
PAGE_SIZE = 128
A_HEADS = 4
A_KDIM = 128
A_VDIM = 128
A_WIDTH = A_HEADS * A_KDIM
B_HEADS = 8
B_HEAD_DIM = 64
B_WIDTH = B_HEADS * B_HEAD_DIM
MOBA_BLOCK = 256
MOBA_TOPK = 3
ROT_DIM = B_HEAD_DIM // 4
ROPE_THETA = 500000.0
N_GROUPS = 4
EXPERTS_PER_GROUP = 8
N_EXPERTS = N_GROUPS * EXPERTS_PER_GROUP
EXPERT_TOPK = 2
D_EXPERT = D_MODEL // 2
NORM_EPS = 1e-6

PAGES_PER_BLOCK = MOBA_BLOCK // PAGE_SIZE
ROW_TILE = 256
HGRN_CHUNK = 16
MOE_TILE = 256
LANES = 128
NEG_BIG = -1e30
LOG2_E = 1.4426950408889634
VMEM_LIMIT_BYTES = 52 * 1024 * 1024


def _params(*sem):
    return pltpu.CompilerParams(dimension_semantics=sem, vmem_limit_bytes=VMEM_LIMIT_BYTES)


def _dot(a, b):
    return jnp.dot(a, b, preferred_element_type=F32)


def _dot_nt(a, b):
    return lax.dot_general(a, b, (((1,), (1,)), ((), ())), preferred_element_type=F32)


def _dot_tn(a, b):
    return lax.dot_general(a, b, (((0,), (0,)), ((), ())), preferred_element_type=F32)


def _split(a):
    hi = a.astype(BF16)
    return hi, (a - hi.astype(F32)).astype(BF16)


def _dot3(a, b, dot=_dot):
    ah, al = _split(a)
    bh, bl = _split(b)
    return dot(ah, bh) + (dot(ah, bl) + dot(al, bh))


def _silu(x):
    return x * jax.nn.sigmoid(x)


def _mod_kernel(c_ref, w_ref, b_ref, o_ref):
    o_ref[...] = _dot3(_silu(c_ref[...]), w_ref[...]) + b_ref[...]


def _modulation(c_all, w, b):
    n = c_all.shape[0]
    d, dout = w.shape
    return pl.pallas_call(
        _mod_kernel,
        grid=(dout // d,),
        in_specs=[pl.BlockSpec((n, d), lambda j: (0, 0)),
                  pl.BlockSpec((d, d), lambda j: (0, j)),
                  pl.BlockSpec((1, d), lambda j: (0, j))],
        out_specs=pl.BlockSpec((n, d), lambda j: (0, j)),
        out_shape=jax.ShapeDtypeStruct((n, dout), F32),
        compiler_params=_params("parallel"),
        name="modulation",
    )(c_all, w, b.reshape(1, dout))


def _pick_rows(n_first, first_ref, second_ref):
    return jnp.where(pl.program_id(0) < n_first, first_ref[...], second_ref[...])


def _two_group_specs(width, n_first):
    return [pl.BlockSpec((ROW_TILE, width), lambda i: (jnp.minimum(i, n_first - 1), 0)),
            pl.BlockSpec((ROW_TILE, width), lambda i: (0, 0))]


def _mod_specs(mods, k, tiles_per_seq):
    mod_p, mod_s = mods
    n_seq, d = mod_p.shape[1], mod_p.shape[3]
    return [pl.BlockSpec((None, None, 1, d), lambda i: (k, jnp.minimum(i // tiles_per_seq, n_seq - 1), 0, 0)),
            pl.BlockSpec((None, ROW_TILE, d), lambda i: (k, 0, 0))]


def _inproj_kernel(xp_ref, xs_ref, shp_ref, shs_ref, scp_ref, scs_ref, g_ref, w_ref,
                   za_ref, zb_ref, zg_ref, *, n_first):
    x = _pick_rows(n_first, xp_ref, xs_ref)
    h = x * lax.rsqrt(jnp.mean(x * x, axis=-1, keepdims=True) + NORM_EPS) * g_ref[...]
    h = h * (1.0 + _pick_rows(n_first, scp_ref, scs_ref)) + _pick_rows(n_first, shp_ref, shs_ref)
    hh = h.astype(BF16)
    wa = 4 * A_WIDTH
    for c in range(0, wa, 512):
        za_ref[:, c:c + 512] = _dot(hh, w_ref[:, c:c + 512])
    for c in range(0, 3 * B_WIDTH, 512):
        zb_ref[:, c:c + 512] = _dot(hh, w_ref[:, wa + c:wa + c + 512])
    wg = wa + 3 * B_WIDTH
    for c in range(0, 2 * D_MODEL, 512):
        zg_ref[:, c:c + 512] = _dot(hh, w_ref[:, wg + c:wg + c + 512])


def _inproj(xp, xs, mods, norm_g, w_in, tiles_per_seq):
    d = xp.shape[1]
    n_first = xp.shape[0] // ROW_TILE
    m = xp.shape[0] + xs.shape[0]
    w_hi = w_in.astype(BF16)
    wa = 4 * A_WIDTH
    const = lambda i: (0, 0)
    row = lambda i: (i, 0)
    return pl.pallas_call(
        functools.partial(_inproj_kernel, n_first=n_first),
        grid=(m // ROW_TILE,),
        in_specs=_two_group_specs(d, n_first) + _mod_specs(mods, 0, tiles_per_seq)
                 + _mod_specs(mods, 1, tiles_per_seq) + [
                  pl.BlockSpec((1, d), const),
                  pl.BlockSpec(w_hi.shape, const, pipeline_mode=pl.Buffered(1))],
        out_specs=[pl.BlockSpec((ROW_TILE, wa), row),
                   pl.BlockSpec((ROW_TILE, 3 * B_WIDTH), row),
                   pl.BlockSpec((ROW_TILE, 2 * D_MODEL), row)],
        out_shape=[jax.ShapeDtypeStruct((m, wa), F32),
                   jax.ShapeDtypeStruct((m, 3 * B_WIDTH), F32),
                   jax.ShapeDtypeStruct((m, 2 * D_MODEL), F32)],
        compiler_params=_params("parallel"),
        name="inproj",
    )(xp, xs, *mods, *mods, norm_g.reshape(1, d), w_hi)


def _rope_kernel(q_ref, k_ref, v_ref, c_ref, s1_ref, s2_ref, qo_ref, ko_ref, vo_ref):
    cos, s1, s2 = c_ref[...], s1_ref[...], s2_ref[...]
    half = ROT_DIM // 2

    def rope(x):
        up = pltpu.roll(x, B_WIDTH - half, axis=1)
        dn = pltpu.roll(x, half, axis=1)
        return x * cos + up * s1 + dn * s2

    q = rope(q_ref[...]) * (B_HEAD_DIM ** -0.5)
    k = rope(k_ref[...])
    v = v_ref[...]
    for h in range(B_HEADS):
        ls = slice(h * B_HEAD_DIM, (h + 1) * B_HEAD_DIM)
        qo_ref[0, h] = q[:, ls]
        ko_ref[0, h] = k[:, ls]
        vo_ref[0, h] = v[:, ls]


def _rope_tables(pos):
    half = ROT_DIM // 2
    inv = jnp.power(ROPE_THETA, -(jnp.arange(half, dtype=F32) * 2.0 / ROT_DIM))
    ang = pos.astype(F32)[:, None] * inv[None, :]
    cos, sin = jnp.cos(ang), jnp.sin(ang)
    t = pos.shape[0]
    rest = B_HEAD_DIM - ROT_DIM
    c = jnp.concatenate([cos, cos, jnp.ones((t, rest), F32)], axis=-1)
    s1 = jnp.concatenate([-sin, jnp.zeros((t, half + rest), F32)], axis=-1)
    s2 = jnp.concatenate([jnp.zeros((t, half), F32), sin, jnp.zeros((t, rest), F32)], axis=-1)
    return [jnp.tile(a, (1, B_HEADS)) for a in (c, s1, s2)]


def _rope_split(zb, row0, n, t, tile, pos):
    tabs = _rope_tables(pos)
    tps = t // tile
    blk0 = row0 // tile
    zmap = lambda c: (lambda b, s: (blk0 + b * tps + s, c))
    tmap = lambda b, s: (s, 0)
    omap = lambda b, s: (b, 0, s, 0)
    oshape = jax.ShapeDtypeStruct((n, B_HEADS, t, B_HEAD_DIM), F32)
    ospec = pl.BlockSpec((1, B_HEADS, tile, B_HEAD_DIM), omap)
    return pl.pallas_call(
        _rope_kernel,
        grid=(n, tps),
        in_specs=[pl.BlockSpec((tile, B_WIDTH), zmap(0)),
                  pl.BlockSpec((tile, B_WIDTH), zmap(1)),
                  pl.BlockSpec((tile, B_WIDTH), zmap(2)),
                  pl.BlockSpec((tile, B_WIDTH), tmap),
                  pl.BlockSpec((tile, B_WIDTH), tmap),
                  pl.BlockSpec((tile, B_WIDTH), tmap)],
        out_specs=[ospec, ospec, ospec],
        out_shape=[oshape, oshape, oshape],
        compiler_params=_params("parallel", "parallel"),
        name="rope_split",
    )(zb, zb, zb, *tabs)


def _hgrn_kernel(*refs, chunk, has_s0):
    aq_ref, af_ref, ai_ref, ag_ref, lb_ref, gain_ref = refs[:6]
    rest = refs[6:]
    s0_ref = None
    if has_s0:
        s0_ref, rest = rest[0], rest[1:]
    o_ref, so_ref, st_ref, q_s, b_s, k_s = rest
    t = pl.program_id(1)
    tb = aq_ref.shape[0]

    @pl.when(t == 0)
    def _():
        for h in range(A_HEADS):
            if has_s0:
                st_ref[h] = s0_ref[0, h].T
            else:
                st_ref[h] = jnp.zeros((A_VDIM, A_KDIM), F32)

    lb = lb_ref[...]
    f = lb + (1.0 - lb) * jax.nn.sigmoid(af_ref[...])
    logf = jnp.log(f)
    q_s[...] = _silu(aq_ref[...])
    k_s[...] = 1.0 - f
    row = lax.broadcasted_iota(I32, logf.shape, 0) & (chunk - 1)
    b = logf
    sh = 1
    while sh < chunk:
        b = b + jnp.where(row >= sh, pltpu.roll(b, sh, axis=0), 0.0)
        sh *= 2
    b_s[...] = b
    rowc = lax.broadcasted_iota(I32, (chunk, A_KDIM), 0)

    def one_chunk(ci, carry):
        r0 = pl.multiple_of(ci * chunk, chunk)
        rs = pl.ds(r0, chunk)
        for h in range(A_HEADS):
            ls = slice(h * A_KDIM, (h + 1) * A_KDIM)
            qc, bc, kc, vc = q_s[rs, ls], b_s[rs, ls], k_s[rs, ls], ai_ref[rs, ls]
            st = st_ref[h]
            bl = bc[chunk - 1:chunk, :]
            o = _dot_nt((qc * jnp.exp(bc)).astype(BF16), st.astype(BF16))
            grp = 8
            parts = [o[g * grp:(g + 1) * grp] for g in range(chunk // grp)]
            for s in range(chunk):
                for g in range(s // grp, chunk // grp):
                    rows = slice(g * grp, (g + 1) * grp)
                    diff = bc[rows] - bc[s:s + 1, :]
                    if g == s // grp:
                        diff = jnp.where(rowc[rows] >= s, diff, -jnp.inf)
                    e = jnp.exp(diff)
                    r = jnp.sum(qc[rows] * e * kc[s:s + 1, :], axis=1, keepdims=True)
                    parts[g] = parts[g] + r * vc[s:s + 1, :]
            o = jnp.concatenate(parts, axis=0) if len(parts) > 1 else parts[0]
            kp = kc * jnp.exp(bl - bc)
            st_ref[h] = st * jnp.exp(bl) + _dot_tn(vc.astype(BF16), kp.astype(BF16))
            o_ref[rs, ls] = o
        return carry

    n_chunks = tb // chunk
    lax.fori_loop(0, n_chunks, one_chunk, 0, unroll=4 if n_chunks % 4 == 0 else 1)

    gain = gain_ref[...]
    for h in range(A_HEADS):
        ls = slice(h * A_VDIM, (h + 1) * A_VDIM)
        oh = o_ref[:, ls]
        y = oh * lax.rsqrt(jnp.mean(oh * oh, axis=-1, keepdims=True) + NORM_EPS) * gain
        o_ref[:, ls] = y * _silu(ag_ref[:, ls])

    @pl.when(t == pl.num_programs(1) - 1)
    def _():
        for h in range(A_HEADS):
            so_ref[0, h] = st_ref[h].T


def _hgrn(za, lb, gain, row0, n, t, tile, chunk, s0=None):
    tps = t // tile
    blk0 = row0 // tile
    zmap = lambda c: (lambda b, s: (blk0 + b * tps + s, c))
    const = lambda b, s: (0, 0)
    in_specs = [pl.BlockSpec((tile, A_WIDTH), zmap(c)) for c in range(4)]
    in_specs += [pl.BlockSpec((1, A_WIDTH), const), pl.BlockSpec((1, A_VDIM), const)]
    args = [za, za, za, za, lb.reshape(1, A_WIDTH), gain.reshape(1, A_VDIM)]
    if s0 is not None:
        in_specs.append(pl.BlockSpec((1, A_HEADS, A_KDIM, A_VDIM), lambda b, s: (b, 0, 0, 0)))
        args.append(s0)
    return pl.pallas_call(
        functools.partial(_hgrn_kernel, chunk=chunk, has_s0=s0 is not None),
        grid=(n, tps),
        in_specs=in_specs,
        out_specs=[pl.BlockSpec((tile, A_WIDTH), lambda b, s: (b * tps + s, 0)),
                   pl.BlockSpec((1, A_HEADS, A_KDIM, A_VDIM), lambda b, s: (b, 0, 0, 0))],
        out_shape=[jax.ShapeDtypeStruct((n * t, A_WIDTH), F32),
                   jax.ShapeDtypeStruct((n, A_HEADS, A_KDIM, A_VDIM), F32)],
        scratch_shapes=[pltpu.VMEM((A_HEADS, A_VDIM, A_KDIM), F32),
                        pltpu.VMEM((tile, A_WIDTH), F32),
                        pltpu.VMEM((tile, A_WIDTH), F32),
                        pltpu.VMEM((tile, A_WIDTH), F32)],
        compiler_params=_params("parallel", "arbitrary"),
        name="hgrn2",
    )(*args)


MOBA_CHUNK_BLOCKS = 4
MOBA_HEADS_PER_STEP = 2
MOBA_QUERY_BLOCKS = 4


def _moba_prompt_kernel(q_ref, k_ref, v_ref, o_ref, km_ref, ka_ref, vt_ref, eye_ref):
    j = pl.program_id(2)
    blk = MOBA_BLOCK
    hp, nq, dh = q_ref.shape[1], q_ref.shape[2], q_ref.shape[3]
    n_chunks, kc = ka_ref.shape[0], ka_ref.shape[1]
    n_blk = n_chunks * MOBA_CHUNK_BLOCKS
    eye = (lax.broadcasted_iota(I32, (dh, dh), 0) == lax.broadcasted_iota(I32, (dh, dh), 1)).astype(BF16)

    @pl.when(j == 0)
    def _():
        eye_ref[...] = (lax.broadcasted_iota(I32, (blk, blk), 0)
                        == lax.broadcasted_iota(I32, (blk, blk), 1)).astype(BF16)
        for c in range(n_chunks):
            rows = slice(c * kc, (c + 1) * kc)
            key_blk = lax.broadcasted_iota(I32, (kc, n_blk), 0) // blk + c * MOBA_CHUNK_BLOCKS
            onehot = key_blk == lax.broadcasted_iota(I32, (kc, n_blk), 1)
            ka_ref[c, :, hp * dh:] = jnp.where(onehot, 1.0, 0.0).astype(BF16)
            for hh in range(hp):
                ka_ref[c, :, hh * dh:(hh + 1) * dh] = k_ref[0, hh, rows, :].astype(BF16)
                vt_ref[hh, c] = _dot_nt(eye, v_ref[0, hh, rows, :].astype(BF16)).astype(BF16)
        for hh in range(hp):
            for i in range(n_blk):
                km_ref[hh, i:i + 1, :] = jnp.mean(k_ref[0, hh, i * blk:(i + 1) * blk, :], axis=0, keepdims=True)

    bidx = lax.broadcasted_iota(I32, (n_blk, nq), 0)
    own = j * (nq // blk) + lax.broadcasted_iota(I32, (n_blk, nq), 1) // blk
    valid = bidx < own
    q_rows, biases = [], []
    for hh in range(hp):
        q = q_ref[0, hh]
        sc = jnp.where(valid, _dot3(km_ref[hh], q, _dot_nt), -jnp.inf)
        rank = jnp.zeros((n_blk, nq), I32)
        for i in range(n_blk - 1):
            row = sc[i:i + 1, :]
            rank = rank + jnp.where(row > sc, 1, jnp.where(row == sc, jnp.where(bidx > i, 1, 0), 0))
        biases.append(jnp.where(valid, jnp.where(rank < MOBA_TOPK, 0.0, NEG_BIG),
                                jnp.where(bidx == own, 0.0, NEG_BIG)))
        q_t = _dot_nt(eye, (q * LOG2_E).astype(BF16))
        zero = jnp.zeros((dh, nq), F32)
        q_rows.append(jnp.concatenate([q_t if g == hh else zero for g in range(hp)], axis=1))
    qa = jnp.concatenate(q_rows + [jnp.concatenate(biases, axis=1)], axis=0).astype(BF16)

    def scores(c):
        return jnp.concatenate([_dot(ka_ref[c, :kc // 2], qa), _dot(ka_ref[c, kc // 2:], qa)], axis=0)

    def weighted_values(c, p):
        pb = p.astype(BF16)
        return jnp.concatenate([_dot(vt_ref[hh, c], pb[:, hh * nq:(hh + 1) * nq]) for hh in range(hp)], axis=1)

    cj = (j * nq) // kc
    key_pos = lax.broadcasted_iota(I32, (kc, hp * nq), 0) + cj * kc
    q_pos = (lax.broadcasted_iota(I32, (kc, hp * nq), 1) & (nq - 1)) + j * nq
    s = jnp.where(key_pos <= q_pos, scores(cj), NEG_BIG)
    m = jnp.max(s, axis=0, keepdims=True)
    p = jnp.exp2(s - m)
    init = (m, jnp.sum(p, axis=0, keepdims=True), weighted_values(cj, p))

    def past_chunk(c, carry):
        m, l, acc = carry
        s = scores(c)
        m_new = jnp.maximum(m, jnp.max(s, axis=0, keepdims=True))
        p = jnp.exp2(s - m_new)
        alpha = jnp.exp2(m - m_new)
        return m_new, alpha * l + jnp.sum(p, axis=0, keepdims=True), alpha * acc + weighted_values(c, p)

    _, l, acc = lax.fori_loop(0, cj, past_chunk, init)
    o_t = (acc / l).astype(BF16)
    o_ref[...] = jnp.concatenate(
        [jnp.concatenate([_dot_nt(eye_ref[...], o_t[:, hh * nq + u * blk:hh * nq + (u + 1) * blk])
                          for u in range(nq // blk)], axis=0) for hh in range(hp)], axis=1).astype(BF16)


def _moba_prompt(q, k, v):
    b, h, s, dh = q.shape
    hp = MOBA_HEADS_PER_STEP
    n_blk = s // MOBA_BLOCK
    nq = MOBA_QUERY_BLOCKS * MOBA_BLOCK
    assert n_blk % MOBA_CHUNK_BLOCKS == 0 and h % hp == 0 and hp * dh == LANES
    assert MOBA_CHUNK_BLOCKS % MOBA_QUERY_BLOCKS == 0
    n_chunks = n_blk // MOBA_CHUNK_BLOCKS
    n_tiles = s // nq
    kc = MOBA_CHUNK_BLOCKS * MOBA_BLOCK
    full = pl.BlockSpec((1, hp, s, dh), lambda bi, hi, j: (bi, hi, 0, 0))
    return pl.pallas_call(
        _moba_prompt_kernel,
        grid=(b, h // hp, n_tiles),
        in_specs=[pl.BlockSpec((1, hp, nq, dh), lambda bi, hi, j: (bi, hi, j, 0)), full, full],
        out_specs=pl.BlockSpec((nq, hp * dh), lambda bi, hi, j: (bi * n_tiles + j, hi)),
        out_shape=jax.ShapeDtypeStruct((b * s, h * dh), BF16),
        scratch_shapes=[pltpu.VMEM((hp, n_blk, dh), F32),
                        pltpu.VMEM((n_chunks, kc, hp * dh + n_blk), BF16),
                        pltpu.VMEM((hp, n_chunks, dh, kc), BF16),
                        pltpu.VMEM((MOBA_BLOCK, MOBA_BLOCK), BF16)],
        compiler_params=_params("parallel", "parallel", "arbitrary"),
        name="moba_prompt",
    )(q, k, v)


SELECT_PAGES_PER_STEP = 16
PAGE_RING = 3


def _select_kernel(pt_ref, q_ref, ck_ref, o_ref, buf, pm_ref, sem, *, layer, pages_per_seq):
    n_grp = pl.num_programs(1)
    g = pl.program_id(1)
    step = pl.program_id(0) * n_grp + g
    n_steps = pl.num_programs(0) * n_grp
    ring = buf.shape[0]
    slot = step % ring
    pps, n_heads, dh, ps = buf.shape[1], buf.shape[2], buf.shape[3], buf.shape[4]
    t = q_ref.shape[2]

    def page_copy(st, sl, i, pg):
        return pltpu.make_async_copy(ck_ref.at[layer, pg], buf.at[sl, i], sem.at[sl])

    def fetch(st):
        base = (st // n_grp) * pages_per_seq + (st % n_grp) * pps
        for i in range(pps):
            page_copy(st, st % ring, i, pt_ref[base + i]).start()

    for ahead in range(ring - 1):
        @pl.when((step == 0) & (ahead < n_steps))
        def _():
            fetch(ahead)

    @pl.when(step + ring - 1 < n_steps)
    def _():
        fetch(step + ring - 1)

    for i in range(pps):
        page_copy(step, slot, i, 0).wait()

    bps = pps // PAGES_PER_BLOCK
    diag = lax.broadcasted_iota(I32, (dh, ps), 0) == lax.broadcasted_iota(I32, (dh, ps), 1)
    blocks = []
    for b in range(bps):
        xb = buf[slot, b * PAGES_PER_BLOCK]
        for pi in range(1, PAGES_PER_BLOCK):
            xb = xb + buf[slot, b * PAGES_PER_BLOCK + pi]
        blocks.append(xb)
    x = jnp.stack(blocks, axis=0)
    r = jnp.sum(x, axis=-1, keepdims=True) * (1.0 / (ps * PAGES_PER_BLOCK))
    bm_step = jnp.sum(jnp.where(diag, r, 0.0), axis=2)
    pm_ref[pl.ds(pl.multiple_of(g * bps * n_heads, bps * n_heads), bps * n_heads), :] = (
        bm_step.reshape(bps * n_heads, ps))

    @pl.when(g == n_grp - 1)
    def _():
        n_blk = pages_per_seq // PAGES_PER_BLOCK
        lane = lax.broadcasted_iota(I32, (t, n_blk), 1).astype(F32)
        for h in range(n_heads):
            bm = pm_ref[pl.ds(h, n_blk, stride=n_heads), :]
            sc = _dot3(q_ref[0, h], bm[:, :dh], _dot_nt)
            res = jnp.zeros((t, n_blk), F32)
            for k in range(MOBA_TOPK):
                m = jnp.max(sc, axis=1, keepdims=True)
                idx = jnp.min(jnp.where(sc == m, lane, float(n_blk)), axis=1, keepdims=True)
                res = jnp.where(lane == k, idx, res)
                sc = jnp.where(lane == idx, -jnp.inf, sc)
            o_ref[0, h] = res.astype(I32)


def _select_blocks(q, cache_kt, page_table, layer):
    n, h, t, dh = q.shape
    ps = cache_kt.shape[-1]
    pages_per_seq = page_table.shape[1]
    pps = SELECT_PAGES_PER_STEP
    assert pages_per_seq % pps == 0 and pages_per_seq % PAGES_PER_BLOCK == 0 and dh <= ps
    n_blk = pages_per_seq // PAGES_PER_BLOCK
    return pl.pallas_call(
        functools.partial(_select_kernel, layer=layer, pages_per_seq=pages_per_seq),
        grid_spec=pltpu.PrefetchScalarGridSpec(
            num_scalar_prefetch=1,
            grid=(n, pages_per_seq // pps),
            in_specs=[pl.BlockSpec((1, h, t, dh), lambda i, g, pt: (i, 0, 0, 0)),
                      pl.BlockSpec(memory_space=pl.ANY)],
            out_specs=pl.BlockSpec((1, h, t, n_blk), lambda i, g, pt: (i, 0, 0, 0)),
            scratch_shapes=[pltpu.VMEM((PAGE_RING, pps, h, dh, ps), F32),
                            pltpu.VMEM((n_blk * h, ps), F32),
                            pltpu.SemaphoreType.DMA((PAGE_RING,))]),
        out_shape=jax.ShapeDtypeStruct((n, h, t, n_blk), I32),
        compiler_params=_params("arbitrary", "arbitrary"),
        name="moba_select",
    )(page_table.reshape(-1), q, cache_kt)


def _moba_sample_kernel(sel_ref, pt_ref, q_ref, kn_ref, vn_ref, ck_ref, cv_ref, o_ref, kbuf, vbuf, sem, *,
                        layer, pages_per_seq):
    n_heads = pl.num_programs(1)
    step = pl.program_id(0) * n_heads + pl.program_id(1)
    n_steps = pl.num_programs(0) * n_heads
    t = q_ref.shape[2]
    ring, n_pages = kbuf.shape[0], kbuf.shape[1]
    pages_per_q = n_pages // t
    slot = step % ring

    def page_copies(st, sl, i, pg):
        hh = st % n_heads
        return (pltpu.make_async_copy(ck_ref.at[layer, pg, hh], kbuf.at[sl, i], sem.at[0, sl]),
                pltpu.make_async_copy(cv_ref.at[layer, pg, hh], vbuf.at[sl, i], sem.at[1, sl]))

    def fetch(st):
        sl = st % ring
        pt_base = (st // n_heads) * pages_per_seq
        for b in range(n_pages // PAGES_PER_BLOCK):
            blk = sel_ref[st * (n_pages // PAGES_PER_BLOCK) + b]
            for pi in range(PAGES_PER_BLOCK):
                pg = pt_ref[pt_base + blk * PAGES_PER_BLOCK + pi]
                for thread, cp in enumerate(page_copies(st, sl, b * PAGES_PER_BLOCK + pi, pg)):
                    cp.start(priority=thread)

    for ahead in range(ring - 1):
        @pl.when((step == 0) & (ahead < n_steps))
        def _():
            fetch(ahead)

    @pl.when(step + ring - 1 < n_steps)
    def _():
        fetch(step + ring - 1)

    for i in range(n_pages):
        for cp in page_copies(step, slot, i, 0):
            cp.wait()

    q = q_ref[0, 0]
    qb = q.astype(BF16)
    def pages_of(buf_ref, tq):
        return jnp.concatenate([buf_ref[slot, tq * pages_per_q + i].astype(BF16) for i in range(pages_per_q)],
                               axis=1)

    s = jnp.concatenate([_dot(qb, pages_of(kbuf, tq)) for tq in range(t)], axis=1)
    per_q = pages_per_q * PAGE_SIZE
    col = lax.broadcasted_iota(I32, s.shape, 1)
    lo = lax.broadcasted_iota(I32, s.shape, 0) * per_q
    own = (col >= lo) & (col < lo + per_q)
    sn = _dot_nt(q, kn_ref[0, 0])
    causal = lax.broadcasted_iota(I32, (t, t), 1) <= lax.broadcasted_iota(I32, (t, t), 0)
    m = jnp.maximum(jnp.max(jnp.where(own, s, NEG_BIG), axis=1, keepdims=True),
                    jnp.max(jnp.where(causal, sn, NEG_BIG), axis=1, keepdims=True))
    p = jnp.where(own, jnp.exp(s - m), 0.0).astype(BF16)
    pn = jnp.where(causal, jnp.exp(sn - m), 0.0)
    l = jnp.sum(p.astype(F32), axis=1, keepdims=True) + jnp.sum(pn, axis=1, keepdims=True)
    o = _dot(pn, vn_ref[0, 0])
    for tq in range(t):
        o = o + _dot_nt(p[:, tq * per_q:(tq + 1) * per_q], pages_of(vbuf, tq))
    o_ref[0, 0] = o / l


def _moba_sample(q, k_new, v_new, cache_kt, cache_vt, sel, page_table, layer):
    n, h, t, dh = q.shape
    n_pages = t * MOBA_TOPK * PAGES_PER_BLOCK
    blk = pl.BlockSpec((1, 1, t, dh), lambda i, j, sl, pt: (i, j, 0, 0))
    anyspec = pl.BlockSpec(memory_space=pl.ANY)
    return pl.pallas_call(
        functools.partial(_moba_sample_kernel, layer=layer, pages_per_seq=page_table.shape[1]),
        grid_spec=pltpu.PrefetchScalarGridSpec(
            num_scalar_prefetch=2,
            grid=(n, h),
            in_specs=[blk, blk, blk, anyspec, anyspec],
            out_specs=blk,
            scratch_shapes=[pltpu.VMEM((PAGE_RING, n_pages, dh, PAGE_SIZE), F32),
                            pltpu.VMEM((PAGE_RING, n_pages, dh, PAGE_SIZE), F32),
                            pltpu.SemaphoreType.DMA((2, PAGE_RING))]),
        out_shape=jax.ShapeDtypeStruct((n, h, t, dh), F32),
        compiler_params=_params("arbitrary", "arbitrary"),
        name="moba_sample",
    )(sel.reshape(-1), page_table.reshape(-1), q, k_new, v_new, cache_kt, cache_vt)


def _merge_kernel(xp_ref, xs_ref, oap_ref, oas_ref, obp_ref, obs_ref, ga_ref, gb_ref, g1p_ref, g1s_ref,
                  shp_ref, shs_ref, scp_ref, scs_ref, nf_ref, wpa_ref, wpb_ref, wo_ref, wr_ref, br_ref,
                  x1_ref, h2_ref, eid_ref, gw_ref, *, n_first):
    pa = _dot(_pick_rows(n_first, oap_ref, oas_ref).astype(BF16), wpa_ref[...])
    pb = _dot(_pick_rows(n_first, obp_ref, obs_ref), wpb_ref[...])
    mix = jax.nn.sigmoid(ga_ref[...]) * pa + jax.nn.sigmoid(gb_ref[...]) * pb
    x1 = (_pick_rows(n_first, xp_ref, xs_ref)
          + _pick_rows(n_first, g1p_ref, g1s_ref) * _dot(mix.astype(BF16), wo_ref[...]))
    x1_ref[...] = x1
    h2 = x1 * lax.rsqrt(jnp.mean(x1 * x1, axis=-1, keepdims=True) + NORM_EPS) * nf_ref[...]
    h2 = h2 * (1.0 + _pick_rows(n_first, scp_ref, scs_ref)) + _pick_rows(n_first, shp_ref, shs_ref)
    hb = h2.astype(BF16).astype(F32)
    half = hb.shape[1] // 2
    words = pltpu.bitcast(hb[:, :half], jnp.uint32) | (pltpu.bitcast(hb[:, half:], jnp.uint32) >> 16)
    for s in range(h2_ref.shape[1]):
        h2_ref[:, s, :] = words[:, s * LANES:(s + 1) * LANES]

    logits = _dot3(h2, wr_ref[...]) + br_ref[...]
    lane = lax.broadcasted_iota(I32, logits.shape, 1)
    lanef = lane.astype(F32)
    is_g = lane < N_GROUPS
    gl = jnp.where(is_g, logits, -jnp.inf)
    gmax = jnp.max(gl, axis=1, keepdims=True)
    g_sel = jnp.min(jnp.where(gl == gmax, lanef, float(LANES)), axis=1, keepdims=True)
    p_group = 1.0 / jnp.sum(jnp.exp(gl - gmax), axis=1, keepdims=True)
    e_lo = N_GROUPS + g_sel * EXPERTS_PER_GROUP
    in_grp = (lanef >= e_lo) & (lanef < e_lo + EXPERTS_PER_GROUP)
    el = jnp.where(in_grp, logits, -jnp.inf)
    m1 = jnp.max(el, axis=1, keepdims=True)
    i1 = jnp.min(jnp.where(el == m1, lanef, float(LANES)), axis=1, keepdims=True)
    el2 = jnp.where(lanef == i1, -jnp.inf, el)
    m2 = jnp.max(el2, axis=1, keepdims=True)
    i2 = jnp.min(jnp.where(el2 == m2, lanef, float(LANES)), axis=1, keepdims=True)
    e2 = jnp.exp(m2 - m1)
    w1 = p_group / (1.0 + e2)
    w2 = p_group * e2 / (1.0 + e2)
    eid = jnp.where(lane == 0, i1 - N_GROUPS, jnp.where(lane == 1, i2 - N_GROUPS, 0.0))
    eid_ref[...] = eid.astype(I32)
    gw_ref[...] = jnp.where(lane == 0, w1, jnp.where(lane == 1, w2, 0.0))


def _merge(xp, xs, oa_p, oa_s, ob_p, ob_s, zg, mods, norm_ffn, wpa, wpb, wo, w_router, b_router,
           tiles_per_seq):
    d = xp.shape[1]
    n_first = xp.shape[0] // ROW_TILE
    m = xp.shape[0] + xs.shape[0]
    const = lambda i: (0, 0)
    row = lambda i: (i, 0)
    wspec = lambda w: pl.BlockSpec(w.shape, const, pipeline_mode=pl.Buffered(1))
    return pl.pallas_call(
        functools.partial(_merge_kernel, n_first=n_first),
        grid=(m // ROW_TILE,),
        in_specs=_two_group_specs(d, n_first) + _two_group_specs(A_WIDTH, n_first)
                 + _two_group_specs(B_WIDTH, n_first) + [
                  pl.BlockSpec((ROW_TILE, d), lambda i: (i, 0)),
                  pl.BlockSpec((ROW_TILE, d), lambda i: (i, 1))]
                 + _mod_specs(mods, 2, tiles_per_seq) + _mod_specs(mods, 3, tiles_per_seq)
                 + _mod_specs(mods, 4, tiles_per_seq) + [
                  pl.BlockSpec((1, d), const),
                  wspec(wpa), wspec(wpb), wspec(wo), wspec(w_router),
                  pl.BlockSpec((1, LANES), const)],
        out_specs=[pl.BlockSpec((ROW_TILE, d), row),
                   pl.BlockSpec((ROW_TILE, d // (2 * LANES), LANES), lambda i: (i, 0, 0)),
                   pl.BlockSpec((ROW_TILE, LANES), row),
                   pl.BlockSpec((ROW_TILE, LANES), row)],
        out_shape=[jax.ShapeDtypeStruct((m, d), F32),
                   jax.ShapeDtypeStruct((m, d // (2 * LANES), LANES), jnp.uint32),
                   jax.ShapeDtypeStruct((m, LANES), I32),
                   jax.ShapeDtypeStruct((m, LANES), F32)],
        compiler_params=_params("parallel"),
        name="merge_router",
    )(xp, xs, oa_p, oa_s, ob_p, ob_s, zg, zg, *mods, *mods, *mods,
      norm_ffn.reshape(1, d), wpa, wpb, wo, w_router, b_router)


MOE_VMEM_LIMIT_BYTES = 58 * 1024 * 1024


def _moe_kernel(be_ref, nu_ref, tok_ref, h_ref, wg_ref, wu_ref, wd_ref, y_ref, xbuf, wg_s, wu_s, wd_s):
    i = pl.program_id(0)
    n_used = nu_ref[0]
    n_sub = xbuf.shape[0] // MOE_TILE
    prev = be_ref[jnp.maximum(i - 1, 0)]
    fresh = (i == 0) | (be_ref[i] != prev)

    @pl.when(fresh)
    def _():
        wg_s[...] = wg_ref[...].astype(BF16)
        wu_s[...] = wu_ref[...].astype(BF16)
        wd_s[...] = wd_ref[...].astype(BF16)

    @pl.when(i < n_used)
    def _():
        for r in range(MOE_TILE):
            src = pl.multiple_of(tok_ref[i * MOE_TILE + r] * n_sub, n_sub)
            xbuf[r * n_sub:(r + 1) * n_sub, :] = h_ref[pl.ds(src, n_sub), :]
        words = [xbuf[pl.ds(s, MOE_TILE, stride=n_sub), :] for s in range(n_sub)]
        xb = jnp.concatenate(
            [pltpu.bitcast(w & jnp.uint32(0xFFFF0000), F32).astype(BF16) for w in words]
            + [pltpu.bitcast(w << 16, F32).astype(BF16) for w in words], axis=1)
        hid = _silu(_dot(xb, wg_s[...])) * _dot(xb, wu_s[...])
        y_ref[...] = _dot(hid.astype(BF16), wd_s[...])

    @pl.when(i >= n_used)
    def _():
        y_ref[...] = jnp.zeros(y_ref.shape, F32)


def _moe(h, row_tok, block_expert, n_used, w_gate, w_up, w_down, layer):
    n_tok, n_sub, lanes = h.shape
    d = 2 * n_sub * lanes
    n_rows = row_tok.shape[0]
    de = w_gate.shape[-1]
    return pl.pallas_call(
        _moe_kernel,
        grid_spec=pltpu.PrefetchScalarGridSpec(
            num_scalar_prefetch=3,
            grid=(n_rows // MOE_TILE,),
            in_specs=[pl.BlockSpec((n_tok * n_sub, lanes), lambda i, be, nu, tk: (0, 0),
                                   pipeline_mode=pl.Buffered(1)),
                      pl.BlockSpec((None, None, d, de), lambda i, be, nu, tk: (layer, be[i], 0, 0)),
                      pl.BlockSpec((None, None, d, de), lambda i, be, nu, tk: (layer, be[i], 0, 0)),
                      pl.BlockSpec((None, None, de, d), lambda i, be, nu, tk: (layer, be[i], 0, 0))],
            out_specs=pl.BlockSpec((MOE_TILE, d), lambda i, be, nu, tk: (i, 0)),
            scratch_shapes=[pltpu.VMEM((MOE_TILE * n_sub, lanes), jnp.uint32),
                            pltpu.VMEM((d, de), BF16), pltpu.VMEM((d, de), BF16), pltpu.VMEM((de, d), BF16)]),
        out_shape=jax.ShapeDtypeStruct((n_rows, d), F32),
        compiler_params=pltpu.CompilerParams(dimension_semantics=("arbitrary",),
                                             vmem_limit_bytes=MOE_VMEM_LIMIT_BYTES),
        name="moe_experts",
    )(block_expert, n_used, row_tok, h.reshape(n_tok * n_sub, lanes), w_gate, w_up, w_down)


def _final_kernel(x1_ref, y0_ref, y1_ref, gw_ref, g2p_ref, g2s_ref, nf_ref, op_ref, os_ref, *, last, n_first):
    gw = gw_ref[...]
    f = gw[:, 0:1] * y0_ref[...] + gw[:, 1:2] * y1_ref[...]
    x2 = x1_ref[...] + _pick_rows(n_first, g2p_ref, g2s_ref) * f
    if last:
        x2 = x2 * lax.rsqrt(jnp.mean(x2 * x2, axis=-1, keepdims=True) + NORM_EPS) * nf_ref[...]
    i = pl.program_id(0)

    @pl.when(i < n_first)
    def _():
        op_ref[...] = x2

    @pl.when(i >= n_first)
    def _():
        os_ref[...] = x2


def _final(x1, y0, y1, gw, mods, norm_final, tiles_per_seq, last, n_first):
    m, d = x1.shape
    row = lambda i: (i, 0)
    rs = pl.BlockSpec((ROW_TILE, d), row)
    return pl.pallas_call(
        functools.partial(_final_kernel, last=last, n_first=n_first),
        grid=(m // ROW_TILE,),
        in_specs=[rs, rs, rs, pl.BlockSpec((ROW_TILE, LANES), row)] + _mod_specs(mods, 5, tiles_per_seq) + [
                  pl.BlockSpec((1, d), lambda i: (0, 0))],
        out_specs=_two_group_specs(d, n_first),
        out_shape=[jax.ShapeDtypeStruct((n_first * ROW_TILE, d), F32),
                   jax.ShapeDtypeStruct((m - n_first * ROW_TILE, d), F32)],
        compiler_params=_params("arbitrary"),
        name="combine_final",
    )(x1, y0, y1, gw, *mods, norm_final.reshape(1, d))


def _dispatch(eid):
    m = eid.shape[0]
    n_assign = m * EXPERT_TOPK
    flat_e = eid.reshape(-1)
    flat_tok = jnp.repeat(jnp.arange(m, dtype=I32), EXPERT_TOPK)
    onehot = (flat_e[:, None] == jnp.arange(N_EXPERTS, dtype=I32)[None, :]).astype(I32)
    csum = jnp.cumsum(onehot, axis=0)
    rank = jnp.take_along_axis(csum, flat_e[:, None], axis=1)[:, 0] - 1
    counts = csum[-1]
    padded = (counts + MOE_TILE - 1) // MOE_TILE * MOE_TILE
    pad_end = jnp.cumsum(padded)
    pad_start = pad_end - padded
    dest = (pad_start[flat_e] + rank).astype(I32)
    n_blocks = -(-(n_assign + N_EXPERTS * (MOE_TILE - 1)) // MOE_TILE)
    row_tok = jnp.zeros((n_blocks * MOE_TILE,), I32).at[dest].set(flat_tok)
    blk_start = jnp.arange(n_blocks, dtype=I32) * MOE_TILE
    block_expert = jnp.minimum(jnp.sum((pad_end[None, :] <= blk_start[:, None]).astype(I32), axis=1),
                               N_EXPERTS - 1).astype(I32)
    n_used = (pad_end[-1:] // MOE_TILE).astype(I32)
    return dest.reshape(m, EXPERT_TOPK), row_tok, block_expert, n_used


def _mod_tiles(mod, n_prompt, reps):
    mod_p = mod[:n_prompt].transpose(1, 0, 2)[:, :, None, :]
    mod_s = jnp.repeat(mod[n_prompt:], reps, axis=0).transpose(1, 0, 2)
    return mod_p, mod_s


def kernel(x_prompt, x_sample, c_prompt, c_sample, cache_k, cache_v, state_hgrn, page_table,
           w_ada, b_ada, norm_mix, norm_ffn, w_in, hgrn_lb_logits, hgrn_norm,
           w_proj_a, w_proj_b, w_out, w_group, b_group, w_expert_router, b_expert_router,
           w_gate, w_up, w_down, norm_final):
    bp, tp, d = x_prompt.shape
    bs, ts, _ = x_sample.shape
    mp_rows, ms_rows = bp * tp, bs * ts
    assert ms_rows == ROW_TILE and tp % ROW_TILE == 0
    m = mp_rows + ms_rows
    tiles_per_seq = tp // ROW_TILE
    n_full = PAST_LEN // MOBA_BLOCK
    assert PAST_LEN % MOBA_BLOCK == 0 and n_full >= MOBA_TOPK

    xp, xs = x_prompt.reshape(mp_rows, d), x_sample.reshape(ms_rows, d)
    c_all = jnp.concatenate([c_prompt, c_sample, jnp.zeros((-(bp + bs) % 8, d), F32)], axis=0)
    lb_all = jnp.cumsum(jax.nn.softmax(hgrn_lb_logits.astype(F32), axis=0), axis=0)
    cache_kt = cache_k.transpose(0, 1, 2, 4, 3)
    cache_vt = cache_v.transpose(0, 1, 2, 4, 3)
    kp_l, vp_l, sp_l, ks_l, vs_l, ss_l = [], [], [], [], [], []

    for l in range(DEPTH):
        mod = _modulation(c_all, w_ada[l], b_ada[l])[:bp + bs].reshape(bp + bs, 6, d)
        mt = _mod_tiles(mod, bp, ts)
        za, zb, zg = _inproj(xp, xs, mt, norm_mix[l], w_in[l], tiles_per_seq)

        oa_p, st_p = _hgrn(za, lb_all[l], hgrn_norm[l], 0, bp, tp, ROW_TILE, HGRN_CHUNK)
        oa_s, st_s = _hgrn(za, lb_all[l], hgrn_norm[l], mp_rows, bs, ts, ts, ts, s0=state_hgrn[l])

        q_p, k_p, v_p = _rope_split(zb, 0, bp, tp, 512, jnp.arange(tp))
        q_s, k_s, v_s = _rope_split(zb, mp_rows, bs, ts, ts, PAST_LEN + jnp.arange(ts))
        ob_p = _moba_prompt(q_p, k_p, v_p)
        sel = _select_blocks(q_s, cache_kt, page_table[:, :n_full * PAGES_PER_BLOCK], l)[..., :MOBA_TOPK]
        ob_s = _moba_sample(q_s, k_s, v_s, cache_kt, cache_vt, sel, page_table, l)
        ob_s = ob_s.transpose(0, 2, 1, 3).reshape(ms_rows, B_WIDTH).astype(BF16)

        w_router = jnp.concatenate(
            [w_group[l], w_expert_router[l].transpose(1, 0, 2).reshape(d, N_EXPERTS),
             jnp.zeros((d, LANES - N_GROUPS - N_EXPERTS), F32)], axis=1)
        b_router = jnp.concatenate(
            [b_group[l], b_expert_router[l].reshape(-1),
             jnp.zeros((LANES - N_GROUPS - N_EXPERTS,), F32)]).reshape(1, LANES)
        x1, h2, eid, gw = _merge(xp, xs, oa_p, oa_s, ob_p, ob_s, zg, mt, norm_ffn[l], w_proj_a[l].astype(BF16),
                                 w_proj_b[l].astype(BF16), w_out[l].astype(BF16), w_router, b_router,
                                 tiles_per_seq)

        dest, row_tok, block_expert, n_used = _dispatch(eid[:, :EXPERT_TOPK])
        ys = _moe(h2, row_tok, block_expert, n_used, w_gate, w_up, w_down, l)
        xp, xs = _final(x1, ys[dest[:, 0]], ys[dest[:, 1]], gw, mt, norm_final, tiles_per_seq,
                        l == DEPTH - 1, mp_rows // ROW_TILE)

        kp_l.append(k_p); vp_l.append(v_p); sp_l.append(st_p)
        ks_l.append(k_s); vs_l.append(v_s); ss_l.append(st_s)

    y_prompt = xp.reshape(bp, tp, d)
    y_sample = xs.reshape(bs, ts, d)
    return (y_prompt, y_sample, jnp.stack(kp_l), jnp.stack(vp_l), jnp.stack(sp_l),
            jnp.stack(ks_l), jnp.stack(vs_l), jnp.stack(ss_l))
```

```python
import functools

import numpy as np
import jax
import jax.numpy as jnp
from jax import lax
from jax.experimental import pallas as pl
from jax.experimental.pallas import tpu as pltpu

F32 = jnp.float32
BF16 = jnp.bfloat16
I32 = jnp.int32

D_MODEL = 1024
DEPTH = 1
PAST_LEN = 16384
PAGE_SIZE = 128
A_HEADS = 4
A_KDIM = 128
A_VDIM = 128
A_WIDTH = A_HEADS * A_KDIM
B_HEADS = 8
B_HEAD_DIM = 64
B_WIDTH = B_HEADS * B_HEAD_DIM
MOBA_BLOCK = 256
MOBA_TOPK = 3
ROT_DIM = B_HEAD_DIM // 4
ROPE_THETA = 500000.0
N_GROUPS = 4
EXPERTS_PER_GROUP = 8
N_EXPERTS = N_GROUPS * EXPERTS_PER_GROUP
EXPERT_TOPK = 2
D_EXPERT = D_MODEL // 2
NORM_EPS = 1e-6

PAGES_PER_BLOCK = MOBA_BLOCK // PAGE_SIZE
ROW_TILE = 256
HGRN_CHUNK = 16
MOE_TILE = 256
LANES = 128
NEG_BIG = -1e30
LOG2_E = 1.4426950408889634
VMEM_LIMIT_BYTES = 52 * 1024 * 1024


def _params(*sem):
    return pltpu.CompilerParams(dimension_semantics=sem, vmem_limit_bytes=VMEM_LIMIT_BYTES)


def _dot(a, b):
    return jnp.dot(a, b, preferred_element_type=F32)


def _dot_nt(a, b):
    return lax.dot_general(a, b, (((1,), (1,)), ((), ())), preferred_element_type=F32)


def _dot_tn(a, b):
    return lax.dot_general(a, b, (((0,), (0,)), ((), ())), preferred_element_type=F32)


def _split(a):
    hi = a.astype(BF16)
    return hi, (a - hi.astype(F32)).astype(BF16)


def _dot3(a, b, dot=_dot):
    ah, al = _split(a)
    bh, bl = _split(b)
    return dot(ah, bh) + (dot(ah, bl) + dot(al, bh))


def _silu(x):
    return x * jax.nn.sigmoid(x)


def _mod_kernel(c_ref, w_ref, b_ref, o_ref):
    o_ref[...] = _dot3(_silu(c_ref[...]), w_ref[...]) + b_ref[...]


def _modulation(c_all, w, b):
    n = c_all.shape[0]
    d, dout = w.shape
    return pl.pallas_call(
        _mod_kernel,
        grid=(dout // d,),
        in_specs=[pl.BlockSpec((n, d), lambda j: (0, 0)),
                  pl.BlockSpec((d, d), lambda j: (0, j)),
                  pl.BlockSpec((1, d), lambda j: (0, j))],
        out_specs=pl.BlockSpec((n, d), lambda j: (0, j)),
        out_shape=jax.ShapeDtypeStruct((n, dout), F32),
        compiler_params=_params("parallel"),
        name="modulation",
    )(c_all, w, b.reshape(1, dout))


def _pick_rows(n_first, first_ref, second_ref):
    return jnp.where(pl.program_id(0) < n_first, first_ref[...], second_ref[...])


def _two_group_specs(width, n_first):
    return [pl.BlockSpec((ROW_TILE, width), lambda i: (jnp.minimum(i, n_first - 1), 0)),
            pl.BlockSpec((ROW_TILE, width), lambda i: (0, 0))]


def _mod_specs(mods, k, tiles_per_seq):
    mod_p, mod_s = mods
    n_seq, d = mod_p.shape[1], mod_p.shape[3]
    return [pl.BlockSpec((None, None, 1, d), lambda i: (k, jnp.minimum(i // tiles_per_seq, n_seq - 1), 0, 0)),
            pl.BlockSpec((None, ROW_TILE, d), lambda i: (k, 0, 0))]


def _inproj_kernel(xp_ref, xs_ref, shp_ref, shs_ref, scp_ref, scs_ref, g_ref, w_ref,
                   za_ref, zb_ref, zg_ref, *, n_first):
    x = _pick_rows(n_first, xp_ref, xs_ref)
    h = x * lax.rsqrt(jnp.mean(x * x, axis=-1, keepdims=True) + NORM_EPS) * g_ref[...]
    h = h * (1.0 + _pick_rows(n_first, scp_ref, scs_ref)) + _pick_rows(n_first, shp_ref, shs_ref)
    hh = h.astype(BF16)
    wa = 4 * A_WIDTH
    for c in range(0, wa, 512):
        za_ref[:, c:c + 512] = _dot(hh, w_ref[:, c:c + 512])
    for c in range(0, 3 * B_WIDTH, 512):
        zb_ref[:, c:c + 512] = _dot(hh, w_ref[:, wa + c:wa + c + 512])
    wg = wa + 3 * B_WIDTH
    for c in range(0, 2 * D_MODEL, 512):
        zg_ref[:, c:c + 512] = _dot(hh, w_ref[:, wg + c:wg + c + 512])


def _inproj(xp, xs, mods, norm_g, w_in, tiles_per_seq):
    d = xp.shape[1]
    n_first = xp.shape[0] // ROW_TILE
    m = xp.shape[0] + xs.shape[0]
    w_hi = w_in.astype(BF16)
    wa = 4 * A_WIDTH
    const = lambda i: (0, 0)
    row = lambda i: (i, 0)
    return pl.pallas_call(
        functools.partial(_inproj_kernel, n_first=n_first),
        grid=(m // ROW_TILE,),
        in_specs=_two_group_specs(d, n_first) + _mod_specs(mods, 0, tiles_per_seq)
                 + _mod_specs(mods, 1, tiles_per_seq) + [
                  pl.BlockSpec((1, d), const),
                  pl.BlockSpec(w_hi.shape, const, pipeline_mode=pl.Buffered(1))],
        out_specs=[pl.BlockSpec((ROW_TILE, wa), row),
                   pl.BlockSpec((ROW_TILE, 3 * B_WIDTH), row),
                   pl.BlockSpec((ROW_TILE, 2 * D_MODEL), row)],
        out_shape=[jax.ShapeDtypeStruct((m, wa), F32),
                   jax.ShapeDtypeStruct((m, 3 * B_WIDTH), F32),
                   jax.ShapeDtypeStruct((m, 2 * D_MODEL), F32)],
        compiler_params=_params("parallel"),
        name="inproj",
    )(xp, xs, *mods, *mods, norm_g.reshape(1, d), w_hi)


def _rope_kernel(q_ref, k_ref, v_ref, c_ref, s1_ref, s2_ref, qo_ref, ko_ref, vo_ref):
    cos, s1, s2 = c_ref[...], s1_ref[...], s2_ref[...]
    half = ROT_DIM // 2

    def rope(x):
        up = pltpu.roll(x, B_WIDTH - half, axis=1)
        dn = pltpu.roll(x, half, axis=1)
        return x * cos + up * s1 + dn * s2

    q = rope(q_ref[...]) * (B_HEAD_DIM ** -0.5)
    k = rope(k_ref[...])
    v = v_ref[...]
    for h in range(B_HEADS):
        ls = slice(h * B_HEAD_DIM, (h + 1) * B_HEAD_DIM)
        qo_ref[0, h] = q[:, ls]
        ko_ref[0, h] = k[:, ls]
        vo_ref[0, h] = v[:, ls]


def _rope_tables(pos):
    half = ROT_DIM // 2
    inv = jnp.power(ROPE_THETA, -(jnp.arange(half, dtype=F32) * 2.0 / ROT_DIM))
    ang = pos.astype(F32)[:, None] * inv[None, :]
    cos, sin = jnp.cos(ang), jnp.sin(ang)
    t = pos.shape[0]
    rest = B_HEAD_DIM - ROT_DIM
    c = jnp.concatenate([cos, cos, jnp.ones((t, rest), F32)], axis=-1)
    s1 = jnp.concatenate([-sin, jnp.zeros((t, half + rest), F32)], axis=-1)
    s2 = jnp.concatenate([jnp.zeros((t, half), F32), sin, jnp.zeros((t, rest), F32)], axis=-1)
    return [jnp.tile(a, (1, B_HEADS)) for a in (c, s1, s2)]


def _rope_split(zb, row0, n, t, tile, pos):
    tabs = _rope_tables(pos)
    tps = t // tile
    blk0 = row0 // tile
    zmap = lambda c: (lambda b, s: (blk0 + b * tps + s, c))
    tmap = lambda b, s: (s, 0)
    omap = lambda b, s: (b, 0, s, 0)
    oshape = jax.ShapeDtypeStruct((n, B_HEADS, t, B_HEAD_DIM), F32)
    ospec = pl.BlockSpec((1, B_HEADS, tile, B_HEAD_DIM), omap)
    return pl.pallas_call(
        _rope_kernel,
        grid=(n, tps),
        in_specs=[pl.BlockSpec((tile, B_WIDTH), zmap(0)),
                  pl.BlockSpec((tile, B_WIDTH), zmap(1)),
                  pl.BlockSpec((tile, B_WIDTH), zmap(2)),
                  pl.BlockSpec((tile, B_WIDTH), tmap),
                  pl.BlockSpec((tile, B_WIDTH), tmap),
                  pl.BlockSpec((tile, B_WIDTH), tmap)],
        out_specs=[ospec, ospec, ospec],
        out_shape=[oshape, oshape, oshape],
        compiler_params=_params("parallel", "parallel"),
        name="rope_split",
    )(zb, zb, zb, *tabs)


def _hgrn_kernel(*refs, chunk, has_s0):
    aq_ref, af_ref, ai_ref, ag_ref, lb_ref, gain_ref = refs[:6]
    rest = refs[6:]
    s0_ref = None
    if has_s0:
        s0_ref, rest = rest[0], rest[1:]
    o_ref, so_ref, st_ref, q_s, b_s, k_s = rest
    t = pl.program_id(1)
    tb = aq_ref.shape[0]

    @pl.when(t == 0)
    def _():
        for h in range(A_HEADS):
            if has_s0:
                st_ref[h] = s0_ref[0, h].T
            else:
                st_ref[h] = jnp.zeros((A_VDIM, A_KDIM), F32)

    lb = lb_ref[...]
    f = lb + (1.0 - lb) * jax.nn.sigmoid(af_ref[...])
    logf = jnp.log(f)
    q_s[...] = _silu(aq_ref[...])
    k_s[...] = 1.0 - f
    row = lax.broadcasted_iota(I32, logf.shape, 0) & (chunk - 1)
    b = logf
    sh = 1
    while sh < chunk:
        b = b + jnp.where(row >= sh, pltpu.roll(b, sh, axis=0), 0.0)
        sh *= 2
    b_s[...] = b
    rowc = lax.broadcasted_iota(I32, (chunk, A_KDIM), 0)

    def one_chunk(ci, carry):
        r0 = pl.multiple_of(ci * chunk, chunk)
        rs = pl.ds(r0, chunk)
        for h in range(A_HEADS):
            ls = slice(h * A_KDIM, (h + 1) * A_KDIM)
            qc, bc, kc, vc = q_s[rs, ls], b_s[rs, ls], k_s[rs, ls], ai_ref[rs, ls]
            st = st_ref[h]
            bl = bc[chunk - 1:chunk, :]
            o = _dot_nt((qc * jnp.exp(bc)).astype(BF16), st.astype(BF16))
            grp = 8
            parts = [o[g * grp:(g + 1) * grp] for g in range(chunk // grp)]
            for s in range(chunk):
                for g in range(s // grp, chunk // grp):
                    rows = slice(g * grp, (g + 1) * grp)
                    diff = bc[rows] - bc[s:s + 1, :]
                    if g == s // grp:
                        diff = jnp.where(rowc[rows] >= s, diff, -jnp.inf)
                    e = jnp.exp(diff)
                    r = jnp.sum(qc[rows] * e * kc[s:s + 1, :], axis=1, keepdims=True)
                    parts[g] = parts[g] + r * vc[s:s + 1, :]
            o = jnp.concatenate(parts, axis=0) if len(parts) > 1 else parts[0]
            kp = kc * jnp.exp(bl - bc)
            st_ref[h] = st * jnp.exp(bl) + _dot_tn(vc.astype(BF16), kp.astype(BF16))
            o_ref[rs, ls] = o
        return carry

    n_chunks = tb // chunk
    lax.fori_loop(0, n_chunks, one_chunk, 0, unroll=4 if n_chunks % 4 == 0 else 1)

    gain = gain_ref[...]
    for h in range(A_HEADS):
        ls = slice(h * A_VDIM, (h + 1) * A_VDIM)
        oh = o_ref[:, ls]
        y = oh * lax.rsqrt(jnp.mean(oh * oh, axis=-1, keepdims=True) + NORM_EPS) * gain
        o_ref[:, ls] = y * _silu(ag_ref[:, ls])

    @pl.when(t == pl.num_programs(1) - 1)
    def _():
        for h in range(A_HEADS):
            so_ref[0, h] = st_ref[h].T


def _hgrn(za, lb, gain, row0, n, t, tile, chunk, s0=None):
    tps = t // tile
    blk0 = row0 // tile
    zmap = lambda c: (lambda b, s: (blk0 + b * tps + s, c))
    const = lambda b, s: (0, 0)
    in_specs = [pl.BlockSpec((tile, A_WIDTH), zmap(c)) for c in range(4)]
    in_specs += [pl.BlockSpec((1, A_WIDTH), const), pl.BlockSpec((1, A_VDIM), const)]
    args = [za, za, za, za, lb.reshape(1, A_WIDTH), gain.reshape(1, A_VDIM)]
    if s0 is not None:
        in_specs.append(pl.BlockSpec((1, A_HEADS, A_KDIM, A_VDIM), lambda b, s: (b, 0, 0, 0)))
        args.append(s0)
    return pl.pallas_call(
        functools.partial(_hgrn_kernel, chunk=chunk, has_s0=s0 is not None),
        grid=(n, tps),
        in_specs=in_specs,
        out_specs=[pl.BlockSpec((tile, A_WIDTH), lambda b, s: (b * tps + s, 0)),
                   pl.BlockSpec((1, A_HEADS, A_KDIM, A_VDIM), lambda b, s: (b, 0, 0, 0))],
        out_shape=[jax.ShapeDtypeStruct((n * t, A_WIDTH), F32),
                   jax.ShapeDtypeStruct((n, A_HEADS, A_KDIM, A_VDIM), F32)],
        scratch_shapes=[pltpu.VMEM((A_HEADS, A_VDIM, A_KDIM), F32),
                        pltpu.VMEM((tile, A_WIDTH), F32),
                        pltpu.VMEM((tile, A_WIDTH), F32),
                        pltpu.VMEM((tile, A_WIDTH), F32)],
        compiler_params=_params("parallel", "arbitrary"),
        name="hgrn2",
    )(*args)


MOBA_CHUNK_BLOCKS = 4
MOBA_HEADS_PER_STEP = 2
MOBA_QUERY_BLOCKS = 4


def _moba_prompt_kernel(q_ref, k_ref, v_ref, o_ref, km_ref, ka_ref, vt_ref, eye_ref):
    j = pl.program_id(2)
    blk = MOBA_BLOCK
    hp, nq, dh = q_ref.shape[1], q_ref.shape[2], q_ref.shape[3]
    n_chunks, kc = ka_ref.shape[0], ka_ref.shape[1]
    n_blk = n_chunks * MOBA_CHUNK_BLOCKS
    eye = (lax.broadcasted_iota(I32, (dh, dh), 0) == lax.broadcasted_iota(I32, (dh, dh), 1)).astype(BF16)

    @pl.when(j == 0)
    def _():
        eye_ref[...] = (lax.broadcasted_iota(I32, (blk, blk), 0)
                        == lax.broadcasted_iota(I32, (blk, blk), 1)).astype(BF16)
        for c in range(n_chunks):
            rows = slice(c * kc, (c + 1) * kc)
            key_blk = lax.broadcasted_iota(I32, (kc, n_blk), 0) // blk + c * MOBA_CHUNK_BLOCKS
            onehot = key_blk == lax.broadcasted_iota(I32, (kc, n_blk), 1)
            ka_ref[c, :, hp * dh:] = jnp.where(onehot, 1.0, 0.0).astype(BF16)
            for hh in range(hp):
                ka_ref[c, :, hh * dh:(hh + 1) * dh] = k_ref[0, hh, rows, :].astype(BF16)
                vt_ref[hh, c] = _dot_nt(eye, v_ref[0, hh, rows, :].astype(BF16)).astype(BF16)
        for hh in range(hp):
            for i in range(n_blk):
                km_ref[hh, i:i + 1, :] = jnp.mean(k_ref[0, hh, i * blk:(i + 1) * blk, :], axis=0, keepdims=True)

    bidx = lax.broadcasted_iota(I32, (n_blk, nq), 0)
    own = j * (nq // blk) + lax.broadcasted_iota(I32, (n_blk, nq), 1) // blk
    valid = bidx < own
    q_rows, biases = [], []
    for hh in range(hp):
        q = q_ref[0, hh]
        sc = jnp.where(valid, _dot3(km_ref[hh], q, _dot_nt), -jnp.inf)
        rank = jnp.zeros((n_blk, nq), I32)
        for i in range(n_blk - 1):
            row = sc[i:i + 1, :]
            rank = rank + jnp.where(row > sc, 1, jnp.where(row == sc, jnp.where(bidx > i, 1, 0), 0))
        biases.append(jnp.where(valid, jnp.where(rank < MOBA_TOPK, 0.0, NEG_BIG),
                                jnp.where(bidx == own, 0.0, NEG_BIG)))
        q_t = _dot_nt(eye, (q * LOG2_E).astype(BF16))
        zero = jnp.zeros((dh, nq), F32)
        q_rows.append(jnp.concatenate([q_t if g == hh else zero for g in range(hp)], axis=1))
    qa = jnp.concatenate(q_rows + [jnp.concatenate(biases, axis=1)], axis=0).astype(BF16)

    def scores(c):
        return jnp.concatenate([_dot(ka_ref[c, :kc // 2], qa), _dot(ka_ref[c, kc // 2:], qa)], axis=0)

    def weighted_values(c, p):
        pb = p.astype(BF16)
        return jnp.concatenate([_dot(vt_ref[hh, c], pb[:, hh * nq:(hh + 1) * nq]) for hh in range(hp)], axis=1)

    cj = (j * nq) // kc
    key_pos = lax.broadcasted_iota(I32, (kc, hp * nq), 0) + cj * kc
    q_pos = (lax.broadcasted_iota(I32, (kc, hp * nq), 1) & (nq - 1)) + j * nq
    s = jnp.where(key_pos <= q_pos, scores(cj), NEG_BIG)
    m = jnp.max(s, axis=0, keepdims=True)
    p = jnp.exp2(s - m)
    init = (m, jnp.sum(p, axis=0, keepdims=True), weighted_values(cj, p))

    def past_chunk(c, carry):
        m, l, acc = carry
        s = scores(c)
        m_new = jnp.maximum(m, jnp.max(s, axis=0, keepdims=True))
        p = jnp.exp2(s - m_new)
        alpha = jnp.exp2(m - m_new)
        return m_new, alpha * l + jnp.sum(p, axis=0, keepdims=True), alpha * acc + weighted_values(c, p)

    _, l, acc = lax.fori_loop(0, cj, past_chunk, init)
    o_t = (acc / l).astype(BF16)
    o_ref[...] = jnp.concatenate(
        [jnp.concatenate([_dot_nt(eye_ref[...], o_t[:, hh * nq + u * blk:hh * nq + (u + 1) * blk])
                          for u in range(nq // blk)], axis=0) for hh in range(hp)], axis=1).astype(BF16)


def _moba_prompt(q, k, v):
    b, h, s, dh = q.shape
    hp = MOBA_HEADS_PER_STEP
    n_blk = s // MOBA_BLOCK
    nq = MOBA_QUERY_BLOCKS * MOBA_BLOCK
    assert n_blk % MOBA_CHUNK_BLOCKS == 0 and h % hp == 0 and hp * dh == LANES
    assert MOBA_CHUNK_BLOCKS % MOBA_QUERY_BLOCKS == 0
    n_chunks = n_blk // MOBA_CHUNK_BLOCKS
    n_tiles = s // nq
    kc = MOBA_CHUNK_BLOCKS * MOBA_BLOCK
    full = pl.BlockSpec((1, hp, s, dh), lambda bi, hi, j: (bi, hi, 0, 0))
    return pl.pallas_call(
        _moba_prompt_kernel,
        grid=(b, h // hp, n_tiles),
        in_specs=[pl.BlockSpec((1, hp, nq, dh), lambda bi, hi, j: (bi, hi, j, 0)), full, full],
        out_specs=pl.BlockSpec((nq, hp * dh), lambda bi, hi, j: (bi * n_tiles + j, hi)),
        out_shape=jax.ShapeDtypeStruct((b * s, h * dh), BF16),
        scratch_shapes=[pltpu.VMEM((hp, n_blk, dh), F32),
                        pltpu.VMEM((n_chunks, kc, hp * dh + n_blk), BF16),
                        pltpu.VMEM((hp, n_chunks, dh, kc), BF16),
                        pltpu.VMEM((MOBA_BLOCK, MOBA_BLOCK), BF16)],
        compiler_params=_params("parallel", "parallel", "arbitrary"),
        name="moba_prompt",
    )(q, k, v)


SELECT_PAGES_PER_STEP = 16
PAGE_RING = 4


def _select_kernel(pt_ref, q_ref, ck_ref, o_ref, buf, pm_ref, sem, *, layer, pages_per_seq):
    n_grp = pl.num_programs(1)
    g = pl.program_id(1)
    step = pl.program_id(0) * n_grp + g
    n_steps = pl.num_programs(0) * n_grp
    ring = buf.shape[0]
    slot = step % ring
    pps, n_heads, dh, ps = buf.shape[1], buf.shape[2], buf.shape[3], buf.shape[4]
    t = q_ref.shape[2]

    def page_copy(st, sl, i, pg):
        return pltpu.make_async_copy(ck_ref.at[layer, pg], buf.at[sl, i], sem.at[sl])

    def fetch(st):
        base = (st // n_grp) * pages_per_seq + (st % n_grp) * pps
        for i in range(pps):
            page_copy(st, st % ring, i, pt_ref[base + i]).start()

    for ahead in range(ring - 1):
        @pl.when((step == 0) & (ahead < n_steps))
        def _():
            fetch(ahead)

    @pl.when(step + ring - 1 < n_steps)
    def _():
        fetch(step + ring - 1)

    for i in range(pps):
        page_copy(step, slot, i, 0).wait()

    bps = pps // PAGES_PER_BLOCK
    diag = lax.broadcasted_iota(I32, (dh, ps), 0) == lax.broadcasted_iota(I32, (dh, ps), 1)
    blocks = []
    for b in range(bps):
        xb = buf[slot, b * PAGES_PER_BLOCK]
        for pi in range(1, PAGES_PER_BLOCK):
            xb = xb + buf[slot, b * PAGES_PER_BLOCK + pi]
        blocks.append(xb)
    x = jnp.stack(blocks, axis=0)
    r = jnp.sum(x, axis=-1, keepdims=True) * (1.0 / (ps * PAGES_PER_BLOCK))
    bm_step = jnp.sum(jnp.where(diag, r, 0.0), axis=2)
    pm_ref[pl.ds(pl.multiple_of(g * bps * n_heads, bps * n_heads), bps * n_heads), :] = (
        bm_step.reshape(bps * n_heads, ps))

    @pl.when(g == n_grp - 1)
    def _():
        n_blk = pages_per_seq // PAGES_PER_BLOCK
        lane = lax.broadcasted_iota(I32, (t, n_blk), 1).astype(F32)
        for h in range(n_heads):
            bm = pm_ref[pl.ds(h, n_blk, stride=n_heads), :]
            sc = _dot3(q_ref[0, h], bm[:, :dh], _dot_nt)
            res = jnp.zeros((t, n_blk), F32)
            for k in range(MOBA_TOPK):
                m = jnp.max(sc, axis=1, keepdims=True)
                idx = jnp.min(jnp.where(sc == m, lane, float(n_blk)), axis=1, keepdims=True)
                res = jnp.where(lane == k, idx, res)
                sc = jnp.where(lane == idx, -jnp.inf, sc)
            o_ref[0, h] = res.astype(I32)


def _select_blocks(q, cache_kt, page_table, layer):
    n, h, t, dh = q.shape
    ps = cache_kt.shape[-1]
    pages_per_seq = page_table.shape[1]
    pps = SELECT_PAGES_PER_STEP
    assert pages_per_seq % pps == 0 and pages_per_seq % PAGES_PER_BLOCK == 0 and dh <= ps
    n_blk = pages_per_seq // PAGES_PER_BLOCK
    return pl.pallas_call(
        functools.partial(_select_kernel, layer=layer, pages_per_seq=pages_per_seq),
        grid_spec=pltpu.PrefetchScalarGridSpec(
            num_scalar_prefetch=1,
            grid=(n, pages_per_seq // pps),
            in_specs=[pl.BlockSpec((1, h, t, dh), lambda i, g, pt: (i, 0, 0, 0)),
                      pl.BlockSpec(memory_space=pl.ANY)],
            out_specs=pl.BlockSpec((1, h, t, n_blk), lambda i, g, pt: (i, 0, 0, 0)),
            scratch_shapes=[pltpu.VMEM((PAGE_RING, pps, h, dh, ps), F32),
                            pltpu.VMEM((n_blk * h, ps), F32),
                            pltpu.SemaphoreType.DMA((PAGE_RING,))]),
        out_shape=jax.ShapeDtypeStruct((n, h, t, n_blk), I32),
        compiler_params=_params("arbitrary", "arbitrary"),
        name="moba_select",
    )(page_table.reshape(-1), q, cache_kt)


def _moba_sample_kernel(sel_ref, pt_ref, q_ref, kn_ref, vn_ref, ck_ref, cv_ref, o_ref, kbuf, vbuf, sem, *,
                        layer, pages_per_seq):
    n_heads = pl.num_programs(1)
    step = pl.program_id(0) * n_heads + pl.program_id(1)
    n_steps = pl.num_programs(0) * n_heads
    t = q_ref.shape[2]
    ring, n_pages = kbuf.shape[0], kbuf.shape[1]
    pages_per_q = n_pages // t
    slot = step % ring

    def page_copies(st, sl, i, pg):
        hh = st % n_heads
        return (pltpu.make_async_copy(ck_ref.at[layer, pg, hh], kbuf.at[sl, i], sem.at[0, sl]),
                pltpu.make_async_copy(cv_ref.at[layer, pg, hh], vbuf.at[sl, i], sem.at[1, sl]))

    def fetch(st):
        sl = st % ring
        pt_base = (st // n_heads) * pages_per_seq
        for b in range(n_pages // PAGES_PER_BLOCK):
            blk = sel_ref[st * (n_pages // PAGES_PER_BLOCK) + b]
            for pi in range(PAGES_PER_BLOCK):
                pg = pt_ref[pt_base + blk * PAGES_PER_BLOCK + pi]
                for thread, cp in enumerate(page_copies(st, sl, b * PAGES_PER_BLOCK + pi, pg)):
                    cp.start(priority=thread)

    for ahead in range(ring - 1):
        @pl.when((step == 0) & (ahead < n_steps))
        def _():
            fetch(ahead)

    @pl.when(step + ring - 1 < n_steps)
    def _():
        fetch(step + ring - 1)

    for i in range(n_pages):
        for cp in page_copies(step, slot, i, 0):
            cp.wait()

    q = q_ref[0, 0]
    qb = q.astype(BF16)
    def pages_of(buf_ref, tq):
        return jnp.concatenate([buf_ref[slot, tq * pages_per_q + i].astype(BF16) for i in range(pages_per_q)],
                               axis=1)

    s = jnp.concatenate([_dot(qb, pages_of(kbuf, tq)) for tq in range(t)], axis=1)
    per_q = pages_per_q * PAGE_SIZE
    col = lax.broadcasted_iota(I32, s.shape, 1)
    lo = lax.broadcasted_iota(I32, s.shape, 0) * per_q
    own = (col >= lo) & (col < lo + per_q)
    sn = _dot_nt(q, kn_ref[0, 0])
    causal = lax.broadcasted_iota(I32, (t, t), 1) <= lax.broadcasted_iota(I32, (t, t), 0)
    m = jnp.maximum(jnp.max(jnp.where(own, s, NEG_BIG), axis=1, keepdims=True),
                    jnp.max(jnp.where(causal, sn, NEG_BIG), axis=1, keepdims=True))
    p = jnp.where(own, jnp.exp(s - m), 0.0).astype(BF16)
    pn = jnp.where(causal, jnp.exp(sn - m), 0.0)
    l = jnp.sum(p.astype(F32), axis=1, keepdims=True) + jnp.sum(pn, axis=1, keepdims=True)
    o = _dot(pn, vn_ref[0, 0])
    for tq in range(t):
        o = o + _dot_nt(p[:, tq * per_q:(tq + 1) * per_q], pages_of(vbuf, tq))
    o_ref[0, 0] = o / l


def _moba_sample(q, k_new, v_new, cache_kt, cache_vt, sel, page_table, layer):
    n, h, t, dh = q.shape
    n_pages = t * MOBA_TOPK * PAGES_PER_BLOCK
    blk = pl.BlockSpec((1, 1, t, dh), lambda i, j, sl, pt: (i, j, 0, 0))
    anyspec = pl.BlockSpec(memory_space=pl.ANY)
    return pl.pallas_call(
        functools.partial(_moba_sample_kernel, layer=layer, pages_per_seq=page_table.shape[1]),
        grid_spec=pltpu.PrefetchScalarGridSpec(
            num_scalar_prefetch=2,
            grid=(n, h),
            in_specs=[blk, blk, blk, anyspec, anyspec],
            out_specs=blk,
            scratch_shapes=[pltpu.VMEM((PAGE_RING, n_pages, dh, PAGE_SIZE), F32),
                            pltpu.VMEM((PAGE_RING, n_pages, dh, PAGE_SIZE), F32),
                            pltpu.SemaphoreType.DMA((2, PAGE_RING))]),
        out_shape=jax.ShapeDtypeStruct((n, h, t, dh), F32),
        compiler_params=_params("arbitrary", "arbitrary"),
        name="moba_sample",
    )(sel.reshape(-1), page_table.reshape(-1), q, k_new, v_new, cache_kt, cache_vt)


def _merge_kernel(xp_ref, xs_ref, oap_ref, oas_ref, obp_ref, obs_ref, ga_ref, gb_ref, g1p_ref, g1s_ref,
                  shp_ref, shs_ref, scp_ref, scs_ref, nf_ref, wpa_ref, wpb_ref, wo_ref, wr_ref, br_ref,
                  x1_ref, h2_ref, eid_ref, gw_ref, *, n_first):
    pa = _dot(_pick_rows(n_first, oap_ref, oas_ref).astype(BF16), wpa_ref[...])
    pb = _dot(_pick_rows(n_first, obp_ref, obs_ref), wpb_ref[...])
    mix = jax.nn.sigmoid(ga_ref[...]) * pa + jax.nn.sigmoid(gb_ref[...]) * pb
    x1 = (_pick_rows(n_first, xp_ref, xs_ref)
          + _pick_rows(n_first, g1p_ref, g1s_ref) * _dot(mix.astype(BF16), wo_ref[...]))
    x1_ref[...] = x1
    h2 = x1 * lax.rsqrt(jnp.mean(x1 * x1, axis=-1, keepdims=True) + NORM_EPS) * nf_ref[...]
    h2 = h2 * (1.0 + _pick_rows(n_first, scp_ref, scs_ref)) + _pick_rows(n_first, shp_ref, shs_ref)
    hb = h2.astype(BF16).astype(F32)
    half = hb.shape[1] // 2
    words = pltpu.bitcast(hb[:, :half], jnp.uint32) | (pltpu.bitcast(hb[:, half:], jnp.uint32) >> 16)
    for s in range(h2_ref.shape[1]):
        h2_ref[:, s, :] = words[:, s * LANES:(s + 1) * LANES]

    logits = _dot3(h2, wr_ref[...]) + br_ref[...]
    lane = lax.broadcasted_iota(I32, logits.shape, 1)
    lanef = lane.astype(F32)
    is_g = lane < N_GROUPS
    gl = jnp.where(is_g, logits, -jnp.inf)
    gmax = jnp.max(gl, axis=1, keepdims=True)
    g_sel = jnp.min(jnp.where(gl == gmax, lanef, float(LANES)), axis=1, keepdims=True)
    p_group = 1.0 / jnp.sum(jnp.exp(gl - gmax), axis=1, keepdims=True)
    e_lo = N_GROUPS + g_sel * EXPERTS_PER_GROUP
    in_grp = (lanef >= e_lo) & (lanef < e_lo + EXPERTS_PER_GROUP)
    el = jnp.where(in_grp, logits, -jnp.inf)
    m1 = jnp.max(el, axis=1, keepdims=True)
    i1 = jnp.min(jnp.where(el == m1, lanef, float(LANES)), axis=1, keepdims=True)
    el2 = jnp.where(lanef == i1, -jnp.inf, el)
    m2 = jnp.max(el2, axis=1, keepdims=True)
    i2 = jnp.min(jnp.where(el2 == m2, lanef, float(LANES)), axis=1, keepdims=True)
    e2 = jnp.exp(m2 - m1)
    w1 = p_group / (1.0 + e2)
    w2 = p_group * e2 / (1.0 + e2)
    eid = jnp.where(lane == 0, i1 - N_GROUPS, jnp.where(lane == 1, i2 - N_GROUPS, 0.0))
    eid_ref[...] = eid.astype(I32)
    gw_ref[...] = jnp.where(lane == 0, w1, jnp.where(lane == 1, w2, 0.0))


def _merge(xp, xs, oa_p, oa_s, ob_p, ob_s, zg, mods, norm_ffn, wpa, wpb, wo, w_router, b_router,
           tiles_per_seq):
    d = xp.shape[1]
    n_first = xp.shape[0] // ROW_TILE
    m = xp.shape[0] + xs.shape[0]
    const = lambda i: (0, 0)
    row = lambda i: (i, 0)
    wspec = lambda w: pl.BlockSpec(w.shape, const, pipeline_mode=pl.Buffered(1))
    return pl.pallas_call(
        functools.partial(_merge_kernel, n_first=n_first),
        grid=(m // ROW_TILE,),
        in_specs=_two_group_specs(d, n_first) + _two_group_specs(A_WIDTH, n_first)
                 + _two_group_specs(B_WIDTH, n_first) + [
                  pl.BlockSpec((ROW_TILE, d), lambda i: (i, 0)),
                  pl.BlockSpec((ROW_TILE, d), lambda i: (i, 1))]
                 + _mod_specs(mods, 2, tiles_per_seq) + _mod_specs(mods, 3, tiles_per_seq)
                 + _mod_specs(mods, 4, tiles_per_seq) + [
                  pl.BlockSpec((1, d), const),
                  wspec(wpa), wspec(wpb), wspec(wo), wspec(w_router),
                  pl.BlockSpec((1, LANES), const)],
        out_specs=[pl.BlockSpec((ROW_TILE, d), row),
                   pl.BlockSpec((ROW_TILE, d // (2 * LANES), LANES), lambda i: (i, 0, 0)),
                   pl.BlockSpec((ROW_TILE, LANES), row),
                   pl.BlockSpec((ROW_TILE, LANES), row)],
        out_shape=[jax.ShapeDtypeStruct((m, d), F32),
                   jax.ShapeDtypeStruct((m, d // (2 * LANES), LANES), jnp.uint32),
                   jax.ShapeDtypeStruct((m, LANES), I32),
                   jax.ShapeDtypeStruct((m, LANES), F32)],
        compiler_params=_params("parallel"),
        name="merge_router",
    )(xp, xs, oa_p, oa_s, ob_p, ob_s, zg, zg, *mods, *mods, *mods,
      norm_ffn.reshape(1, d), wpa, wpb, wo, w_router, b_router)


MOE_VMEM_LIMIT_BYTES = 58 * 1024 * 1024


def _moe_kernel(be_ref, nu_ref, tok_ref, h_ref, wg_ref, wu_ref, wd_ref, y_ref, xbuf, wg_s, wu_s, wd_s):
    i = pl.program_id(0)
    n_used = nu_ref[0]
    n_sub = xbuf.shape[0] // MOE_TILE
    prev = be_ref[jnp.maximum(i - 1, 0)]
    fresh = (i == 0) | (be_ref[i] != prev)

    @pl.when(fresh)
    def _():
        wg_s[...] = wg_ref[...].astype(BF16)
        wu_s[...] = wu_ref[...].astype(BF16)
        wd_s[...] = wd_ref[...].astype(BF16)

    @pl.when(i < n_used)
    def _():
        for r in range(MOE_TILE):
            src = pl.multiple_of(tok_ref[i * MOE_TILE + r] * n_sub, n_sub)
            xbuf[r * n_sub:(r + 1) * n_sub, :] = h_ref[pl.ds(src, n_sub), :]
        words = [xbuf[pl.ds(s, MOE_TILE, stride=n_sub), :] for s in range(n_sub)]
        xb = jnp.concatenate(
            [pltpu.bitcast(w & jnp.uint32(0xFFFF0000), F32).astype(BF16) for w in words]
            + [pltpu.bitcast(w << 16, F32).astype(BF16) for w in words], axis=1)
        hid = _silu(_dot(xb, wg_s[...])) * _dot(xb, wu_s[...])
        y_ref[...] = _dot(hid.astype(BF16), wd_s[...])

    @pl.when(i >= n_used)
    def _():
        y_ref[...] = jnp.zeros(y_ref.shape, F32)


def _moe(h, row_tok, block_expert, n_used, w_gate, w_up, w_down, layer):
    n_tok, n_sub, lanes = h.shape
    d = 2 * n_sub * lanes
    n_rows = row_tok.shape[0]
    de = w_gate.shape[-1]
    return pl.pallas_call(
        _moe_kernel,
        grid_spec=pltpu.PrefetchScalarGridSpec(
            num_scalar_prefetch=3,
            grid=(n_rows // MOE_TILE,),
            in_specs=[pl.BlockSpec((n_tok * n_sub, lanes), lambda i, be, nu, tk: (0, 0),
                                   pipeline_mode=pl.Buffered(1)),
                      pl.BlockSpec((None, None, d, de), lambda i, be, nu, tk: (layer, be[i], 0, 0)),
                      pl.BlockSpec((None, None, d, de), lambda i, be, nu, tk: (layer, be[i], 0, 0)),
                      pl.BlockSpec((None, None, de, d), lambda i, be, nu, tk: (layer, be[i], 0, 0))],
            out_specs=pl.BlockSpec((MOE_TILE, d), lambda i, be, nu, tk: (i, 0)),
            scratch_shapes=[pltpu.VMEM((MOE_TILE * n_sub, lanes), jnp.uint32),
                            pltpu.VMEM((d, de), BF16), pltpu.VMEM((d, de), BF16), pltpu.VMEM((de, d), BF16)]),
        out_shape=jax.ShapeDtypeStruct((n_rows, d), F32),
        compiler_params=pltpu.CompilerParams(dimension_semantics=("arbitrary",),
                                             vmem_limit_bytes=MOE_VMEM_LIMIT_BYTES),
        name="moe_experts",
    )(block_expert, n_used, row_tok, h.reshape(n_tok * n_sub, lanes), w_gate, w_up, w_down)


def _final_kernel(x1_ref, y0_ref, y1_ref, gw_ref, g2p_ref, g2s_ref, nf_ref, op_ref, os_ref, *, last, n_first):
    gw = gw_ref[...]
    f = gw[:, 0:1] * y0_ref[...] + gw[:, 1:2] * y1_ref[...]
    x2 = x1_ref[...] + _pick_rows(n_first, g2p_ref, g2s_ref) * f
    if last:
        x2 = x2 * lax.rsqrt(jnp.mean(x2 * x2, axis=-1, keepdims=True) + NORM_EPS) * nf_ref[...]
    i = pl.program_id(0)

    @pl.when(i < n_first)
    def _():
        op_ref[...] = x2

    @pl.when(i >= n_first)
    def _():
        os_ref[...] = x2


def _final(x1, y0, y1, gw, mods, norm_final, tiles_per_seq, last, n_first):
    m, d = x1.shape
    row = lambda i: (i, 0)
    rs = pl.BlockSpec((ROW_TILE, d), row)
    return pl.pallas_call(
        functools.partial(_final_kernel, last=last, n_first=n_first),
        grid=(m // ROW_TILE,),
        in_specs=[rs, rs, rs, pl.BlockSpec((ROW_TILE, LANES), row)] + _mod_specs(mods, 5, tiles_per_seq) + [
                  pl.BlockSpec((1, d), lambda i: (0, 0))],
        out_specs=_two_group_specs(d, n_first),
        out_shape=[jax.ShapeDtypeStruct((n_first * ROW_TILE, d), F32),
                   jax.ShapeDtypeStruct((m - n_first * ROW_TILE, d), F32)],
        compiler_params=_params("arbitrary"),
        name="combine_final",
    )(x1, y0, y1, gw, *mods, norm_final.reshape(1, d))


def _dispatch(eid):
    m = eid.shape[0]
    n_assign = m * EXPERT_TOPK
    flat_e = eid.reshape(-1)
    flat_tok = jnp.repeat(jnp.arange(m, dtype=I32), EXPERT_TOPK)
    onehot = (flat_e[:, None] == jnp.arange(N_EXPERTS, dtype=I32)[None, :]).astype(I32)
    csum = jnp.cumsum(onehot, axis=0)
    rank = jnp.take_along_axis(csum, flat_e[:, None], axis=1)[:, 0] - 1
    counts = csum[-1]
    padded = (counts + MOE_TILE - 1) // MOE_TILE * MOE_TILE
    pad_end = jnp.cumsum(padded)
    pad_start = pad_end - padded
    dest = (pad_start[flat_e] + rank).astype(I32)
    n_blocks = -(-(n_assign + N_EXPERTS * (MOE_TILE - 1)) // MOE_TILE)
    row_tok = jnp.zeros((n_blocks * MOE_TILE,), I32).at[dest].set(flat_tok)
    blk_start = jnp.arange(n_blocks, dtype=I32) * MOE_TILE
    block_expert = jnp.minimum(jnp.sum((pad_end[None, :] <= blk_start[:, None]).astype(I32), axis=1),
                               N_EXPERTS - 1).astype(I32)
    n_used = (pad_end[-1:] // MOE_TILE).astype(I32)
    return dest.reshape(m, EXPERT_TOPK), row_tok, block_expert, n_used


def _mod_tiles(mod, n_prompt, reps):
    mod_p = mod[:n_prompt].transpose(1, 0, 2)[:, :, None, :]
    mod_s = jnp.repeat(mod[n_prompt:], reps, axis=0).transpose(1, 0, 2)
    return mod_p, mod_s


def kernel(x_prompt, x_sample, c_prompt, c_sample, cache_k, cache_v, state_hgrn, page_table,
           w_ada, b_ada, norm_mix, norm_ffn, w_in, hgrn_lb_logits, hgrn_norm,
           w_proj_a, w_proj_b, w_out, w_group, b_group, w_expert_router, b_expert_router,
           w_gate, w_up, w_down, norm_final):
    bp, tp, d = x_prompt.shape
    bs, ts, _ = x_sample.shape
    mp_rows, ms_rows = bp * tp, bs * ts
    assert ms_rows == ROW_TILE and tp % ROW_TILE == 0
    m = mp_rows + ms_rows
    tiles_per_seq = tp // ROW_TILE
    n_full = PAST_LEN // MOBA_BLOCK
    assert PAST_LEN % MOBA_BLOCK == 0 and n_full >= MOBA_TOPK

    xp, xs = x_prompt.reshape(mp_rows, d), x_sample.reshape(ms_rows, d)
    c_all = jnp.concatenate([c_prompt, c_sample, jnp.zeros((-(bp + bs) % 8, d), F32)], axis=0)
    lb_all = jnp.cumsum(jax.nn.softmax(hgrn_lb_logits.astype(F32), axis=0), axis=0)
    cache_kt = cache_k.transpose(0, 1, 2, 4, 3)
    cache_vt = cache_v.transpose(0, 1, 2, 4, 3)
    kp_l, vp_l, sp_l, ks_l, vs_l, ss_l = [], [], [], [], [], []

    for l in range(DEPTH):
        mod = _modulation(c_all, w_ada[l], b_ada[l])[:bp + bs].reshape(bp + bs, 6, d)
        mt = _mod_tiles(mod, bp, ts)
        za, zb, zg = _inproj(xp, xs, mt, norm_mix[l], w_in[l], tiles_per_seq)

        oa_p, st_p = _hgrn(za, lb_all[l], hgrn_norm[l], 0, bp, tp, ROW_TILE, HGRN_CHUNK)
        oa_s, st_s = _hgrn(za, lb_all[l], hgrn_norm[l], mp_rows, bs, ts, ts, ts, s0=state_hgrn[l])

        q_p, k_p, v_p = _rope_split(zb, 0, bp, tp, 512, jnp.arange(tp))
        q_s, k_s, v_s = _rope_split(zb, mp_rows, bs, ts, ts, PAST_LEN + jnp.arange(ts))
        ob_p = _moba_prompt(q_p, k_p, v_p)
        sel = _select_blocks(q_s, cache_kt, page_table[:, :n_full * PAGES_PER_BLOCK], l)[..., :MOBA_TOPK]
        ob_s = _moba_sample(q_s, k_s, v_s, cache_kt, cache_vt, sel, page_table, l)
        ob_s = ob_s.transpose(0, 2, 1, 3).reshape(ms_rows, B_WIDTH).astype(BF16)

        w_router = jnp.concatenate(
            [w_group[l], w_expert_router[l].transpose(1, 0, 2).reshape(d, N_EXPERTS),
             jnp.zeros((d, LANES - N_GROUPS - N_EXPERTS), F32)], axis=1)
        b_router = jnp.concatenate(
            [b_group[l], b_expert_router[l].reshape(-1),
             jnp.zeros((LANES - N_GROUPS - N_EXPERTS,), F32)]).reshape(1, LANES)
        x1, h2, eid, gw = _merge(xp, xs, oa_p, oa_s, ob_p, ob_s, zg, mt, norm_ffn[l], w_proj_a[l].astype(BF16),
                                 w_proj_b[l].astype(BF16), w_out[l].astype(BF16), w_router, b_router,
                                 tiles_per_seq)

        dest, row_tok, block_expert, n_used = _dispatch(eid[:, :EXPERT_TOPK])
        ys = _moe(h2, row_tok, block_expert, n_used, w_gate, w_up, w_down, l)
        xp, xs = _final(x1, ys[dest[:, 0]], ys[dest[:, 1]], gw, mt, norm_final, tiles_per_seq,
                        l == DEPTH - 1, mp_rows // ROW_TILE)

        kp_l.append(k_p); vp_l.append(v_p); sp_l.append(st_p)
        ks_l.append(k_s); vs_l.append(v_s); ss_l.append(st_s)

    y_prompt = xp.reshape(bp, tp, d)
    y_sample = xs.reshape(bs, ts, d)
    return (y_prompt, y_sample, jnp.stack(kp_l), jnp.stack(vp_l), jnp.stack(sp_l),
            jnp.stack(ks_l), jnp.stack(vs_l), jnp.stack(ss_l))
```

```python
import functools

import numpy as np
import jax
import jax.numpy as jnp
from jax import lax
from jax.experimental import pallas as pl
from jax.experimental.pallas import tpu as pltpu

F32 = jnp.float32
BF16 = jnp.bfloat16
I32 = jnp.int32

D_MODEL = 1024
DEPTH = 1
PAST_LEN = 16384
PAGE_SIZE = 128
A_HEADS = 4
A_KDIM = 128
A_VDIM = 128
A_WIDTH = A_HEADS * A_KDIM
B_HEADS = 8
B_HEAD_DIM = 64
B_WIDTH = B_HEADS * B_HEAD_DIM
MOBA_BLOCK = 256
MOBA_TOPK = 3
ROT_DIM = B_HEAD_DIM // 4
ROPE_THETA = 500000.0
N_GROUPS = 4
EXPERTS_PER_GROUP = 8
N_EXPERTS = N_GROUPS * EXPERTS_PER_GROUP
EXPERT_TOPK = 2
D_EXPERT = D_MODEL // 2
NORM_EPS = 1e-6

PAGES_PER_BLOCK = MOBA_BLOCK // PAGE_SIZE
ROW_TILE = 256
HGRN_CHUNK = 16
MOE_TILE = 256
LANES = 128
NEG_BIG = -1e30
LOG2_E = 1.4426950408889634
VMEM_LIMIT_BYTES = 52 * 1024 * 1024


def _params(*sem):
    return pltpu.CompilerParams(dimension_semantics=sem, vmem_limit_bytes=VMEM_LIMIT_BYTES)


def _dot(a, b):
    return jnp.dot(a, b, preferred_element_type=F32)


def _dot_nt(a, b):
    return lax.dot_general(a, b, (((1,), (1,)), ((), ())), preferred_element_type=F32)


def _dot_tn(a, b):
    return lax.dot_general(a, b, (((0,), (0,)), ((), ())), preferred_element_type=F32)


def _split(a):
    hi = a.astype(BF16)
    return hi, (a - hi.astype(F32)).astype(BF16)


def _dot3(a, b, dot=_dot):
    ah, al = _split(a)
    bh, bl = _split(b)
    return dot(ah, bh) + (dot(ah, bl) + dot(al, bh))


def _silu(x):
    return x * jax.nn.sigmoid(x)


def _mod_kernel(c_ref, w_ref, b_ref, o_ref):
    o_ref[...] = _dot3(_silu(c_ref[...]), w_ref[...]) + b_ref[...]


def _modulation(c_all, w, b):
    n = c_all.shape[0]
    d, dout = w.shape
    return pl.pallas_call(
        _mod_kernel,
        grid=(dout // d,),
        in_specs=[pl.BlockSpec((n, d), lambda j: (0, 0)),
                  pl.BlockSpec((d, d), lambda j: (0, j)),
                  pl.BlockSpec((1, d), lambda j: (0, j))],
        out_specs=pl.BlockSpec((n, d), lambda j: (0, j)),
        out_shape=jax.ShapeDtypeStruct((n, dout), F32),
        compiler_params=_params("parallel"),
        name="modulation",
    )(c_all, w, b.reshape(1, dout))


def _pick_rows(n_first, first_ref, second_ref):
    return jnp.where(pl.program_id(0) < n_first, first_ref[...], second_ref[...])


def _two_group_specs(width, n_first):
    return [pl.BlockSpec((ROW_TILE, width), lambda i: (jnp.minimum(i, n_first - 1), 0)),
            pl.BlockSpec((ROW_TILE, width), lambda i: (0, 0))]


def _mod_specs(mods, k, tiles_per_seq):
    mod_p, mod_s = mods
    n_seq, d = mod_p.shape[1], mod_p.shape[3]
    return [pl.BlockSpec((None, None, 1, d), lambda i: (k, jnp.minimum(i // tiles_per_seq, n_seq - 1), 0, 0)),
            pl.BlockSpec((None, ROW_TILE, d), lambda i: (k, 0, 0))]


def _inproj_kernel(xp_ref, xs_ref, shp_ref, shs_ref, scp_ref, scs_ref, g_ref, w_ref,
                   za_ref, zb_ref, zg_ref, *, n_first):
    x = _pick_rows(n_first, xp_ref, xs_ref)
    h = x * lax.rsqrt(jnp.mean(x * x, axis=-1, keepdims=True) + NORM_EPS) * g_ref[...]
    h = h * (1.0 + _pick_rows(n_first, scp_ref, scs_ref)) + _pick_rows(n_first, shp_ref, shs_ref)
    hh = h.astype(BF16)
    wa = 4 * A_WIDTH
    for c in range(0, wa, 512):
        za_ref[:, c:c + 512] = _dot(hh, w_ref[:, c:c + 512])
    for c in range(0, 3 * B_WIDTH, 512):
        zb_ref[:, c:c + 512] = _dot(hh, w_ref[:, wa + c:wa + c + 512])
    wg = wa + 3 * B_WIDTH
    for c in range(0, 2 * D_MODEL, 512):
        zg_ref[:, c:c + 512] = _dot(hh, w_ref[:, wg + c:wg + c + 512])


def _inproj(xp, xs, mods, norm_g, w_in, tiles_per_seq):
    d = xp.shape[1]
    n_first = xp.shape[0] // ROW_TILE
    m = xp.shape[0] + xs.shape[0]
    w_hi = w_in.astype(BF16)
    wa = 4 * A_WIDTH
    const = lambda i: (0, 0)
    row = lambda i: (i, 0)
    return pl.pallas_call(
        functools.partial(_inproj_kernel, n_first=n_first),
        grid=(m // ROW_TILE,),
        in_specs=_two_group_specs(d, n_first) + _mod_specs(mods, 0, tiles_per_seq)
                 + _mod_specs(mods, 1, tiles_per_seq) + [
                  pl.BlockSpec((1, d), const),
                  pl.BlockSpec(w_hi.shape, const, pipeline_mode=pl.Buffered(1))],
        out_specs=[pl.BlockSpec((ROW_TILE, wa), row),
                   pl.BlockSpec((ROW_TILE, 3 * B_WIDTH), row),
                   pl.BlockSpec((ROW_TILE, 2 * D_MODEL), row)],
        out_shape=[jax.ShapeDtypeStruct((m, wa), F32),
                   jax.ShapeDtypeStruct((m, 3 * B_WIDTH), F32),
                   jax.ShapeDtypeStruct((m, 2 * D_MODEL), F32)],
        compiler_params=_params("parallel"),
        name="inproj",
    )(xp, xs, *mods, *mods, norm_g.reshape(1, d), w_hi)


def _rope_kernel(q_ref, k_ref, v_ref, c_ref, s1_ref, s2_ref, qo_ref, ko_ref, vo_ref):
    cos, s1, s2 = c_ref[...], s1_ref[...], s2_ref[...]
    half = ROT_DIM // 2

    def rope(x):
        up = pltpu.roll(x, B_WIDTH - half, axis=1)
        dn = pltpu.roll(x, half, axis=1)
        return x * cos + up * s1 + dn * s2

    q = rope(q_ref[...]) * (B_HEAD_DIM ** -0.5)
    k = rope(k_ref[...])
    v = v_ref[...]
    for h in range(B_HEADS):
        ls = slice(h * B_HEAD_DIM, (h + 1) * B_HEAD_DIM)
        qo_ref[0, h] = q[:, ls]
        ko_ref[0, h] = k[:, ls]
        vo_ref[0, h] = v[:, ls]


def _rope_tables(pos):
    half = ROT_DIM // 2
    inv = jnp.power(ROPE_THETA, -(jnp.arange(half, dtype=F32) * 2.0 / ROT_DIM))
    ang = pos.astype(F32)[:, None] * inv[None, :]
    cos, sin = jnp.cos(ang), jnp.sin(ang)
    t = pos.shape[0]
    rest = B_HEAD_DIM - ROT_DIM
    c = jnp.concatenate([cos, cos, jnp.ones((t, rest), F32)], axis=-1)
    s1 = jnp.concatenate([-sin, jnp.zeros((t, half + rest), F32)], axis=-1)
    s2 = jnp.concatenate([jnp.zeros((t, half), F32), sin, jnp.zeros((t, rest), F32)], axis=-1)
    return [jnp.tile(a, (1, B_HEADS)) for a in (c, s1, s2)]


def _rope_split(zb, row0, n, t, tile, pos):
    tabs = _rope_tables(pos)
    tps = t // tile
    blk0 = row0 // tile
    zmap = lambda c: (lambda b, s: (blk0 + b * tps + s, c))
    tmap = lambda b, s: (s, 0)
    omap = lambda b, s: (b, 0, s, 0)
    oshape = jax.ShapeDtypeStruct((n, B_HEADS, t, B_HEAD_DIM), F32)
    ospec = pl.BlockSpec((1, B_HEADS, tile, B_HEAD_DIM), omap)
    return pl.pallas_call(
        _rope_kernel,
        grid=(n, tps),
        in_specs=[pl.BlockSpec((tile, B_WIDTH), zmap(0)),
                  pl.BlockSpec((tile, B_WIDTH), zmap(1)),
                  pl.BlockSpec((tile, B_WIDTH), zmap(2)),
                  pl.BlockSpec((tile, B_WIDTH), tmap),
                  pl.BlockSpec((tile, B_WIDTH), tmap),
                  pl.BlockSpec((tile, B_WIDTH), tmap)],
        out_specs=[ospec, ospec, ospec],
        out_shape=[oshape, oshape, oshape],
        compiler_params=_params("parallel", "parallel"),
        name="rope_split",
    )(zb, zb, zb, *tabs)


def _hgrn_kernel(*refs, chunk, has_s0):
    aq_ref, af_ref, ai_ref, ag_ref, lb_ref, gain_ref = refs[:6]
    rest = refs[6:]
    s0_ref = None
    if has_s0:
        s0_ref, rest = rest[0], rest[1:]
    o_ref, so_ref, st_ref, q_s, b_s, k_s = rest
    t = pl.program_id(1)
    tb = aq_ref.shape[0]

    @pl.when(t == 0)
    def _():
        for h in range(A_HEADS):
            if has_s0:
                st_ref[h] = s0_ref[0, h].T
            else:
                st_ref[h] = jnp.zeros((A_VDIM, A_KDIM), F32)

    lb = lb_ref[...]
    f = lb + (1.0 - lb) * jax.nn.sigmoid(af_ref[...])
    logf = jnp.log(f)
    q_s[...] = _silu(aq_ref[...])
    k_s[...] = 1.0 - f
    row = lax.broadcasted_iota(I32, logf.shape, 0) & (chunk - 1)
    b = logf
    sh = 1
    while sh < chunk:
        b = b + jnp.where(row >= sh, pltpu.roll(b, sh, axis=0), 0.0)
        sh *= 2
    b_s[...] = b
    rowc = lax.broadcasted_iota(I32, (chunk, A_KDIM), 0)

    def one_chunk(ci, carry):
        r0 = pl.multiple_of(ci * chunk, chunk)
        rs = pl.ds(r0, chunk)
        for h in range(A_HEADS):
            ls = slice(h * A_KDIM, (h + 1) * A_KDIM)
            qc, bc, kc, vc = q_s[rs, ls], b_s[rs, ls], k_s[rs, ls], ai_ref[rs, ls]
            st = st_ref[h]
            bl = bc[chunk - 1:chunk, :]
            o = _dot_nt((qc * jnp.exp(bc)).astype(BF16), st.astype(BF16))
            grp = 8
            parts = [o[g * grp:(g + 1) * grp] for g in range(chunk // grp)]
            for s in range(chunk):
                for g in range(s // grp, chunk // grp):
                    rows = slice(g * grp, (g + 1) * grp)
                    diff = bc[rows] - bc[s:s + 1, :]
                    if g == s // grp:
                        diff = jnp.where(rowc[rows] >= s, diff, -jnp.inf)
                    e = jnp.exp(diff)
                    r = jnp.sum(qc[rows] * e * kc[s:s + 1, :], axis=1, keepdims=True)
                    parts[g] = parts[g] + r * vc[s:s + 1, :]
            o = jnp.concatenate(parts, axis=0) if len(parts) > 1 else parts[0]
            kp = kc * jnp.exp(bl - bc)
            st_ref[h] = st * jnp.exp(bl) + _dot_tn(vc.astype(BF16), kp.astype(BF16))
            o_ref[rs, ls] = o
        return carry

    n_chunks = tb // chunk
    lax.fori_loop(0, n_chunks, one_chunk, 0, unroll=4 if n_chunks % 4 == 0 else 1)

    gain = gain_ref[...]
    for h in range(A_HEADS):
        ls = slice(h * A_VDIM, (h + 1) * A_VDIM)
        oh = o_ref[:, ls]
        y = oh * lax.rsqrt(jnp.mean(oh * oh, axis=-1, keepdims=True) + NORM_EPS) * gain
        o_ref[:, ls] = y * _silu(ag_ref[:, ls])

    @pl.when(t == pl.num_programs(1) - 1)
    def _():
        for h in range(A_HEADS):
            so_ref[0, h] = st_ref[h].T


def _hgrn(za, lb, gain, row0, n, t, tile, chunk, s0=None):
    tps = t // tile
    blk0 = row0 // tile
    zmap = lambda c: (lambda b, s: (blk0 + b * tps + s, c))
    const = lambda b, s: (0, 0)
    in_specs = [pl.BlockSpec((tile, A_WIDTH), zmap(c)) for c in range(4)]
    in_specs += [pl.BlockSpec((1, A_WIDTH), const), pl.BlockSpec((1, A_VDIM), const)]
    args = [za, za, za, za, lb.reshape(1, A_WIDTH), gain.reshape(1, A_VDIM)]
    if s0 is not None:
        in_specs.append(pl.BlockSpec((1, A_HEADS, A_KDIM, A_VDIM), lambda b, s: (b, 0, 0, 0)))
        args.append(s0)
    return pl.pallas_call(
        functools.partial(_hgrn_kernel, chunk=chunk, has_s0=s0 is not None),
        grid=(n, tps),
        in_specs=in_specs,
        out_specs=[pl.BlockSpec((tile, A_WIDTH), lambda b, s: (b * tps + s, 0)),
                   pl.BlockSpec((1, A_HEADS, A_KDIM, A_VDIM), lambda b, s: (b, 0, 0, 0))],
        out_shape=[jax.ShapeDtypeStruct((n * t, A_WIDTH), F32),
                   jax.ShapeDtypeStruct((n, A_HEADS, A_KDIM, A_VDIM), F32)],
        scratch_shapes=[pltpu.VMEM((A_HEADS, A_VDIM, A_KDIM), F32),
                        pltpu.VMEM((tile, A_WIDTH), F32),
                        pltpu.VMEM((tile, A_WIDTH), F32),
                        pltpu.VMEM((tile, A_WIDTH), F32)],
        compiler_params=_params("parallel", "arbitrary"),
        name="hgrn2",
    )(*args)


MOBA_CHUNK_BLOCKS = 4
MOBA_HEADS_PER_STEP = 2
MOBA_QUERY_BLOCKS = 4


def _moba_prompt_kernel(q_ref, k_ref, v_ref, o_ref, km_ref, ka_ref, vt_ref, eye_ref):
    j = pl.program_id(2)
    blk = MOBA_BLOCK
    hp, nq, dh = q_ref.shape[1], q_ref.shape[2], q_ref.shape[3]
    n_chunks, kc = ka_ref.shape[0], ka_ref.shape[1]
    n_blk = n_chunks * MOBA_CHUNK_BLOCKS
    eye = (lax.broadcasted_iota(I32, (dh, dh), 0) == lax.broadcasted_iota(I32, (dh, dh), 1)).astype(BF16)

    @pl.when(j == 0)
    def _():
        eye_ref[...] = (lax.broadcasted_iota(I32, (blk, blk), 0)
                        == lax.broadcasted_iota(I32, (blk, blk), 1)).astype(BF16)
        for c in range(n_chunks):
            rows = slice(c * kc, (c + 1) * kc)
            key_blk = lax.broadcasted_iota(I32, (kc, n_blk), 0) // blk + c * MOBA_CHUNK_BLOCKS
            onehot = key_blk == lax.broadcasted_iota(I32, (kc, n_blk), 1)
            ka_ref[c, :, hp * dh:] = jnp.where(onehot, 1.0, 0.0).astype(BF16)
            for hh in range(hp):
                ka_ref[c, :, hh * dh:(hh + 1) * dh] = k_ref[0, hh, rows, :].astype(BF16)
                vt_ref[hh, c] = _dot_nt(eye, v_ref[0, hh, rows, :].astype(BF16)).astype(BF16)
        for hh in range(hp):
            for i in range(n_blk):
                km_ref[hh, i:i + 1, :] = jnp.mean(k_ref[0, hh, i * blk:(i + 1) * blk, :], axis=0, keepdims=True)

    bidx = lax.broadcasted_iota(I32, (n_blk, nq), 0)
    own = j * (nq // blk) + lax.broadcasted_iota(I32, (n_blk, nq), 1) // blk
    valid = bidx < own
    q_rows, biases = [], []
    for hh in range(hp):
        q = q_ref[0, hh]
        sc = jnp.where(valid, _dot3(km_ref[hh], q, _dot_nt), -jnp.inf)
        rank = jnp.zeros((n_blk, nq), I32)
        for i in range(n_blk - 1):
            row = sc[i:i + 1, :]
            rank = rank + jnp.where(row > sc, 1, jnp.where(row == sc, jnp.where(bidx > i, 1, 0), 0))
        biases.append(jnp.where(valid, jnp.where(rank < MOBA_TOPK, 0.0, NEG_BIG),
                                jnp.where(bidx == own, 0.0, NEG_BIG)))
        q_t = _dot_nt(eye, (q * LOG2_E).astype(BF16))
        zero = jnp.zeros((dh, nq), F32)
        q_rows.append(jnp.concatenate([q_t if g == hh else zero for g in range(hp)], axis=1))
    qa = jnp.concatenate(q_rows + [jnp.concatenate(biases, axis=1)], axis=0).astype(BF16)

    def scores(c):
        return jnp.concatenate([_dot(ka_ref[c, :kc // 2], qa), _dot(ka_ref[c, kc // 2:], qa)], axis=0)

    def weighted_values(c, p):
        pb = p.astype(BF16)
        return jnp.concatenate([_dot(vt_ref[hh, c], pb[:, hh * nq:(hh + 1) * nq]) for hh in range(hp)], axis=1)

    cj = (j * nq) // kc
    key_pos = lax.broadcasted_iota(I32, (kc, hp * nq), 0) + cj * kc
    q_pos = (lax.broadcasted_iota(I32, (kc, hp * nq), 1) & (nq - 1)) + j * nq
    s = jnp.where(key_pos <= q_pos, scores(cj), NEG_BIG)
    m = jnp.max(s, axis=0, keepdims=True)
    p = jnp.exp2(s - m)
    init = (m, jnp.sum(p, axis=0, keepdims=True), weighted_values(cj, p))

    def past_chunk(c, carry):
        m, l, acc = carry
        s = scores(c)
        m_new = jnp.maximum(m, jnp.max(s, axis=0, keepdims=True))
        p = jnp.exp2(s - m_new)
        alpha = jnp.exp2(m - m_new)
        return m_new, alpha * l + jnp.sum(p, axis=0, keepdims=True), alpha * acc + weighted_values(c, p)

    _, l, acc = lax.fori_loop(0, cj, past_chunk, init)
    o_t = (acc / l).astype(BF16)
    o_ref[...] = jnp.concatenate(
        [jnp.concatenate([_dot_nt(eye_ref[...], o_t[:, hh * nq + u * blk:hh * nq + (u + 1) * blk])
                          for u in range(nq // blk)], axis=0) for hh in range(hp)], axis=1).astype(BF16)


def _moba_prompt(q, k, v):
    b, h, s, dh = q.shape
    hp = MOBA_HEADS_PER_STEP
    n_blk = s // MOBA_BLOCK
    nq = MOBA_QUERY_BLOCKS * MOBA_BLOCK
    assert n_blk % MOBA_CHUNK_BLOCKS == 0 and h % hp == 0 and hp * dh == LANES
    assert MOBA_CHUNK_BLOCKS % MOBA_QUERY_BLOCKS == 0
    n_chunks = n_blk // MOBA_CHUNK_BLOCKS
    n_tiles = s // nq
    kc = MOBA_CHUNK_BLOCKS * MOBA_BLOCK
    full = pl.BlockSpec((1, hp, s, dh), lambda bi, hi, j: (bi, hi, 0, 0))
    return pl.pallas_call(
        _moba_prompt_kernel,
        grid=(b, h // hp, n_tiles),
        in_specs=[pl.BlockSpec((1, hp, nq, dh), lambda bi, hi, j: (bi, hi, j, 0)), full, full],
        out_specs=pl.BlockSpec((nq, hp * dh), lambda bi, hi, j: (bi * n_tiles + j, hi)),
        out_shape=jax.ShapeDtypeStruct((b * s, h * dh), BF16),
        scratch_shapes=[pltpu.VMEM((hp, n_blk, dh), F32),
                        pltpu.VMEM((n_chunks, kc, hp * dh + n_blk), BF16),
                        pltpu.VMEM((hp, n_chunks, dh, kc), BF16),
                        pltpu.VMEM((MOBA_BLOCK, MOBA_BLOCK), BF16)],
        compiler_params=_params("parallel", "parallel", "arbitrary"),
        name="moba_prompt",
    )(q, k, v)


SELECT_PAGES_PER_STEP = 16
PAGE_RING = 4


def _select_kernel(pt_ref, q_ref, ck_ref, o_ref, buf, pm_ref, sem, *, layer, pages_per_seq):
    n_grp = pl.num_programs(1)
    g = pl.program_id(1)
    step = pl.program_id(0) * n_grp + g
    n_steps = pl.num_programs(0) * n_grp
    ring = buf.shape[0]
    slot = step % ring
    pps, n_heads, dh, ps = buf.shape[1], buf.shape[2], buf.shape[3], buf.shape[4]
    t = q_ref.shape[2]

    def page_copy(st, sl, i, pg):
        return pltpu.make_async_copy(ck_ref.at[layer, pg], buf.at[sl, i], sem.at[sl])

    def fetch(st):
        base = (st // n_grp) * pages_per_seq + (st % n_grp) * pps
        for i in range(pps):
            page_copy(st, st % ring, i, pt_ref[base + i]).start()

    for ahead in range(ring - 1):
        @pl.when((step == 0) & (ahead < n_steps))
        def _():
            fetch(ahead)

    @pl.when(step + ring - 1 < n_steps)
    def _():
        fetch(step + ring - 1)

    for i in range(pps):
        page_copy(step, slot, i, 0).wait()

    bps = pps // PAGES_PER_BLOCK
    diag = lax.broadcasted_iota(I32, (dh, ps), 0) == lax.broadcasted_iota(I32, (dh, ps), 1)
    blocks = []
    for b in range(bps):
        xb = buf[slot, b * PAGES_PER_BLOCK]
        for pi in range(1, PAGES_PER_BLOCK):
            xb = xb + buf[slot, b * PAGES_PER_BLOCK + pi]
        blocks.append(xb)
    x = jnp.stack(blocks, axis=0)
    r = jnp.sum(x, axis=-1, keepdims=True) * (1.0 / (ps * PAGES_PER_BLOCK))
    bm_step = jnp.sum(jnp.where(diag, r, 0.0), axis=2)
    pm_ref[pl.ds(pl.multiple_of(g * bps * n_heads, bps * n_heads), bps * n_heads), :] = (
        bm_step.reshape(bps * n_heads, ps))

    @pl.when(g == n_grp - 1)
    def _():
        n_blk = pages_per_seq // PAGES_PER_BLOCK
        lane = lax.broadcasted_iota(I32, (t, n_blk), 1).astype(F32)
        for h in range(n_heads):
            bm = pm_ref[pl.ds(h, n_blk, stride=n_heads), :]
            sc = _dot3(q_ref[0, h], bm[:, :dh], _dot_nt)
            res = jnp.zeros((t, n_blk), F32)
            for k in range(MOBA_TOPK):
                m = jnp.max(sc, axis=1, keepdims=True)
                idx = jnp.min(jnp.where(sc == m, lane, float(n_blk)), axis=1, keepdims=True)
                res = jnp.where(lane == k, idx, res)
                sc = jnp.where(lane == idx, -jnp.inf, sc)
            o_ref[0, h] = res.astype(I32)


def _select_blocks(q, cache_kt, page_table, layer):
    n, h, t, dh = q.shape
    ps = cache_kt.shape[-1]
    pages_per_seq = page_table.shape[1]
    pps = SELECT_PAGES_PER_STEP
    assert pages_per_seq % pps == 0 and pages_per_seq % PAGES_PER_BLOCK == 0 and dh <= ps
    n_blk = pages_per_seq // PAGES_PER_BLOCK
    return pl.pallas_call(
        functools.partial(_select_kernel, layer=layer, pages_per_seq=pages_per_seq),
        grid_spec=pltpu.PrefetchScalarGridSpec(
            num_scalar_prefetch=1,
            grid=(n, pages_per_seq // pps),
            in_specs=[pl.BlockSpec((1, h, t, dh), lambda i, g, pt: (i, 0, 0, 0)),
                      pl.BlockSpec(memory_space=pl.ANY)],
            out_specs=pl.BlockSpec((1, h, t, n_blk), lambda i, g, pt: (i, 0, 0, 0)),
            scratch_shapes=[pltpu.VMEM((PAGE_RING, pps, h, dh, ps), F32),
                            pltpu.VMEM((n_blk * h, ps), F32),
                            pltpu.SemaphoreType.DMA((PAGE_RING,))]),
        out_shape=jax.ShapeDtypeStruct((n, h, t, n_blk), I32),
        compiler_params=_params("arbitrary", "arbitrary"),
        name="moba_select",
    )(page_table.reshape(-1), q, cache_kt)


def _moba_sample_kernel(sel_ref, pt_ref, q_ref, kn_ref, vn_ref, ck_ref, cv_ref, o_ref, kbuf, vbuf, sem, *,
                        layer, pages_per_seq):
    n_heads = pl.num_programs(1)
    step = pl.program_id(0) * n_heads + pl.program_id(1)
    n_steps = pl.num_programs(0) * n_heads
    t = q_ref.shape[2]
    ring, n_pages = kbuf.shape[0], kbuf.shape[1]
    pages_per_q = n_pages // t
    slot = step % ring

    def page_copies(st, sl, i, pg):
        hh = st % n_heads
        return (pltpu.make_async_copy(ck_ref.at[layer, pg, hh], kbuf.at[sl, i], sem.at[0, sl]),
                pltpu.make_async_copy(cv_ref.at[layer, pg, hh], vbuf.at[sl, i], sem.at[1, sl]))

    def fetch(st):
        sl = st % ring
        pt_base = (st // n_heads) * pages_per_seq
        for b in range(n_pages // PAGES_PER_BLOCK):
            blk = sel_ref[st * (n_pages // PAGES_PER_BLOCK) + b]
            for pi in range(PAGES_PER_BLOCK):
                pg = pt_ref[pt_base + blk * PAGES_PER_BLOCK + pi]
                for thread, cp in enumerate(page_copies(st, sl, b * PAGES_PER_BLOCK + pi, pg)):
                    cp.start(priority=thread)

    for ahead in range(ring - 1):
        @pl.when((step == 0) & (ahead < n_steps))
        def _():
            fetch(ahead)

    @pl.when(step + ring - 1 < n_steps)
    def _():
        fetch(step + ring - 1)

    for i in range(n_pages):
        for cp in page_copies(step, slot, i, 0):
            cp.wait()

    q = q_ref[0, 0]
    qb = q.astype(BF16)
    def pages_of(buf_ref, tq):
        return jnp.concatenate([buf_ref[slot, tq * pages_per_q + i].astype(BF16) for i in range(pages_per_q)],
                               axis=1)

    s = jnp.concatenate([_dot(qb, pages_of(kbuf, tq)) for tq in range(t)], axis=1)
    per_q = pages_per_q * PAGE_SIZE
    col = lax.broadcasted_iota(I32, s.shape, 1)
    lo = lax.broadcasted_iota(I32, s.shape, 0) * per_q
    own = (col >= lo) & (col < lo + per_q)
    sn = _dot_nt(q, kn_ref[0, 0])
    causal = lax.broadcasted_iota(I32, (t, t), 1) <= lax.broadcasted_iota(I32, (t, t), 0)
    m = jnp.maximum(jnp.max(jnp.where(own, s, NEG_BIG), axis=1, keepdims=True),
                    jnp.max(jnp.where(causal, sn, NEG_BIG), axis=1, keepdims=True))
    p = jnp.where(own, jnp.exp(s - m), 0.0).astype(BF16)
    pn = jnp.where(causal, jnp.exp(sn - m), 0.0)
    l = jnp.sum(p.astype(F32), axis=1, keepdims=True) + jnp.sum(pn, axis=1, keepdims=True)
    o = _dot(pn, vn_ref[0, 0])
    for tq in range(t):
        o = o + _dot_nt(p[:, tq * per_q:(tq + 1) * per_q], pages_of(vbuf, tq))
    o_ref[0, 0] = o / l


def _moba_sample(q, k_new, v_new, cache_kt, cache_vt, sel, page_table, layer):
    n, h, t, dh = q.shape
    n_pages = t * MOBA_TOPK * PAGES_PER_BLOCK
    blk = pl.BlockSpec((1, 1, t, dh), lambda i, j, sl, pt: (i, j, 0, 0))
    anyspec = pl.BlockSpec(memory_space=pl.ANY)
    return pl.pallas_call(
        functools.partial(_moba_sample_kernel, layer=layer, pages_per_seq=page_table.shape[1]),
        grid_spec=pltpu.PrefetchScalarGridSpec(
            num_scalar_prefetch=2,
            grid=(n, h),
            in_specs=[blk, blk, blk, anyspec, anyspec],
            out_specs=blk,
            scratch_shapes=[pltpu.VMEM((PAGE_RING, n_pages, dh, PAGE_SIZE), F32),
                            pltpu.VMEM((PAGE_RING, n_pages, dh, PAGE_SIZE), F32),
                            pltpu.SemaphoreType.DMA((2, PAGE_RING))]),
        out_shape=jax.ShapeDtypeStruct((n, h, t, dh), F32),
        compiler_params=_params("arbitrary", "arbitrary"),
        name="moba_sample",
    )(sel.reshape(-1), page_table.reshape(-1), q, k_new, v_new, cache_kt, cache_vt)


def _merge_kernel(xp_ref, xs_ref, oap_ref, oas_ref, obp_ref, obs_ref, ga_ref, gb_ref, g1p_ref, g1s_ref,
                  shp_ref, shs_ref, scp_ref, scs_ref, nf_ref, wpa_ref, wpb_ref, wo_ref, wr_ref, br_ref,
                  x1_ref, h2_ref, eid_ref, gw_ref, *, n_first):
    pa = _dot(_pick_rows(n_first, oap_ref, oas_ref).astype(BF16), wpa_ref[...])
    pb = _dot(_pick_rows(n_first, obp_ref, obs_ref), wpb_ref[...])
    mix = jax.nn.sigmoid(ga_ref[...]) * pa + jax.nn.sigmoid(gb_ref[...]) * pb
    x1 = (_pick_rows(n_first, xp_ref, xs_ref)
          + _pick_rows(n_first, g1p_ref, g1s_ref) * _dot(mix.astype(BF16), wo_ref[...]))
    x1_ref[...] = x1
    h2 = x1 * lax.rsqrt(jnp.mean(x1 * x1, axis=-1, keepdims=True) + NORM_EPS) * nf_ref[...]
    h2 = h2 * (1.0 + _pick_rows(n_first, scp_ref, scs_ref)) + _pick_rows(n_first, shp_ref, shs_ref)
    hb = h2.astype(BF16).astype(F32)
    half = hb.shape[1] // 2
    words = pltpu.bitcast(hb[:, :half], jnp.uint32) | (pltpu.bitcast(hb[:, half:], jnp.uint32) >> 16)
    for s in range(h2_ref.shape[1]):
        h2_ref[:, s, :] = words[:, s * LANES:(s + 1) * LANES]

    logits = _dot3(h2, wr_ref[...]) + br_ref[...]
    lane = lax.broadcasted_iota(I32, logits.shape, 1)
    lanef = lane.astype(F32)
    is_g = lane < N_GROUPS
    gl = jnp.where(is_g, logits, -jnp.inf)
    gmax = jnp.max(gl, axis=1, keepdims=True)
    g_sel = jnp.min(jnp.where(gl == gmax, lanef, float(LANES)), axis=1, keepdims=True)
    p_group = 1.0 / jnp.sum(jnp.exp(gl - gmax), axis=1, keepdims=True)
    e_lo = N_GROUPS + g_sel * EXPERTS_PER_GROUP
    in_grp = (lanef >= e_lo) & (lanef < e_lo + EXPERTS_PER_GROUP)
    el = jnp.where(in_grp, logits, -jnp.inf)
    m1 = jnp.max(el, axis=1, keepdims=True)
    i1 = jnp.min(jnp.where(el == m1, lanef, float(LANES)), axis=1, keepdims=True)
    el2 = jnp.where(lanef == i1, -jnp.inf, el)
    m2 = jnp.max(el2, axis=1, keepdims=True)
    i2 = jnp.min(jnp.where(el2 == m2, lanef, float(LANES)), axis=1, keepdims=True)
    e2 = jnp.exp(m2 - m1)
    w1 = p_group / (1.0 + e2)
    w2 = p_group * e2 / (1.0 + e2)
    eid = jnp.where(lane == 0, i1 - N_GROUPS, jnp.where(lane == 1, i2 - N_GROUPS, 0.0))
    eid_ref[...] = eid.astype(I32)
    gw_ref[...] = jnp.where(lane == 0, w1, jnp.where(lane == 1, w2, 0.0))


def _merge(xp, xs, oa_p, oa_s, ob_p, ob_s, zg, mods, norm_ffn, wpa, wpb, wo, w_router, b_router,
           tiles_per_seq):
    d = xp.shape[1]
    n_first = xp.shape[0] // ROW_TILE
    m = xp.shape[0] + xs.shape[0]
    const = lambda i: (0, 0)
    row = lambda i: (i, 0)
    wspec = lambda w: pl.BlockSpec(w.shape, const, pipeline_mode=pl.Buffered(1))
    return pl.pallas_call(
        functools.partial(_merge_kernel, n_first=n_first),
        grid=(m // ROW_TILE,),
        in_specs=_two_group_specs(d, n_first) + _two_group_specs(A_WIDTH, n_first)
                 + _two_group_specs(B_WIDTH, n_first) + [
                  pl.BlockSpec((ROW_TILE, d), lambda i: (i, 0)),
                  pl.BlockSpec((ROW_TILE, d), lambda i: (i, 1))]
                 + _mod_specs(mods, 2, tiles_per_seq) + _mod_specs(mods, 3, tiles_per_seq)
                 + _mod_specs(mods, 4, tiles_per_seq) + [
                  pl.BlockSpec((1, d), const),
                  wspec(wpa), wspec(wpb), wspec(wo), wspec(w_router),
                  pl.BlockSpec((1, LANES), const)],
        out_specs=[pl.BlockSpec((ROW_TILE, d), row),
                   pl.BlockSpec((ROW_TILE, d // (2 * LANES), LANES), lambda i: (i, 0, 0)),
                   pl.BlockSpec((ROW_TILE, LANES), row),
                   pl.BlockSpec((ROW_TILE, LANES), row)],
        out_shape=[jax.ShapeDtypeStruct((m, d), F32),
                   jax.ShapeDtypeStruct((m, d // (2 * LANES), LANES), jnp.uint32),
                   jax.ShapeDtypeStruct((m, LANES), I32),
                   jax.ShapeDtypeStruct((m, LANES), F32)],
        compiler_params=_params("parallel"),
        name="merge_router",
    )(xp, xs, oa_p, oa_s, ob_p, ob_s, zg, zg, *mods, *mods, *mods,
      norm_ffn.reshape(1, d), wpa, wpb, wo, w_router, b_router)


MOE_VMEM_LIMIT_BYTES = 58 * 1024 * 1024


def _moe_kernel(be_ref, nu_ref, tok_ref, h_ref, wg_ref, wu_ref, wd_ref, y_ref, xbuf, wg_s, wu_s, wd_s):
    i = pl.program_id(0)
    n_used = nu_ref[0]
    n_sub = xbuf.shape[0] // MOE_TILE
    prev = be_ref[jnp.maximum(i - 1, 0)]
    fresh = (i == 0) | (be_ref[i] != prev)

    @pl.when(fresh)
    def _():
        wg_s[...] = wg_ref[...].astype(BF16)
        wu_s[...] = wu_ref[...].astype(BF16)
        wd_s[...] = wd_ref[...].astype(BF16)

    @pl.when(i < n_used)
    def _():
        for r in range(MOE_TILE):
            src = pl.multiple_of(tok_ref[i * MOE_TILE + r] * n_sub, n_sub)
            xbuf[r * n_sub:(r + 1) * n_sub, :] = h_ref[pl.ds(src, n_sub), :]
        words = [xbuf[pl.ds(s, MOE_TILE, stride=n_sub), :] for s in range(n_sub)]
        xb = jnp.concatenate(
            [pltpu.bitcast(w & jnp.uint32(0xFFFF0000), F32).astype(BF16) for w in words]
            + [pltpu.bitcast(w << 16, F32).astype(BF16) for w in words], axis=1)
        hid = _silu(_dot(xb, wg_s[...])) * _dot(xb, wu_s[...])
        y_ref[...] = _dot(hid.astype(BF16), wd_s[...])

    @pl.when(i >= n_used)
    def _():
        y_ref[...] = jnp.zeros(y_ref.shape, F32)


def _moe(h, row_tok, block_expert, n_used, w_gate, w_up, w_down, layer):
    n_tok, n_sub, lanes = h.shape
    d = 2 * n_sub * lanes
    n_rows = row_tok.shape[0]
    de = w_gate.shape[-1]
    return pl.pallas_call(
        _moe_kernel,
        grid_spec=pltpu.PrefetchScalarGridSpec(
            num_scalar_prefetch=3,
            grid=(n_rows // MOE_TILE,),
            in_specs=[pl.BlockSpec((n_tok * n_sub, lanes), lambda i, be, nu, tk: (0, 0),
                                   pipeline_mode=pl.Buffered(1)),
                      pl.BlockSpec((None, None, d, de), lambda i, be, nu, tk: (layer, be[i], 0, 0)),
                      pl.BlockSpec((None, None, d, de), lambda i, be, nu, tk: (layer, be[i], 0, 0)),
                      pl.BlockSpec((None, None, de, d), lambda i, be, nu, tk: (layer, be[i], 0, 0))],
            out_specs=pl.BlockSpec((MOE_TILE, d), lambda i, be, nu, tk: (i, 0)),
            scratch_shapes=[pltpu.VMEM((MOE_TILE * n_sub, lanes), jnp.uint32),
                            pltpu.VMEM((d, de), BF16), pltpu.VMEM((d, de), BF16), pltpu.VMEM((de, d), BF16)]),
        out_shape=jax.ShapeDtypeStruct((n_rows, d), F32),
        compiler_params=pltpu.CompilerParams(dimension_semantics=("arbitrary",),
                                             vmem_limit_bytes=MOE_VMEM_LIMIT_BYTES),
        name="moe_experts",
    )(block_expert, n_used, row_tok, h.reshape(n_tok * n_sub, lanes), w_gate, w_up, w_down)


def _final_kernel(x1_ref, y0_ref, y1_ref, gw_ref, g2p_ref, g2s_ref, nf_ref, op_ref, os_ref, *, last, n_first):
    gw = gw_ref[...]
    f = gw[:, 0:1] * y0_ref[...] + gw[:, 1:2] * y1_ref[...]
    x2 = x1_ref[...] + _pick_rows(n_first, g2p_ref, g2s_ref) * f
    if last:
        x2 = x2 * lax.rsqrt(jnp.mean(x2 * x2, axis=-1, keepdims=True) + NORM_EPS) * nf_ref[...]
    i = pl.program_id(0)

    @pl.when(i < n_first)
    def _():
        op_ref[...] = x2

    @pl.when(i >= n_first)
    def _():
        os_ref[...] = x2


def _final(x1, y0, y1, gw, mods, norm_final, tiles_per_seq, last, n_first):
    m, d = x1.shape
    row = lambda i: (i, 0)
    rs = pl.BlockSpec((ROW_TILE, d), row)
    return pl.pallas_call(
        functools.partial(_final_kernel, last=last, n_first=n_first),
        grid=(m // ROW_TILE,),
        in_specs=[rs, rs, rs, pl.BlockSpec((ROW_TILE, LANES), row)] + _mod_specs(mods, 5, tiles_per_seq) + [
                  pl.BlockSpec((1, d), lambda i: (0, 0))],
        out_specs=_two_group_specs(d, n_first),
        out_shape=[jax.ShapeDtypeStruct((n_first * ROW_TILE, d), F32),
                   jax.ShapeDtypeStruct((m - n_first * ROW_TILE, d), F32)],
        compiler_params=_params("arbitrary"),
        name="combine_final",
    )(x1, y0, y1, gw, *mods, norm_final.reshape(1, d))


def _dispatch(eid):
    m = eid.shape[0]
    n_assign = m * EXPERT_TOPK
    flat_e = eid.reshape(-1)
    flat_tok = jnp.repeat(jnp.arange(m, dtype=I32), EXPERT_TOPK)
    onehot = (flat_e[:, None] == jnp.arange(N_EXPERTS, dtype=I32)[None, :]).astype(I32)
    csum = jnp.cumsum(onehot, axis=0)
    rank = jnp.take_along_axis(csum, flat_e[:, None], axis=1)[:, 0] - 1
    counts = csum[-1]
    padded = (counts + MOE_TILE - 1) // MOE_TILE * MOE_TILE
    pad_end = jnp.cumsum(padded)
    pad_start = pad_end - padded
    dest = (pad_start[flat_e] + rank).astype(I32)
    n_blocks = -(-(n_assign + N_EXPERTS * (MOE_TILE - 1)) // MOE_TILE)
    blk_start = jnp.arange(n_blocks, dtype=I32) * MOE_TILE
    block_expert = jnp.minimum(jnp.sum((pad_end[None, :] <= blk_start[:, None]).astype(I32), axis=1),
                               N_EXPERTS - 1).astype(I32)
    order = jnp.argsort(flat_e, stable=True).astype(I32)
    row_e = jnp.repeat(block_expert, MOE_TILE)
    k = jnp.arange(n_blocks * MOE_TILE, dtype=I32) - pad_start[row_e]
    start = jnp.cumsum(counts) - counts
    pos = jnp.clip(start[row_e] + k, 0, n_assign - 1)
    row_tok = jnp.where(k < counts[row_e], flat_tok[order[pos]], 0).astype(I32)
    n_used = (pad_end[-1:] // MOE_TILE).astype(I32)
    return dest.reshape(m, EXPERT_TOPK), row_tok, block_expert, n_used


def _mod_tiles(mod, n_prompt, reps):
    mod_p = mod[:n_prompt].transpose(1, 0, 2)[:, :, None, :]
    mod_s = jnp.repeat(mod[n_prompt:], reps, axis=0).transpose(1, 0, 2)
    return mod_p, mod_s


def kernel(x_prompt, x_sample, c_prompt, c_sample, cache_k, cache_v, state_hgrn, page_table,
           w_ada, b_ada, norm_mix, norm_ffn, w_in, hgrn_lb_logits, hgrn_norm,
           w_proj_a, w_proj_b, w_out, w_group, b_group, w_expert_router, b_expert_router,
           w_gate, w_up, w_down, norm_final):
    bp, tp, d = x_prompt.shape
    bs, ts, _ = x_sample.shape
    mp_rows, ms_rows = bp * tp, bs * ts
    assert ms_rows == ROW_TILE and tp % ROW_TILE == 0
    m = mp_rows + ms_rows
    tiles_per_seq = tp // ROW_TILE
    n_full = PAST_LEN // MOBA_BLOCK
    assert PAST_LEN % MOBA_BLOCK == 0 and n_full >= MOBA_TOPK

    xp, xs = x_prompt.reshape(mp_rows, d), x_sample.reshape(ms_rows, d)
    c_all = jnp.concatenate([c_prompt, c_sample, jnp.zeros((-(bp + bs) % 8, d), F32)], axis=0)
    lb_all = jnp.cumsum(jax.nn.softmax(hgrn_lb_logits.astype(F32), axis=0), axis=0)
    cache_kt = cache_k.transpose(0, 1, 2, 4, 3)
    cache_vt = cache_v.transpose(0, 1, 2, 4, 3)
    kp_l, vp_l, sp_l, ks_l, vs_l, ss_l = [], [], [], [], [], []

    for l in range(DEPTH):
        mod = _modulation(c_all, w_ada[l], b_ada[l])[:bp + bs].reshape(bp + bs, 6, d)
        mt = _mod_tiles(mod, bp, ts)
        za, zb, zg = _inproj(xp, xs, mt, norm_mix[l], w_in[l], tiles_per_seq)

        oa_p, st_p = _hgrn(za, lb_all[l], hgrn_norm[l], 0, bp, tp, ROW_TILE, HGRN_CHUNK)
        oa_s, st_s = _hgrn(za, lb_all[l], hgrn_norm[l], mp_rows, bs, ts, ts, ts, s0=state_hgrn[l])

        q_p, k_p, v_p = _rope_split(zb, 0, bp, tp, 512, jnp.arange(tp))
        q_s, k_s, v_s = _rope_split(zb, mp_rows, bs, ts, ts, PAST_LEN + jnp.arange(ts))
        ob_p = _moba_prompt(q_p, k_p, v_p)
        sel = _select_blocks(q_s, cache_kt, page_table[:, :n_full * PAGES_PER_BLOCK], l)[..., :MOBA_TOPK]
        ob_s = _moba_sample(q_s, k_s, v_s, cache_kt, cache_vt, sel, page_table, l)
        ob_s = ob_s.transpose(0, 2, 1, 3).reshape(ms_rows, B_WIDTH).astype(BF16)

        w_router = jnp.concatenate(
            [w_group[l], w_expert_router[l].transpose(1, 0, 2).reshape(d, N_EXPERTS),
             jnp.zeros((d, LANES - N_GROUPS - N_EXPERTS), F32)], axis=1)
        b_router = jnp.concatenate(
            [b_group[l], b_expert_router[l].reshape(-1),
             jnp.zeros((LANES - N_GROUPS - N_EXPERTS,), F32)]).reshape(1, LANES)
        x1, h2, eid, gw = _merge(xp, xs, oa_p, oa_s, ob_p, ob_s, zg, mt, norm_ffn[l], w_proj_a[l].astype(BF16),
                                 w_proj_b[l].astype(BF16), w_out[l].astype(BF16), w_router, b_router,
                                 tiles_per_seq)

        dest, row_tok, block_expert, n_used = _dispatch(eid[:, :EXPERT_TOPK])
        ys = _moe(h2, row_tok, block_expert, n_used, w_gate, w_up, w_down, l)
        xp, xs = _final(x1, ys[dest[:, 0]], ys[dest[:, 1]], gw, mt, norm_final, tiles_per_seq,
                        l == DEPTH - 1, mp_rows // ROW_TILE)

        kp_l.append(k_p); vp_l.append(v_p); sp_l.append(st_p)
        ks_l.append(k_s); vs_l.append(v_s); ss_l.append(st_s)

    y_prompt = xp.reshape(bp, tp, d)
    y_sample = xs.reshape(bs, ts, d)
    return (y_prompt, y_sample, jnp.stack(kp_l), jnp.stack(vp_l), jnp.stack(sp_l),
            jnp.stack(ks_l), jnp.stack(vs_l), jnp.stack(ss_l))
```
